```python
import math
import jax
import jax.numpy as jnp
from jax import lax
import numpy as np

D_MODEL = 4096
BATCH = 1
SEQ = 16384
DEPTH = 2

CHUNK = 64
D_MIX = D_MODEL
W_A = D_MIX // 4
A_HEAD_DIM = 128
A_HEADS = W_A // A_HEAD_DIM
W_B = D_MIX // 4
S5_GROUP = 16
S5_GROUPS = W_B // S5_GROUP
S5_STATE = 64
W_C = D_MIX // 4
C_HEADS = 4
C_HEAD_DIM = W_C // C_HEADS
ROPE_THETA = 10000.0
W_D = D_MIX - W_A - W_B - W_C
D_BLOCKS = 8
D_BLOCK = W_D // D_BLOCKS
CONV_WIDTH = 4
RG_C = 8.0
D_FF = 14336
N_EXPERTS = 8
TOP_K = 2
D_FF_EXPERT = 4096
N_DENSE = (DEPTH + 1) // 2
N_MOE = DEPTH // 2
EPS = 1e-6
IN_SPLITS = (W_A, W_A, W_A, W_A, W_B, W_C, W_C, W_C, W_C, W_D, W_D)
D_IN = 4 * W_A + W_B + 4 * W_C + 2 * W_D

kernel_name = 'hybrid_parallel_mixer_block'


def rms_norm(x, g):
    xf = x.astype(jnp.float32)
    y = xf * lax.rsqrt(jnp.mean(xf * xf, axis=-1, keepdims=True) + EPS)
    return (y * g.astype(jnp.float32)).astype(x.dtype)


def to_chunks(t):
    b_, s_ = t.shape[:2]
    return jnp.moveaxis(t.reshape(b_, s_ // CHUNK, CHUNK, *t.shape[2:]), 1, 0)


def from_chunks(t):
    nc, b_, l_ = t.shape[:3]
    return jnp.moveaxis(t, 0, 1).reshape(b_, nc * l_, *t.shape[3:])


def hgrn2_mixer(zq, zf, zi, zg, lb, norm_g):
    b_, s_, _ = zq.shape
    hd = (b_, s_, A_HEADS, A_HEAD_DIM)
    f32 = jnp.float32
    q = jax.nn.silu(zq.astype(f32)).reshape(hd)
    f = lb + (1.0 - lb) * jax.nn.sigmoid(zf.astype(f32))
    log_f = jnp.log(f).reshape(hd)
    k = (1.0 - f).reshape(hd)
    v = zi.astype(f32).reshape(hd)
    causal = jnp.tril(jnp.ones((CHUNK, CHUNK), dtype=bool))[None, :, :, None, None]

    def step(state, inp):
        qc, kc, vc, lfc = inp
        cum = jnp.cumsum(lfc, axis=1)
        rel = jnp.where(causal, cum[:, :, None] - cum[:, None, :], -jnp.inf)
        scores = jnp.einsum('bthd,bshd,btshd->bhts', qc, kc, jnp.exp(rel))
        o = (jnp.einsum('bhts,bshv->bthv', scores, vc)
             + jnp.einsum('bthd,bhdv->bthv', qc * jnp.exp(cum), state))
        last = cum[:, -1]
        state = (jnp.exp(last)[..., None] * state
                 + jnp.einsum('bshd,bshv->bhdv', kc * jnp.exp(last[:, None] - cum), vc))
        return state, o

    state0 = jnp.zeros((b_, A_HEADS, A_HEAD_DIM, A_HEAD_DIM), f32)
    _, o = lax.scan(step, state0, (to_chunks(q), to_chunks(k), to_chunks(v), to_chunks(log_f)))
    o = from_chunks(o)
    o = o * lax.rsqrt(jnp.mean(o * o, axis=-1, keepdims=True) + EPS)
    return o.reshape(b_, s_, W_A) * norm_g.astype(f32) * jax.nn.silu(zg.astype(f32))


def _complex_affine_combine(e1, e2):
    a1r, a1i, b1r, b1i = e1
    a2r, a2i, b2r, b2i = e2
    return (a2r * a1r - a2i * a1i,
            a2r * a1i + a2i * a1r,
            a2r * b1r - a2i * b1i + b2r,
            a2r * b1i + a2i * b1r + b2i)


def s5_mixer(u, lam_re, lam_im, log_dt, b_re, b_im, c_re, c_im, d_skip, glu_w, glu_b):
    f32 = jnp.float32
    b_, s_, _ = u.shape
    uf = u.astype(f32)
    lam_re = lam_re.astype(f32)
    lam_im = lam_im.astype(f32)
    dt = jnp.exp(log_dt.astype(f32))[:, None]
    mag = jnp.exp(lam_re * dt)
    ang = lam_im * dt
    ab_re = mag * jnp.cos(ang)
    ab_im = mag * jnp.sin(ang)
    den = lam_re * lam_re + lam_im * lam_im
    num_re = ab_re - 1.0
    coef_re = (num_re * lam_re + ab_im * lam_im) / den
    coef_im = (ab_im * lam_re - num_re * lam_im) / den
    br = b_re.astype(f32)
    bi = b_im.astype(f32)
    bb_re = coef_re[..., None] * br - coef_im[..., None] * bi
    bb_im = coef_re[..., None] * bi + coef_im[..., None] * br
    cr = c_re.astype(f32)
    ci = c_im.astype(f32)
    ug = uf.reshape(b_, s_, S5_GROUPS, S5_GROUP)

    def step(carry, uc):
        s_re, s_im = carry
        bu_re = jnp.einsum('blgp,gnp->blgn', uc, bb_re)
        bu_im = jnp.einsum('blgp,gnp->blgn', uc, bb_im)
        bu_re = bu_re.at[:, 0].add(ab_re * s_re - ab_im * s_im)
        bu_im = bu_im.at[:, 0].add(ab_re * s_im + ab_im * s_re)
        a_re = jnp.broadcast_to(ab_re, bu_re.shape)
        a_im = jnp.broadcast_to(ab_im, bu_im.shape)
        _, _, x_re, x_im = lax.associative_scan(_complex_affine_combine, (a_re, a_im, bu_re, bu_im), axis=1)
        y = jnp.einsum('blgn,gpn->blgp', x_re, cr) - jnp.einsum('blgn,gpn->blgp', x_im, ci)
        return (x_re[:, -1], x_im[:, -1]), y

    carry0 = (jnp.zeros((b_, S5_GROUPS, S5_STATE), f32), jnp.zeros((b_, S5_GROUPS, S5_STATE), f32))
    _, y = lax.scan(step, carry0, to_chunks(ug))
    y = from_chunks(y).reshape(b_, s_, W_B) + d_skip.astype(f32) * uf
    y = jax.nn.gelu(y)
    return y * jax.nn.sigmoid(y @ glu_w.astype(f32) + glu_b.astype(f32))


def rope(t, cos, sin):
    half = t.shape[-1] // 2
    t1, t2 = t[..., :half], t[..., half:]
    return jnp.concatenate([t1 * cos - t2 * sin, t1 * sin + t2 * cos], axis=-1)


def retention_mixer(zq, zk, zv, zg, norm_g):
    f32 = jnp.float32
    b_, s_, _ = zq.shape
    hd = (b_, s_, C_HEADS, C_HEAD_DIM)
    pos = jnp.arange(s_, dtype=f32)
    inv_freq = ROPE_THETA ** (-jnp.arange(0, C_HEAD_DIM, 2, dtype=f32) / C_HEAD_DIM)
    ang = pos[:, None] * inv_freq[None, :]
    cos = jnp.cos(ang)[None, :, None, :]
    sin = jnp.sin(ang)[None, :, None, :]
    q = rope(zq.astype(f32).reshape(hd), cos, sin)
    k = rope(zk.astype(f32).reshape(hd), cos, sin) * (C_HEAD_DIM ** -0.5)
    v = zv.astype(f32).reshape(hd)
    log_gamma = jnp.log(1.0 - 2.0 ** (-5.0 - jnp.arange(C_HEADS, dtype=f32)))
    idx = jnp.arange(CHUNK, dtype=f32)
    rel = idx[:, None] - idx[None, :]
    dmat = jnp.where(rel[None] >= 0, jnp.exp(jnp.maximum(rel, 0.0)[None] * log_gamma[:, None, None]), 0.0)
    q_dec = jnp.exp((idx + 1.0)[:, None] * log_gamma[None, :])[None, :, :, None]
    k_dec = jnp.exp((CHUNK - 1.0 - idx)[:, None] * log_gamma[None, :])[None, :, :, None]
    c_dec = jnp.exp(CHUNK * log_gamma)[None, :, None, None]

    def step(state, inp):
        qc, kc, vc = inp
        scores = jnp.einsum('bthd,bshd->bhts', qc, kc) * dmat[None]
        o = (jnp.einsum('bhts,bshv->bthv', scores, vc)
             + jnp.einsum('bthd,bhdv->bthv', qc, state) * q_dec)
        state = c_dec * state + jnp.einsum('bshd,bshv->bhdv', kc * k_dec, vc)
        return state, o

    state0 = jnp.zeros((b_, C_HEADS, C_HEAD_DIM, C_HEAD_DIM), f32)
    _, o = lax.scan(step, state0, (to_chunks(q), to_chunks(k), to_chunks(v)))
    o = from_chunks(o)
    mu = jnp.mean(o, axis=-1, keepdims=True)
    var = jnp.mean(jnp.square(o - mu), axis=-1, keepdims=True)
    o = (o - mu) * lax.rsqrt(var + EPS)
    return o.reshape(b_, s_, W_C) * norm_g.astype(f32) * jax.nn.silu(zg.astype(f32))


def _real_affine_combine(e1, e2):
    a1, b1 = e1
    a2, b2 = e2
    return a2 * a1, a2 * b1 + b2


def rglru_mixer(z_gate, z_x, conv_w, conv_b, w_a, b_a, w_x, b_x, lam):
    f32 = jnp.float32
    b_, s_, _ = z_x.shape
    xf = z_x.astype(f32)
    cw = conv_w.astype(f32)
    xp = jnp.pad(xf, ((0, 0), (CONV_WIDTH - 1, 0), (0, 0)))
    xc = conv_b.astype(f32)
    for tap in range(CONV_WIDTH):
        xc = xc + xp[:, tap:tap + s_] * cw[tap]
    xb = xc.reshape(b_, s_, D_BLOCKS, D_BLOCK)
    r = jax.nn.sigmoid(jnp.einsum('bshi,hij->bshj', xb, w_a.astype(f32)).reshape(b_, s_, W_D) + b_a.astype(f32))
    i = jax.nn.sigmoid(jnp.einsum('bshi,hij->bshj', xb, w_x.astype(f32)).reshape(b_, s_, W_D) + b_x.astype(f32))
    log_a = -RG_C * r * jax.nn.softplus(-lam.astype(f32))
    a = jnp.exp(log_a)
    u = jnp.sqrt(-jnp.expm1(2.0 * log_a)) * (i * xc)
    _, h = lax.associative_scan(_real_affine_combine, (a, u), axis=1)
    return jax.nn.gelu(z_gate.astype(f32)) * h


def swiglu(h, w1, w3, w2):
    return (jax.nn.silu(h @ w1) * (h @ w3)) @ w2


def moe_swiglu(h, router_w, w1, w3, w2):
    b_, s_, d_ = h.shape
    t = h.reshape(b_ * s_, d_)
    logits = (t @ router_w).astype(jnp.float32)
    top_v, top_i = lax.top_k(logits, TOP_K)
    gates = jax.nn.softmax(top_v, axis=-1)
    combine = jnp.einsum('tk,tke->te', gates, jax.nn.one_hot(top_i, N_EXPERTS, dtype=jnp.float32)).astype(t.dtype)
    out = jnp.zeros_like(t)
    for e in range(N_EXPERTS):
        out = out + combine[:, e:e + 1] * swiglu(t, w1[e], w3[e], w2[e])
    return out.reshape(b_, s_, d_)


def setup_inputs(seed: int = 0) -> dict:
    key = jax.random.key(seed)
    keys = list(jax.random.split(key, 48))
    f32 = jnp.float32

    def nrm(shape, scale):
        return jax.random.normal(keys.pop(), shape, f32) * scale

    def gain(shape):
        return 1.0 + nrm(shape, 0.02)

    n = jnp.arange(S5_STATE, dtype=f32)
    rg_u = jax.random.uniform(keys.pop(), (DEPTH, W_D), f32, 0.9, 0.999)
    rg_s = rg_u ** (1.0 / RG_C)
    return {
        'x': nrm((BATCH, SEQ, D_MODEL), 1.0),
        'norm_mix_g': gain((DEPTH, D_MODEL)),
        'norm_ffn_g': gain((DEPTH, D_MODEL)),
        'final_norm_g': gain((D_MODEL,)),
        'w_in': nrm((DEPTH, D_MODEL, D_IN), D_MODEL ** -0.5),
        'w_out': nrm((DEPTH, D_MIX, D_MODEL), D_MIX ** -0.5),
        'hgrn_lb_logits': nrm((DEPTH, W_A), 0.5),
        'hgrn_norm_g': gain((DEPTH, W_A)),
        's5_lambda_re': -0.5 + nrm((DEPTH, S5_GROUPS, S5_STATE), 0.01),
        's5_lambda_im': math.pi * n + nrm((DEPTH, S5_GROUPS, S5_STATE), 0.01),
        's5_log_dt': jax.random.uniform(keys.pop(), (DEPTH, S5_GROUPS), f32, math.log(1e-3), math.log(1e-1)),
        's5_b_re': nrm((DEPTH, S5_GROUPS, S5_STATE, S5_GROUP), (2.0 * S5_GROUP) ** -0.5),
        's5_b_im': nrm((DEPTH, S5_GROUPS, S5_STATE, S5_GROUP), (2.0 * S5_GROUP) ** -0.5),
        's5_c_re': nrm((DEPTH, S5_GROUPS, S5_GROUP, S5_STATE), (2.0 * S5_STATE) ** -0.5),
        's5_c_im': nrm((DEPTH, S5_GROUPS, S5_GROUP, S5_STATE), (2.0 * S5_STATE) ** -0.5),
        's5_d': nrm((DEPTH, W_B), 0.5),
        's5_glu_w': nrm((DEPTH, W_B, W_B), W_B ** -0.5),
        's5_glu_b': nrm((DEPTH, W_B), 0.01),
        'ret_norm_g': gain((DEPTH, W_C)),
        'rg_conv_w': nrm((DEPTH, CONV_WIDTH, W_D), CONV_WIDTH ** -0.5),
        'rg_conv_b': nrm((DEPTH, W_D), 0.01),
        'rg_w_a': nrm((DEPTH, D_BLOCKS, D_BLOCK, D_BLOCK), D_BLOCK ** -0.5),
        'rg_b_a': nrm((DEPTH, W_D), 0.01),
        'rg_w_x': nrm((DEPTH, D_BLOCKS, D_BLOCK, D_BLOCK), D_BLOCK ** -0.5),
        'rg_b_x': nrm((DEPTH, W_D), 0.01),
        'rg_lambda': jnp.log(rg_s) - jnp.log1p(-rg_s),
        'ffn_w1': nrm((N_DENSE, D_MODEL, D_FF), D_MODEL ** -0.5),
        'ffn_w3': nrm((N_DENSE, D_MODEL, D_FF), D_MODEL ** -0.5),
        'ffn_w2': nrm((N_DENSE, D_FF, D_MODEL), D_FF ** -0.5),
        'router_w': nrm((N_MOE, D_MODEL, N_EXPERTS), D_MODEL ** -0.5),
        'moe_w1': nrm((N_MOE, N_EXPERTS, D_MODEL, D_FF_EXPERT), D_MODEL ** -0.5),
        'moe_w3': nrm((N_MOE, N_EXPERTS, D_MODEL, D_FF_EXPERT), D_MODEL ** -0.5),
        'moe_w2': nrm((N_MOE, N_EXPERTS, D_FF_EXPERT, D_MODEL), D_FF_EXPERT ** -0.5),
    }


def reference(x, norm_mix_g, norm_ffn_g, final_norm_g, w_in, w_out,
              hgrn_lb_logits, hgrn_norm_g,
              s5_lambda_re, s5_lambda_im, s5_log_dt, s5_b_re, s5_b_im, s5_c_re, s5_c_im,
              s5_d, s5_glu_w, s5_glu_b,
              ret_norm_g,
              rg_conv_w, rg_conv_b, rg_w_a, rg_b_a, rg_w_x, rg_b_x, rg_lambda,
              ffn_w1, ffn_w3, ffn_w2,
              router_w, moe_w1, moe_w3, moe_w2):
    lb_p = jax.nn.softmax(hgrn_lb_logits.astype(jnp.float32), axis=0)
    lb_all = jnp.cumsum(lb_p, axis=0) - lb_p[0]
    split_points = [int(p) for p in np.cumsum(IN_SPLITS)[:-1]]
    for layer in range(DEPTH):
        h = rms_norm(x, norm_mix_g[layer])
        z = h @ w_in[layer]
        (a_q, a_f, a_i, a_g, b_u, c_q, c_k, c_v, c_g, d_gate, d_x) = jnp.split(z, split_points, axis=-1)
        o_a = hgrn2_mixer(a_q, a_f, a_i, a_g, lb_all[layer], hgrn_norm_g[layer])
        o_b = s5_mixer(b_u, s5_lambda_re[layer], s5_lambda_im[layer], s5_log_dt[layer],
                       s5_b_re[layer], s5_b_im[layer], s5_c_re[layer], s5_c_im[layer],
                       s5_d[layer], s5_glu_w[layer], s5_glu_b[layer])
        o_c = retention_mixer(c_q, c_k, c_v, c_g, ret_norm_g[layer])
        o_d = rglru_mixer(d_gate, d_x, rg_conv_w[layer], rg_conv_b[layer], rg_w_a[layer], rg_b_a[layer],
                          rg_w_x[layer], rg_b_x[layer], rg_lambda[layer])
        mix = jnp.concatenate([o_a, o_b, o_c, o_d], axis=-1).astype(x.dtype)
        x = x + mix @ w_out[layer]
        h = rms_norm(x, norm_ffn_g[layer])
        if layer % 2 == 0:
            x = x + swiglu(h, ffn_w1[layer // 2], ffn_w3[layer // 2], ffn_w2[layer // 2])
        else:
            x = x + moe_swiglu(h, router_w[layer // 2], moe_w1[layer // 2], moe_w3[layer // 2], moe_w2[layer // 2])
    return rms_norm(x, final_norm_g)
```

```python
import functools
import math

import jax
import jax.numpy as jnp
from jax import lax
from jax.experimental import pallas as pl
from jax.experimental.pallas import tpu as pltpu

F32 = jnp.float32
BF16 = jnp.bfloat16
EPS = 1e-6
HIGHEST = lax.Precision.HIGHEST

V7X_VMEM_BYTES = 64 * 1024 * 1024
VMEM_LIMIT_BYTES = V7X_VMEM_BYTES - 8 * 1024 * 1024
SUBLANES = 8
LANES = 128

A_HEAD_DIM = 128
S5_GROUP = 16
S5_STATE = 64
C_HEADS = 4
ROPE_THETA = 10000.0
D_BLOCK = 128
CONV_WIDTH = 4
RG_C = 8.0
N_EXPERTS = 8
W_MIX = 1024

ROW_TILE = 512
IN_PROJ_TN = 512
OUT_PROJ_TN = 512
FFN_TF = 256
HGRN_T = 256
HGRN_C = 16
RET_T = 256
S5_T = 256
S5_SLAB = 128
RG_T = 256


def _cparams(*sem):
    return pltpu.CompilerParams(dimension_semantics=sem, vmem_limit_bytes=VMEM_LIMIT_BYTES)


def _rms(xf, g):
    return xf * lax.rsqrt(jnp.mean(xf * xf, axis=-1, keepdims=True) + EPS) * g


def _sigmoid(x):
    return 1.0 / (1.0 + jnp.exp(-x))


def _silu(x):
    return x * _sigmoid(x)


def _gelu_tanh(x):
    c = math.sqrt(2.0 / math.pi)
    return 0.5 * x * (1.0 + jnp.tanh(c * (x + 0.044715 * (x * x * x))))


def _norm_matmul_kernel(x_ref, g_ref, w_ref, o_ref, h_ref):
    @pl.when(pl.program_id(1) == 0)
    def _():
        h_ref[...] = _rms(x_ref[...], g_ref[...]).astype(BF16)

    o_ref[...] = jnp.dot(h_ref[...], w_ref[...], preferred_element_type=F32)


def norm_matmul(x, g, w):
    s, d = x.shape
    n = w.shape[1]
    tm, tn = min(ROW_TILE, s), IN_PROJ_TN
    return pl.pallas_call(
        _norm_matmul_kernel,
        out_shape=jax.ShapeDtypeStruct((s, n), F32),
        grid=(s // tm, n // tn),
        in_specs=[
            pl.BlockSpec((tm, d), lambda i, j: (i, 0)),
            pl.BlockSpec((1, d), lambda i, j: (0, 0)),
            pl.BlockSpec((d, tn), lambda i, j: (0, j)),
        ],
        out_specs=pl.BlockSpec((tm, tn), lambda i, j: (i, j)),
        scratch_shapes=[pltpu.VMEM((tm, d), BF16)],
        compiler_params=_cparams("parallel", "arbitrary"),
        name="norm_in_proj",
    )(x, g.reshape(1, d), w)


def _out_proj_kernel(oa_ref, ob_ref, oc_ref, od_ref, w_ref, x_ref, o_ref):
    acc = x_ref[...]
    for idx, r in enumerate((oa_ref, ob_ref, oc_ref, od_ref)):
        acc = acc + jnp.dot(r[...], w_ref[idx * W_MIX:(idx + 1) * W_MIX, :], preferred_element_type=F32)
    o_ref[...] = acc


def out_proj(parts, w, x):
    s, d = x.shape
    tm, tn = min(ROW_TILE, s), OUT_PROJ_TN
    part_spec = pl.BlockSpec((tm, W_MIX), lambda i, j: (i, 0))
    return pl.pallas_call(
        _out_proj_kernel,
        out_shape=jax.ShapeDtypeStruct((s, d), F32),
        grid=(s // tm, d // tn),
        in_specs=[part_spec, part_spec, part_spec, part_spec,
                  pl.BlockSpec((4 * W_MIX, tn), lambda i, j: (0, j)),
                  pl.BlockSpec((tm, tn), lambda i, j: (i, j))],
        out_specs=pl.BlockSpec((tm, tn), lambda i, j: (i, j)),
        compiler_params=_cparams("parallel", "arbitrary"),
        name="out_proj",
    )(*parts, w, x)


def _ffn_kernel(x_ref, g_ref, w1_ref, w3_ref, w2_ref, o_ref, h_ref):
    @pl.when(pl.program_id(1) == 0)
    def _():
        xf = x_ref[...]
        h_ref[...] = _rms(xf, g_ref[...]).astype(BF16)
        o_ref[...] = xf

    h = h_ref[...]
    a = jnp.dot(h, w1_ref[...], preferred_element_type=F32)
    b = jnp.dot(h, w3_ref[...], preferred_element_type=F32)
    act = (_silu(a) * b).astype(BF16)
    o_ref[...] += jnp.dot(act, w2_ref[...], preferred_element_type=F32)


def ffn(x, g, w1, w3, w2):
    s, d = x.shape
    dff = w1.shape[1]
    tm, tf = min(ROW_TILE, s), FFN_TF
    return pl.pallas_call(
        _ffn_kernel,
        out_shape=jax.ShapeDtypeStruct((s, d), F32),
        grid=(s // tm, dff // tf),
        in_specs=[
            pl.BlockSpec((tm, d), lambda i, f: (i, 0), pipeline_mode=pl.Buffered(1)),
            pl.BlockSpec((1, d), lambda i, f: (0, 0)),
            pl.BlockSpec((d, tf), lambda i, f: (0, f)),
            pl.BlockSpec((d, tf), lambda i, f: (0, f)),
            pl.BlockSpec((tf, d), lambda i, f: (f, 0)),
        ],
        out_specs=pl.BlockSpec((tm, d), lambda i, f: (i, 0)),
        scratch_shapes=[pltpu.VMEM((tm, d), BF16)],
        compiler_params=_cparams("parallel", "arbitrary"),
        name="ffn_swiglu",
    )(x, g.reshape(1, d), w1, w3, w2)


def _router_kernel(x_ref, g_ref, wr_ref, comb_ref):
    h = _rms(x_ref[...], g_ref[...])
    logits = jnp.dot(h, wr_ref[...], precision=HIGHEST, preferred_element_type=F32)
    lane = lax.broadcasted_iota(jnp.int32, logits.shape, 1)
    neg = jnp.float32(-jnp.inf)
    logits = jnp.where(lane < N_EXPERTS, logits, neg)
    v1 = jnp.max(logits, axis=-1, keepdims=True)
    i1 = jnp.min(jnp.where(logits == v1, lane, LANES), axis=-1, keepdims=True)
    rest = jnp.where(lane == i1, neg, logits)
    v2 = jnp.max(rest, axis=-1, keepdims=True)
    i2 = jnp.min(jnp.where(rest == v2, lane, LANES), axis=-1, keepdims=True)
    e2 = jnp.exp(v2 - v1)
    g1 = 1.0 / (1.0 + e2)
    g2 = e2 / (1.0 + e2)
    comb_ref[...] = jnp.where(lane == i1, g1, 0.0) + jnp.where(lane == i2, g2, 0.0)


def router(x, g, router_w):
    s, d = x.shape
    tm = min(ROW_TILE, s)
    wr = jnp.zeros((d, LANES), F32).at[:, :N_EXPERTS].set(router_w.astype(F32))
    return pl.pallas_call(
        _router_kernel,
        out_shape=jax.ShapeDtypeStruct((s, LANES), F32),
        grid=(s // tm,),
        in_specs=[pl.BlockSpec((tm, d), lambda i: (i, 0)),
                  pl.BlockSpec((1, d), lambda i: (0, 0)),
                  pl.BlockSpec((d, LANES), lambda i: (0, 0))],
        out_specs=pl.BlockSpec((tm, LANES), lambda i: (i, 0)),
        compiler_params=_cparams("parallel"),
        name="moe_router",
    )(x, g.reshape(1, d), wr)


def _moe_dense_kernel(x_ref, g_ref, comb_ref, w1_ref, w3_ref, w2_ref, o_ref, h_ref):
    e = pl.program_id(1)

    @pl.when((e == 0) & (pl.program_id(2) == 0))
    def _():
        xf = x_ref[...]
        h_ref[...] = _rms(xf, g_ref[...]).astype(BF16)
        o_ref[...] = xf

    comb = comb_ref[...]
    lane = lax.broadcasted_iota(jnp.int32, comb.shape, 1)
    ce = jnp.sum(jnp.where(lane == e, comb, 0.0), axis=-1, keepdims=True)
    h = h_ref[...]
    a = jnp.dot(h, w1_ref[0], preferred_element_type=F32)
    b = jnp.dot(h, w3_ref[0], preferred_element_type=F32)
    act = (_silu(a) * b).astype(BF16)
    o_ref[...] += ce * jnp.dot(act, w2_ref[0], preferred_element_type=F32)


def moe_dense(x, g, comb, w1, w3, w2):
    s, d = x.shape
    ne, _, dff = w1.shape
    tm, tf = min(ROW_TILE, s), FFN_TF
    return pl.pallas_call(
        _moe_dense_kernel,
        out_shape=jax.ShapeDtypeStruct((s, d), F32),
        grid=(s // tm, ne, dff // tf),
        in_specs=[
            pl.BlockSpec((tm, d), lambda i, e, f: (i, 0), pipeline_mode=pl.Buffered(1)),
            pl.BlockSpec((1, d), lambda i, e, f: (0, 0)),
            pl.BlockSpec((tm, LANES), lambda i, e, f: (i, 0)),
            pl.BlockSpec((1, d, tf), lambda i, e, f: (e, 0, f)),
            pl.BlockSpec((1, d, tf), lambda i, e, f: (e, 0, f)),
            pl.BlockSpec((1, tf, d), lambda i, e, f: (e, f, 0)),
        ],
        out_specs=pl.BlockSpec((tm, d), lambda i, e, f: (i, 0)),
        scratch_shapes=[pltpu.VMEM((tm, d), BF16)],
        compiler_params=_cparams("parallel", "arbitrary", "arbitrary"),
        name="moe_dense",
    )(x, g.reshape(1, d), comb, w1, w3, w2)


def moe(x, g, router_w, w1, w3, w2):
    comb = router(x, g, router_w)
    return moe_dense(x, g, comb, w1.astype(BF16), w3.astype(BF16), w2.astype(BF16))


def _final_norm_kernel(x_ref, g_ref, o_ref):
    o_ref[...] = _rms(x_ref[...], g_ref[...])


def final_norm(x, g):
    s, d = x.shape
    tm = min(ROW_TILE, s)
    return pl.pallas_call(
        _final_norm_kernel,
        out_shape=jax.ShapeDtypeStruct((s, d), F32),
        grid=(s // tm,),
        in_specs=[pl.BlockSpec((tm, d), lambda i: (i, 0)), pl.BlockSpec((1, d), lambda i: (0, 0))],
        out_specs=pl.BlockSpec((tm, d), lambda i: (i, 0)),
        compiler_params=_cparams("parallel"),
        name="final_norm",
    )(x, g.reshape(1, d))


def _hgrn2_kernel(zq_ref, zf_ref, zi_ref, zg_ref, lb_ref, ng_ref, tri_ref, o_ref,
                  st_ref, q_s, k_s, v_s, cum_s, o_s):
    c_len = HGRN_C
    t_len = q_s.shape[0]

    @pl.when(pl.program_id(1) == 0)
    def _():
        st_ref[...] = jnp.zeros_like(st_ref)

    lb = lb_ref[...]
    f = lb + (1.0 - lb) * _sigmoid(zf_ref[...])
    cum_s[...] = jnp.dot(tri_ref[...], jnp.log(f), precision=HIGHEST, preferred_element_type=F32)
    q_s[...] = _silu(zq_ref[...])
    k_s[...] = 1.0 - f
    v_s[...] = zi_ref[...]
    row = lax.broadcasted_iota(jnp.int32, (c_len, A_HEAD_DIM), 0)

    def body(c, carry):
        r0 = pl.multiple_of(c * c_len, c_len)
        q = q_s[pl.ds(r0, c_len), :]
        k = k_s[pl.ds(r0, c_len), :]
        v = v_s[pl.ds(r0, c_len), :]
        cm = cum_s[pl.ds(r0, c_len), :]
        rows = []
        for t in range(c_len):
            dec = jnp.where(row <= t, jnp.exp(jnp.minimum(cm[t:t + 1, :] - cm, 0.0)), 0.0)
            w = (q[t:t + 1, :] * k) * dec
            sc = jnp.sum(w, axis=1, keepdims=True)
            rows.append(jnp.sum(sc * v, axis=0, keepdims=True))
        o_intra = jnp.concatenate(rows, axis=0)
        st = st_ref[...]
        qd = (q * jnp.exp(cm)).astype(BF16)
        o_inter = lax.dot_general(qd, st.astype(BF16), (((1,), (1,)), ((), ())),
                                  preferred_element_type=F32)
        o_s[pl.ds(r0, c_len), :] = o_intra + o_inter
        last = cm[c_len - 1:c_len, :]
        kd = (k * jnp.exp(last - cm)).astype(BF16)
        upd = lax.dot_general(v.astype(BF16), kd, (((0,), (0,)), ((), ())),
                              preferred_element_type=F32)
        st_ref[...] = st * jnp.exp(last) + upd
        return carry

    lax.fori_loop(0, t_len // c_len, body, 0)
    o = o_s[...]
    o = o * lax.rsqrt(jnp.mean(o * o, axis=-1, keepdims=True) + EPS)
    o_ref[...] = (o * ng_ref[...] * _silu(zg_ref[...])).astype(BF16)


def hgrn2(z, col0, lb, norm_g):
    s = z.shape[0]
    t_len = min(HGRN_T, s)
    heads = W_MIX // A_HEAD_DIM
    cb = col0 // A_HEAD_DIM
    r = jnp.arange(t_len)
    tri = ((r[:, None] // HGRN_C == r[None, :] // HGRN_C) & (r[None, :] <= r[:, None])).astype(F32)

    def zspec(k):
        return pl.BlockSpec((t_len, A_HEAD_DIM), lambda h, i: (i, cb + k * heads + h))

    vec = pl.BlockSpec((1, A_HEAD_DIM), lambda h, i: (0, h))
    return pl.pallas_call(
        _hgrn2_kernel,
        out_shape=jax.ShapeDtypeStruct((s, W_MIX), BF16),
        grid=(heads, s // t_len),
        in_specs=[zspec(0), zspec(1), zspec(2), zspec(3), vec, vec,
                  pl.BlockSpec((t_len, t_len), lambda h, i: (0, 0))],
        out_specs=pl.BlockSpec((t_len, A_HEAD_DIM), lambda h, i: (i, h)),
        scratch_shapes=[pltpu.VMEM((A_HEAD_DIM, A_HEAD_DIM), F32)]
        + [pltpu.VMEM((t_len, A_HEAD_DIM), F32)] * 5,
        compiler_params=_cparams("parallel", "arbitrary"),
        name="hgrn2",
    )(z, z, z, z, lb.reshape(1, W_MIX), norm_g.reshape(1, W_MIX), tri)


def _s5_kernel(u_ref, wb_ref, pw_ref, wc_ref, d_ref, gw_ref, gb_ref, o_ref,
               carry_ref, bu_s, y_s):
    i = pl.program_id(0)
    j = pl.program_id(1)
    n_slab = pl.num_programs(1)
    t_len = bu_s.shape[0]
    half = bu_s.shape[1] // 2

    @pl.when(i == 0)
    def _():
        carry_ref[j] = jnp.zeros((SUBLANES, 2 * half), F32)

    u = u_ref[...]
    bu_s[...] = jnp.dot(u.astype(BF16), wb_ref[0], preferred_element_type=F32)
    pw = pw_ref[0]
    p8_re, p8_im = pw[0:8, :half], pw[0:8, half:]
    row = lax.broadcasted_iota(jnp.int32, (SUBLANES, half), 0)

    def body(b, carry):
        c_re, c_im = carry
        r0 = pl.multiple_of(b * SUBLANES, SUBLANES)
        blk = bu_s[pl.ds(r0, SUBLANES), :]
        x_re, x_im = blk[:, :half], blk[:, half:]
        for step, k in enumerate((1, 2, 4)):
            a_re = pw[8 + step:9 + step, :half]
            a_im = pw[8 + step:9 + step, half:]
            s_re = jnp.where(row >= k, pltpu.roll(x_re, k, 0), 0.0)
            s_im = jnp.where(row >= k, pltpu.roll(x_im, k, 0), 0.0)
            x_re, x_im = (x_re + a_re * s_re - a_im * s_im,
                          x_im + a_re * s_im + a_im * s_re)
        x_re, x_im = (x_re + p8_re * c_re - p8_im * c_im,
                      x_im + p8_re * c_im + p8_im * c_re)
        bu_s[pl.ds(r0, SUBLANES), :] = jnp.concatenate([x_re, x_im], axis=1)
        n_re = jnp.broadcast_to(x_re[SUBLANES - 1:SUBLANES, :], x_re.shape)
        n_im = jnp.broadcast_to(x_im[SUBLANES - 1:SUBLANES, :], x_im.shape)
        return n_re, n_im

    c0 = carry_ref[j]
    c_re, c_im = lax.fori_loop(0, t_len // SUBLANES, body, (c0[:, :half], c0[:, half:]))
    carry_ref[j] = jnp.concatenate([c_re, c_im], axis=1)

    y = jnp.dot(bu_s[...].astype(BF16), wc_ref[0], preferred_element_type=F32)
    y = _gelu_tanh(y + d_ref[0] * u)
    y_s[j] = y

    @pl.when(j == n_slab - 1)
    def _():
        yf = jnp.concatenate([y_s[s] for s in range(y_s.shape[0])], axis=1)
        gate = jnp.dot(yf.astype(BF16), gw_ref[...], preferred_element_type=F32) + gb_ref[...]
        o_ref[...] = (yf * _sigmoid(gate)).astype(BF16)


def s5(z, col0, lam_re, lam_im, log_dt, b_re, b_im, c_re, c_im, d_skip, glu_w, glu_b):
    s = z.shape[0]
    t_len = min(S5_T, s)
    groups, n_state = lam_re.shape
    gps = S5_SLAB // S5_GROUP
    n_slab = groups // gps
    half = gps * n_state
    cb = col0 // S5_SLAB
    lam_re = lam_re.astype(F32)
    lam_im = lam_im.astype(F32)
    dt = jnp.exp(log_dt.astype(F32))[:, None]
    mag = jnp.exp(lam_re * dt)
    ang = lam_im * dt
    ab_re = mag * jnp.cos(ang)
    ab_im = mag * jnp.sin(ang)
    den = lam_re * lam_re + lam_im * lam_im
    num_re = ab_re - 1.0
    coef_re = (num_re * lam_re + ab_im * lam_im) / den
    coef_im = (ab_im * lam_re - num_re * lam_im) / den
    br = b_re.astype(F32)
    bi = b_im.astype(F32)
    bb_re = coef_re[..., None] * br - coef_im[..., None] * bi
    bb_im = coef_re[..., None] * bi + coef_im[..., None] * br
    eye = jnp.eye(gps, dtype=F32)

    def blockdiag_in(bb):
        t = bb.reshape(n_slab, gps, n_state, S5_GROUP)
        return jnp.einsum('sgnp,gh->sgphn', t, eye).reshape(n_slab, gps * S5_GROUP, gps * n_state)

    wb = jnp.concatenate([blockdiag_in(bb_re), blockdiag_in(bb_im)], axis=-1).astype(BF16)

    def blockdiag_out(cc):
        t = cc.reshape(n_slab, gps, S5_GROUP, n_state)
        return jnp.einsum('sgpn,gh->sgnhp', t, eye).reshape(n_slab, gps * n_state, gps * S5_GROUP)

    wc = jnp.concatenate([blockdiag_out(c_re.astype(F32)), -blockdiag_out(c_im.astype(F32))],
                         axis=1).astype(BF16)
    expo = jnp.array([1, 2, 3, 4, 5, 6, 7, 8, 1, 2, 4, 0, 0, 0, 0, 0], F32)[:, None, None]
    p_mag = jnp.exp(expo * (lam_re * dt)[None])
    p_re = (p_mag * jnp.cos(expo * ang[None])).reshape(16, n_slab, half)
    p_im = (p_mag * jnp.sin(expo * ang[None])).reshape(16, n_slab, half)
    pw = jnp.moveaxis(jnp.concatenate([p_re, p_im], axis=-1), 1, 0)

    return pl.pallas_call(
        _s5_kernel,
        out_shape=jax.ShapeDtypeStruct((s, W_MIX), BF16),
        grid=(s // t_len, n_slab),
        in_specs=[
            pl.BlockSpec((t_len, S5_SLAB), lambda i, j: (i, cb + j)),
            pl.BlockSpec((1, S5_SLAB, 2 * half), lambda i, j: (j, 0, 0)),
            pl.BlockSpec((1, 16, 2 * half), lambda i, j: (j, 0, 0)),
            pl.BlockSpec((1, 2 * half, S5_SLAB), lambda i, j: (j, 0, 0)),
            pl.BlockSpec((1, 1, S5_SLAB), lambda i, j: (j, 0, 0)),
            pl.BlockSpec((W_MIX, W_MIX), lambda i, j: (0, 0)),
            pl.BlockSpec((1, W_MIX), lambda i, j: (0, 0)),
        ],
        out_specs=pl.BlockSpec((t_len, W_MIX), lambda i, j: (i, 0)),
        scratch_shapes=[pltpu.VMEM((n_slab, SUBLANES, 2 * half), F32),
                        pltpu.VMEM((t_len, 2 * half), F32),
                        pltpu.VMEM((n_slab, t_len, S5_SLAB), F32)],
        compiler_params=_cparams("arbitrary", "arbitrary"),
        name="s5",
    )(z, wb, pw, wc, d_skip.astype(F32).reshape(n_slab, 1, S5_SLAB), glu_w.astype(BF16),
      glu_b.astype(F32).reshape(1, W_MIX))


def _retention_kernel(zq_ref, zk_ref, zv_ref, zg_ref, cos_ref, sin_ref, dmat_ref, qdec_ref, kdec_ref,
                      cdec_ref, ng_ref, o_ref, st_ref):
    @pl.when(pl.program_id(1) == 0)
    def _():
        st_ref[...] = jnp.zeros_like(st_ref)

    half = cos_ref.shape[1]
    cos = cos_ref[...]
    sin = sin_ref[...]

    def rope(t):
        t1, t2 = t[:, :half], t[:, half:]
        return jnp.concatenate([t1 * cos - t2 * sin, t1 * sin + t2 * cos], axis=1)

    q = rope(zq_ref[...])
    k = rope(zk_ref[...]) * (2 * half) ** -0.5
    v = zv_ref[...].astype(BF16)
    qb = q.astype(BF16)
    scores = lax.dot_general(qb, k.astype(BF16), (((1,), (1,)), ((), ())),
                             preferred_element_type=F32) * dmat_ref[0]
    st = st_ref[...]
    o = (jnp.dot(scores.astype(BF16), v, preferred_element_type=F32)
         + jnp.dot(qb, st.astype(BF16), preferred_element_type=F32) * qdec_ref[0])
    kd = (k * kdec_ref[0]).astype(BF16)
    st_ref[...] = cdec_ref[0] * st + lax.dot_general(kd, v, (((0,), (0,)), ((), ())),
                                                     preferred_element_type=F32)
    mu = jnp.mean(o, axis=-1, keepdims=True)
    oc = o - mu
    var = jnp.mean(oc * oc, axis=-1, keepdims=True)
    o = oc * lax.rsqrt(var + EPS)
    o_ref[...] = (o * ng_ref[...] * _silu(zg_ref[...])).astype(BF16)


def retention(z, col0, norm_g):
    s = z.shape[0]
    t_len = min(RET_T, s)
    hd = W_MIX // C_HEADS
    cb = col0 // hd
    pos = jnp.arange(s, dtype=F32)
    inv_freq = ROPE_THETA ** (-jnp.arange(0, hd, 2, dtype=F32) / hd)
    ang = pos[:, None] * inv_freq[None, :]
    cos = jnp.cos(ang)
    sin = jnp.sin(ang)
    log_gamma = jnp.log(1.0 - 2.0 ** (-5.0 - jnp.arange(C_HEADS, dtype=F32)))
    idx = jnp.arange(t_len, dtype=F32)
    rel = idx[:, None] - idx[None, :]
    dmat = jnp.where(rel[None] >= 0, jnp.exp(jnp.maximum(rel, 0.0)[None] * log_gamma[:, None, None]), 0.0)
    qdec = jnp.exp((idx + 1.0)[None, :] * log_gamma[:, None])[..., None]
    kdec = jnp.exp((t_len - 1.0 - idx)[None, :] * log_gamma[:, None])[..., None]
    cdec = jnp.broadcast_to(jnp.exp(t_len * log_gamma)[:, None, None], (C_HEADS, 1, hd))

    def zspec(k):
        return pl.BlockSpec((t_len, hd), lambda h, i: (i, cb + k * C_HEADS + h))

    tab = pl.BlockSpec((t_len, hd // 2), lambda h, i: (i, 0))
    return pl.pallas_call(
        _retention_kernel,
        out_shape=jax.ShapeDtypeStruct((s, W_MIX), BF16),
        grid=(C_HEADS, s // t_len),
        in_specs=[zspec(0), zspec(1), zspec(2), zspec(3), tab, tab,
                  pl.BlockSpec((1, t_len, t_len), lambda h, i: (h, 0, 0)),
                  pl.BlockSpec((1, t_len, 1), lambda h, i: (h, 0, 0)),
                  pl.BlockSpec((1, t_len, 1), lambda h, i: (h, 0, 0)),
                  pl.BlockSpec((1, 1, hd), lambda h, i: (h, 0, 0)),
                  pl.BlockSpec((1, hd), lambda h, i: (0, h))],
        out_specs=pl.BlockSpec((t_len, hd), lambda h, i: (i, h)),
        scratch_shapes=[pltpu.VMEM((hd, hd), F32)],
        compiler_params=_cparams("parallel", "arbitrary"),
        name="retention",
    )(z, z, z, z, cos, sin, dmat, qdec, kdec, cdec, norm_g.reshape(1, W_MIX))


def _rglru_kernel(zg_ref, zx_ref, cw_ref, cb_ref, wa_ref, ba_ref, wx_ref, bx_ref, sp_ref, o_ref,
                  xbuf, h_ref, a_s, u_s):
    t_len = zx_ref.shape[0]

    @pl.when(pl.program_id(0) == 0)
    def _():
        xbuf[0:SUBLANES, :] = jnp.zeros((SUBLANES, W_MIX), F32)
        h_ref[...] = jnp.zeros_like(h_ref)

    xbuf[SUBLANES:, :] = zx_ref[...]
    xc = cb_ref[...]
    for tap in range(CONV_WIDTH):
        off = SUBLANES - (CONV_WIDTH - 1) + tap
        xc = xc + xbuf[off:off + t_len, :] * cw_ref[tap:tap + 1, :]
    xbuf[0:SUBLANES, :] = xbuf[t_len:t_len + SUBLANES, :]
    xcb = xc.astype(BF16)
    n_blk = W_MIX // D_BLOCK
    pre_r = jnp.concatenate(
        [jnp.dot(xcb[:, b * D_BLOCK:(b + 1) * D_BLOCK], wa_ref[b], preferred_element_type=F32)
         for b in range(n_blk)], axis=1)
    pre_i = jnp.concatenate(
        [jnp.dot(xcb[:, b * D_BLOCK:(b + 1) * D_BLOCK], wx_ref[b], preferred_element_type=F32)
         for b in range(n_blk)], axis=1)
    r = _sigmoid(pre_r + ba_ref[...])
    gi = _sigmoid(pre_i + bx_ref[...])
    log_a = -RG_C * r * sp_ref[...]
    a = jnp.exp(log_a)
    a_s[...] = a
    u_s[...] = jnp.sqrt(1.0 - a * a) * (gi * xc)
    row = lax.broadcasted_iota(jnp.int32, (SUBLANES, W_MIX), 0)

    def body(b, h):
        r0 = pl.multiple_of(b * SUBLANES, SUBLANES)
        aa = a_s[pl.ds(r0, SUBLANES), :]
        uu = u_s[pl.ds(r0, SUBLANES), :]
        for k in (1, 2, 4):
            us = jnp.where(row >= k, pltpu.roll(uu, k, 0), 0.0)
            as_ = jnp.where(row >= k, pltpu.roll(aa, k, 0), 1.0)
            uu = uu + aa * us
            aa = aa * as_
        hh = uu + aa * h
        u_s[pl.ds(r0, SUBLANES), :] = hh
        return jnp.broadcast_to(hh[SUBLANES - 1:SUBLANES, :], hh.shape)

    h_ref[...] = lax.fori_loop(0, t_len // SUBLANES, body, h_ref[...])
    o_ref[...] = (_gelu_tanh(zg_ref[...]) * u_s[...]).astype(BF16)


def rglru(z, col0, conv_w, conv_b, w_a, b_a, w_x, b_x, lam):
    s = z.shape[0]
    t_len = min(RG_T, s)
    cb = col0 // W_MIX
    sp = jax.nn.softplus(-lam.astype(F32)).reshape(1, W_MIX)
    row = lambda a: a.astype(F32).reshape(1, W_MIX)
    full = lambda shape: pl.BlockSpec(shape, lambda i: (0,) * len(shape))
    n_blk = W_MIX // D_BLOCK
    return pl.pallas_call(
        _rglru_kernel,
        out_shape=jax.ShapeDtypeStruct((s, W_MIX), BF16),
        grid=(s // t_len,),
        in_specs=[pl.BlockSpec((t_len, W_MIX), lambda i: (i, cb)),
                  pl.BlockSpec((t_len, W_MIX), lambda i: (i, cb + 1)),
                  full((CONV_WIDTH, W_MIX)), full((1, W_MIX)),
                  full((n_blk, D_BLOCK, D_BLOCK)), full((1, W_MIX)),
                  full((n_blk, D_BLOCK, D_BLOCK)), full((1, W_MIX)), full((1, W_MIX))],
        out_specs=pl.BlockSpec((t_len, W_MIX), lambda i: (i, 0)),
        scratch_shapes=[pltpu.VMEM((t_len + SUBLANES, W_MIX), F32),
                        pltpu.VMEM((SUBLANES, W_MIX), F32),
                        pltpu.VMEM((t_len, W_MIX), F32),
                        pltpu.VMEM((t_len, W_MIX), F32)],
        compiler_params=_cparams("arbitrary"),
        name="rglru",
    )(z, z, conv_w.astype(F32), row(conv_b), w_a.astype(BF16), row(b_a), w_x.astype(BF16), row(b_x), sp)


def kernel(x, norm_mix_g, norm_ffn_g, final_norm_g, w_in, w_out, hgrn_lb_logits, hgrn_norm_g, s5_lambda_re, s5_lambda_im, s5_log_dt, s5_b_re, s5_b_im, s5_c_re, s5_c_im, s5_d, s5_glu_w, s5_glu_b, ret_norm_g, rg_conv_w, rg_conv_b, rg_w_a, rg_b_a, rg_w_x, rg_b_x, rg_lambda, ffn_w1, ffn_w3, ffn_w2, router_w, moe_w1, moe_w3, moe_w2):
    b_, s_, d_ = x.shape
    depth = w_in.shape[0]
    xs = x.reshape(b_ * s_, d_).astype(F32)
    lb_p = jax.nn.softmax(hgrn_lb_logits.astype(F32), axis=0)
    lb_all = jnp.cumsum(lb_p, axis=0) - lb_p[0]
    col_a, col_b, col_c, col_d = 0, 4 * W_MIX, 5 * W_MIX, 9 * W_MIX
    for layer in range(depth):
        z = norm_matmul(xs, norm_mix_g[layer], w_in[layer].astype(BF16))
        o_a = hgrn2(z, col_a, lb_all[layer], hgrn_norm_g[layer])
        o_b = s5(z, col_b, s5_lambda_re[layer], s5_lambda_im[layer], s5_log_dt[layer],
                 s5_b_re[layer], s5_b_im[layer], s5_c_re[layer], s5_c_im[layer],
                 s5_d[layer], s5_glu_w[layer], s5_glu_b[layer])
        o_c = retention(z, col_c, ret_norm_g[layer])
        o_d = rglru(z, col_d, rg_conv_w[layer], rg_conv_b[layer], rg_w_a[layer], rg_b_a[layer],
                    rg_w_x[layer], rg_b_x[layer], rg_lambda[layer])
        xs = out_proj((o_a, o_b, o_c, o_d), w_out[layer].astype(BF16), xs)
        if layer % 2 == 0:
            m = layer // 2
            xs = ffn(xs, norm_ffn_g[layer], ffn_w1[m].astype(BF16), ffn_w3[m].astype(BF16),
                     ffn_w2[m].astype(BF16))
        else:
            m = layer // 2
            xs = moe(xs, norm_ffn_g[layer], router_w[m], moe_w1[m], moe_w3[m], moe_w2[m])
    return final_norm(xs, final_norm_g).reshape(b_, s_, d_)
```

```python
import functools
import math

import jax
import jax.numpy as jnp
from jax import lax
from jax.experimental import pallas as pl
from jax.experimental.pallas import tpu as pltpu

F32 = jnp.float32
BF16 = jnp.bfloat16
EPS = 1e-6
HIGHEST = lax.Precision.HIGHEST

V7X_VMEM_BYTES = 64 * 1024 * 1024
VMEM_LIMIT_BYTES = V7X_VMEM_BYTES - 8 * 1024 * 1024
SUBLANES = 8
LANES = 128

A_HEAD_DIM = 128
S5_GROUP = 16
S5_STATE = 64
C_HEADS = 4
ROPE_THETA = 10000.0
D_BLOCK = 128
CONV_WIDTH = 4
RG_C = 8.0
N_EXPERTS = 8
TOP_K = 2
W_MIX = 1024

ROW_TILE = 512
IN_PROJ_TN = 512
OUT_PROJ_TN = 512
FFN_TF = 256
MOE_TM = 512
GATHER_ROWS = 256
COMBINE_ROWS = 256
HGRN_T = 256
HGRN_C = 16
RET_T = 256
S5_T = 256
S5_SLAB = 128
RG_T = 256


def _cparams(*sem):
    return pltpu.CompilerParams(dimension_semantics=sem, vmem_limit_bytes=VMEM_LIMIT_BYTES)


def _rms(xf, g):
    return xf * lax.rsqrt(jnp.mean(xf * xf, axis=-1, keepdims=True) + EPS) * g


def _sigmoid(x):
    return 1.0 / (1.0 + jnp.exp(-x))


def _silu(x):
    return x * _sigmoid(x)


def _gelu_tanh(x):
    c = math.sqrt(2.0 / math.pi)
    return 0.5 * x * (1.0 + jnp.tanh(c * (x + 0.044715 * (x * x * x))))


def _norm_matmul_kernel(x_ref, g_ref, w_ref, o_ref, h_ref):
    @pl.when(pl.program_id(1) == 0)
    def _():
        h_ref[...] = _rms(x_ref[...], g_ref[...]).astype(BF16)

    o_ref[...] = jnp.dot(h_ref[...], w_ref[...], preferred_element_type=F32)


def norm_matmul(x, g, w):
    s, d = x.shape
    n = w.shape[1]
    tm, tn = min(ROW_TILE, s), IN_PROJ_TN
    return pl.pallas_call(
        _norm_matmul_kernel,
        out_shape=jax.ShapeDtypeStruct((s, n), F32),
        grid=(s // tm, n // tn),
        in_specs=[
            pl.BlockSpec((tm, d), lambda i, j: (i, 0)),
            pl.BlockSpec((1, d), lambda i, j: (0, 0)),
            pl.BlockSpec((d, tn), lambda i, j: (0, j)),
        ],
        out_specs=pl.BlockSpec((tm, tn), lambda i, j: (i, j)),
        scratch_shapes=[pltpu.VMEM((tm, d), BF16)],
        compiler_params=_cparams("parallel", "arbitrary"),
        name="norm_in_proj",
    )(x, g.reshape(1, d), w)


def _out_proj_kernel(oa_ref, ob_ref, oc_ref, od_ref, w_ref, x_ref, o_ref):
    acc = x_ref[...]
    for idx, r in enumerate((oa_ref, ob_ref, oc_ref, od_ref)):
        acc = acc + jnp.dot(r[...], w_ref[idx * W_MIX:(idx + 1) * W_MIX, :], preferred_element_type=F32)
    o_ref[...] = acc


def out_proj(parts, w, x):
    s, d = x.shape
    tm, tn = min(ROW_TILE, s), OUT_PROJ_TN
    part_spec = pl.BlockSpec((tm, W_MIX), lambda i, j: (i, 0))
    return pl.pallas_call(
        _out_proj_kernel,
        out_shape=jax.ShapeDtypeStruct((s, d), F32),
        grid=(s // tm, d // tn),
        in_specs=[part_spec, part_spec, part_spec, part_spec,
                  pl.BlockSpec((4 * W_MIX, tn), lambda i, j: (0, j)),
                  pl.BlockSpec((tm, tn), lambda i, j: (i, j))],
        out_specs=pl.BlockSpec((tm, tn), lambda i, j: (i, j)),
        compiler_params=_cparams("parallel", "arbitrary"),
        name="out_proj",
    )(*parts, w, x)


def _ffn_kernel(x_ref, g_ref, w1_ref, w3_ref, w2_ref, o_ref, h_ref):
    @pl.when(pl.program_id(1) == 0)
    def _():
        xf = x_ref[...]
        h_ref[...] = _rms(xf, g_ref[...]).astype(BF16)
        o_ref[...] = xf

    h = h_ref[...]
    a = jnp.dot(h, w1_ref[...], preferred_element_type=F32)
    b = jnp.dot(h, w3_ref[...], preferred_element_type=F32)
    act = (_silu(a) * b).astype(BF16)
    o_ref[...] += jnp.dot(act, w2_ref[...], preferred_element_type=F32)


def ffn(x, g, w1, w3, w2):
    s, d = x.shape
    dff = w1.shape[1]
    tm, tf = min(ROW_TILE, s), FFN_TF
    return pl.pallas_call(
        _ffn_kernel,
        out_shape=jax.ShapeDtypeStruct((s, d), F32),
        grid=(s // tm, dff // tf),
        in_specs=[
            pl.BlockSpec((tm, d), lambda i, f: (i, 0), pipeline_mode=pl.Buffered(1)),
            pl.BlockSpec((1, d), lambda i, f: (0, 0)),
            pl.BlockSpec((d, tf), lambda i, f: (0, f)),
            pl.BlockSpec((d, tf), lambda i, f: (0, f)),
            pl.BlockSpec((tf, d), lambda i, f: (f, 0)),
        ],
        out_specs=pl.BlockSpec((tm, d), lambda i, f: (i, 0)),
        scratch_shapes=[pltpu.VMEM((tm, d), BF16)],
        compiler_params=_cparams("parallel", "arbitrary"),
        name="ffn_swiglu",
    )(x, g.reshape(1, d), w1, w3, w2)


def _router_kernel(x_ref, g_ref, wr_ref, comb_ref):
    h = _rms(x_ref[...], g_ref[...])
    logits = jnp.dot(h, wr_ref[...], precision=HIGHEST, preferred_element_type=F32)
    lane = lax.broadcasted_iota(jnp.int32, logits.shape, 1)
    neg = jnp.float32(-jnp.inf)
    logits = jnp.where(lane < N_EXPERTS, logits, neg)
    v1 = jnp.max(logits, axis=-1, keepdims=True)
    i1 = jnp.min(jnp.where(logits == v1, lane, LANES), axis=-1, keepdims=True)
    rest = jnp.where(lane == i1, neg, logits)
    v2 = jnp.max(rest, axis=-1, keepdims=True)
    i2 = jnp.min(jnp.where(rest == v2, lane, LANES), axis=-1, keepdims=True)
    e2 = jnp.exp(v2 - v1)
    g1 = 1.0 / (1.0 + e2)
    g2 = e2 / (1.0 + e2)
    comb_ref[...] = (jnp.where(lane == 0, i1.astype(F32), 0.0) + jnp.where(lane == 1, i2.astype(F32), 0.0)
                     + jnp.where(lane == 2, g1, 0.0) + jnp.where(lane == 3, g2, 0.0))


def router(x, g, router_w):
    s, d = x.shape
    tm = min(ROW_TILE, s)
    wr = jnp.zeros((d, LANES), F32).at[:, :N_EXPERTS].set(router_w.astype(F32))
    return pl.pallas_call(
        _router_kernel,
        out_shape=jax.ShapeDtypeStruct((s, LANES), F32),
        grid=(s // tm,),
        in_specs=[pl.BlockSpec((tm, d), lambda i: (i, 0)),
                  pl.BlockSpec((1, d), lambda i: (0, 0)),
                  pl.BlockSpec((d, LANES), lambda i: (0, 0))],
        out_specs=pl.BlockSpec((tm, LANES), lambda i: (i, 0)),
        compiler_params=_cparams("parallel"),
        name="moe_router",
    )(x, g.reshape(1, d), wr)


def _row_copy(src_hbm, src_row, dst_ref, dst_row, sem):
    return pltpu.make_async_copy(src_hbm.at[pl.ds(src_row, 1)], dst_ref.at[pl.ds(dst_row, 1)], sem)


def _gather_rows_kernel(idx_ref, x_hbm, o_ref, sem):
    n_rows = o_ref.shape[0]

    def issue(r, c):
        _row_copy(x_hbm, idx_ref[0, 0, r], o_ref, r, sem).start()
        return c

    lax.fori_loop(0, n_rows, issue, 0, unroll=8)

    def wait(r, c):
        _row_copy(x_hbm, 0, o_ref, r, sem).wait()
        return c

    lax.fori_loop(0, n_rows, wait, 0, unroll=8)


def gather_rows(x, idx):
    d = x.shape[1]
    p = idx.shape[0]
    rt = GATHER_ROWS
    return pl.pallas_call(
        _gather_rows_kernel,
        out_shape=jax.ShapeDtypeStruct((p, d), x.dtype),
        grid=(p // rt,),
        in_specs=[pl.BlockSpec((1, 1, rt), lambda i: (i, 0, 0), memory_space=pltpu.SMEM),
                  pl.BlockSpec(memory_space=pl.ANY)],
        out_specs=pl.BlockSpec((rt, d), lambda i: (i, 0)),
        scratch_shapes=[pltpu.SemaphoreType.DMA(())],
        compiler_params=_cparams("arbitrary"),
        name="moe_gather",
    )(idx.reshape(p // rt, 1, rt), x)


def _moe_grouped_kernel(te_ref, nu_ref, x_ref, g_ref, gate_ref, w1_ref, w3_ref, w2_ref, o_ref, h_ref):
    i = pl.program_id(0)
    f = pl.program_id(1)
    used = i < nu_ref[0]

    @pl.when(f == 0)
    def _():
        o_ref[...] = jnp.zeros_like(o_ref)

    @pl.when(used & (f == 0))
    def _():
        h_ref[...] = _rms(x_ref[...], g_ref[...]).astype(BF16)

    @pl.when(used)
    def _():
        h = h_ref[...]
        a = jnp.dot(h, w1_ref[0], preferred_element_type=F32)
        b = jnp.dot(h, w3_ref[0], preferred_element_type=F32)
        act = (_silu(a) * b).astype(BF16)
        o_ref[...] += jnp.dot(act, w2_ref[0], preferred_element_type=F32)

    @pl.when(used & (f == pl.num_programs(1) - 1))
    def _():
        o_ref[...] = o_ref[...] * gate_ref[...]


def moe_grouped(xs, g, gate_sorted, tile_expert, n_used, w1, w3, w2):
    p, d = xs.shape
    dff = w1.shape[2]
    tm, tf = MOE_TM, FFN_TF
    nf = dff // tf

    def row_map(i, f, te, nu):
        return (jnp.minimum(i, nu[0] - 1), 0)

    def f_eff(i, f, nu):
        return jnp.where(i < nu[0], f, nf - 1)

    grid_spec = pltpu.PrefetchScalarGridSpec(
        num_scalar_prefetch=2,
        grid=(p // tm, nf),
        in_specs=[
            pl.BlockSpec((tm, d), row_map, pipeline_mode=pl.Buffered(1)),
            pl.BlockSpec((1, d), lambda i, f, te, nu: (0, 0)),
            pl.BlockSpec((tm, 1), row_map),
            pl.BlockSpec((1, d, tf), lambda i, f, te, nu: (te[i], 0, f_eff(i, f, nu))),
            pl.BlockSpec((1, d, tf), lambda i, f, te, nu: (te[i], 0, f_eff(i, f, nu))),
            pl.BlockSpec((1, tf, d), lambda i, f, te, nu: (te[i], f_eff(i, f, nu), 0)),
        ],
        out_specs=pl.BlockSpec((tm, d), lambda i, f, te, nu: (i, 0)),
        scratch_shapes=[pltpu.VMEM((tm, d), BF16)],
    )
    return pl.pallas_call(
        _moe_grouped_kernel,
        out_shape=jax.ShapeDtypeStruct((p, d), F32),
        grid_spec=grid_spec,
        compiler_params=_cparams("arbitrary", "arbitrary"),
        name="moe_grouped",
    )(tile_expert, n_used, xs, g.reshape(1, d), gate_sorted.reshape(p, 1), w1, w3, w2)


def _combine_kernel(idx_ref, x_ref, y_hbm, o_ref, buf, sem):
    n_rows = o_ref.shape[0]

    def issue(r, c):
        for k in range(TOP_K):
            _row_copy(y_hbm, idx_ref[0, 0, TOP_K * r + k], buf.at[k], r, sem).start()
        return c

    lax.fori_loop(0, n_rows, issue, 0, unroll=4)

    def wait(r, c):
        for k in range(TOP_K):
            _row_copy(y_hbm, 0, buf.at[k], r, sem).wait()
        return c

    lax.fori_loop(0, n_rows, wait, 0, unroll=4)
    acc = x_ref[...]
    for k in range(TOP_K):
        acc = acc + buf[k]
    o_ref[...] = acc


def combine(x, y, dest):
    s, d = x.shape
    rt = COMBINE_ROWS
    return pl.pallas_call(
        _combine_kernel,
        out_shape=jax.ShapeDtypeStruct((s, d), F32),
        grid=(s // rt,),
        in_specs=[pl.BlockSpec((1, 1, TOP_K * rt), lambda i: (i, 0, 0), memory_space=pltpu.SMEM),
                  pl.BlockSpec((rt, d), lambda i: (i, 0)),
                  pl.BlockSpec(memory_space=pl.ANY)],
        out_specs=pl.BlockSpec((rt, d), lambda i: (i, 0)),
        scratch_shapes=[pltpu.VMEM((TOP_K, rt, d), F32), pltpu.SemaphoreType.DMA(())],
        compiler_params=_cparams("arbitrary"),
        name="moe_combine",
    )(dest.reshape(s // rt, 1, TOP_K * rt), x, y)


def _routing_tables(experts, gates, tm, n_tiles):
    n_assign = experts.size
    e_flat = experts.reshape(n_assign)
    onehot = (e_flat[:, None] == jnp.arange(N_EXPERTS, dtype=jnp.int32)[None, :]).astype(jnp.int32)
    csum = jnp.cumsum(onehot, axis=0)
    rank = jnp.sum(csum * onehot, axis=1) - 1
    counts = csum[-1]
    padded = ((counts + tm - 1) // tm) * tm
    seg_end = jnp.cumsum(padded)
    seg_start = seg_end - padded
    dest = (seg_start[e_flat] + rank).astype(jnp.int32)
    p = n_tiles * tm
    src = jnp.zeros((p,), jnp.int32).at[dest].set(jnp.arange(n_assign, dtype=jnp.int32) // TOP_K)
    gate_sorted = jnp.zeros((p,), F32).at[dest].set(gates.reshape(n_assign))
    n_used = (seg_end[-1] // tm).astype(jnp.int32)
    tile_start = jnp.arange(n_tiles, dtype=jnp.int32) * tm
    tile_e = jnp.sum((tile_start[:, None] >= seg_end[None, :]).astype(jnp.int32), axis=1)
    tile_e = jnp.minimum(tile_e, N_EXPERTS - 1)
    last_e = tile_e[n_used - 1]
    tile_e = jnp.where(jnp.arange(n_tiles) < n_used, tile_e, last_e).astype(jnp.int32)
    return src, dest, gate_sorted, tile_e, n_used.reshape(1)


def moe(x, g, router_w, w1, w3, w2):
    s, _ = x.shape
    info = router(x, g, router_w)
    experts = info[:, 0:TOP_K].astype(jnp.int32)
    gates = info[:, TOP_K:2 * TOP_K]
    tm = MOE_TM
    n_tiles = (TOP_K * s) // tm + N_EXPERTS
    src, dest, gate_sorted, tile_e, n_used = _routing_tables(experts, gates, tm, n_tiles)
    xs = gather_rows(x, src)
    y = moe_grouped(xs, g, gate_sorted, tile_e, n_used, w1.astype(BF16), w3.astype(BF16), w2.astype(BF16))
    return combine(x, y, dest)


def _final_norm_kernel(x_ref, g_ref, o_ref):
    o_ref[...] = _rms(x_ref[...], g_ref[...])


def final_norm(x, g):
    s, d = x.shape
    tm = min(ROW_TILE, s)
    return pl.pallas_call(
        _final_norm_kernel,
        out_shape=jax.ShapeDtypeStruct((s, d), F32),
        grid=(s // tm,),
        in_specs=[pl.BlockSpec((tm, d), lambda i: (i, 0)), pl.BlockSpec((1, d), lambda i: (0, 0))],
        out_specs=pl.BlockSpec((tm, d), lambda i: (i, 0)),
        compiler_params=_cparams("parallel"),
        name="final_norm",
    )(x, g.reshape(1, d))


def _hgrn2_kernel(zq_ref, zf_ref, zi_ref, zg_ref, lb_ref, ng_ref, tri_ref, o_ref,
                  st_ref, q_s, k_s, v_s, cum_s, o_s):
    c_len = HGRN_C
    t_len = q_s.shape[0]

    @pl.when(pl.program_id(1) == 0)
    def _():
        st_ref[...] = jnp.zeros_like(st_ref)

    lb = lb_ref[...]
    f = lb + (1.0 - lb) * _sigmoid(zf_ref[...])
    cum_s[...] = jnp.dot(tri_ref[...], jnp.log(f), precision=HIGHEST, preferred_element_type=F32)
    q_s[...] = _silu(zq_ref[...])
    k_s[...] = 1.0 - f
    v_s[...] = zi_ref[...]
    row = lax.broadcasted_iota(jnp.int32, (c_len, A_HEAD_DIM), 0)

    def body(c, carry):
        r0 = pl.multiple_of(c * c_len, c_len)
        q = q_s[pl.ds(r0, c_len), :]
        k = k_s[pl.ds(r0, c_len), :]
        v = v_s[pl.ds(r0, c_len), :]
        cm = cum_s[pl.ds(r0, c_len), :]
        rows = []
        for t in range(c_len):
            dec = jnp.where(row <= t, jnp.exp(jnp.minimum(cm[t:t + 1, :] - cm, 0.0)), 0.0)
            w = (q[t:t + 1, :] * k) * dec
            sc = jnp.sum(w, axis=1, keepdims=True)
            rows.append(jnp.sum(sc * v, axis=0, keepdims=True))
        o_intra = jnp.concatenate(rows, axis=0)
        st = st_ref[...]
        qd = (q * jnp.exp(cm)).astype(BF16)
        o_inter = lax.dot_general(qd, st.astype(BF16), (((1,), (1,)), ((), ())),
                                  preferred_element_type=F32)
        o_s[pl.ds(r0, c_len), :] = o_intra + o_inter
        last = cm[c_len - 1:c_len, :]
        kd = (k * jnp.exp(last - cm)).astype(BF16)
        upd = lax.dot_general(v.astype(BF16), kd, (((0,), (0,)), ((), ())),
                              preferred_element_type=F32)
        st_ref[...] = st * jnp.exp(last) + upd
        return carry

    lax.fori_loop(0, t_len // c_len, body, 0)
    o = o_s[...]
    o = o * lax.rsqrt(jnp.mean(o * o, axis=-1, keepdims=True) + EPS)
    o_ref[...] = (o * ng_ref[...] * _silu(zg_ref[...])).astype(BF16)


def hgrn2(z, col0, lb, norm_g):
    s = z.shape[0]
    t_len = min(HGRN_T, s)
    heads = W_MIX // A_HEAD_DIM
    cb = col0 // A_HEAD_DIM
    r = jnp.arange(t_len)
    tri = ((r[:, None] // HGRN_C == r[None, :] // HGRN_C) & (r[None, :] <= r[:, None])).astype(F32)

    def zspec(k):
        return pl.BlockSpec((t_len, A_HEAD_DIM), lambda h, i: (i, cb + k * heads + h))

    vec = pl.BlockSpec((1, A_HEAD_DIM), lambda h, i: (0, h))
    return pl.pallas_call(
        _hgrn2_kernel,
        out_shape=jax.ShapeDtypeStruct((s, W_MIX), BF16),
        grid=(heads, s // t_len),
        in_specs=[zspec(0), zspec(1), zspec(2), zspec(3), vec, vec,
                  pl.BlockSpec((t_len, t_len), lambda h, i: (0, 0))],
        out_specs=pl.BlockSpec((t_len, A_HEAD_DIM), lambda h, i: (i, h)),
        scratch_shapes=[pltpu.VMEM((A_HEAD_DIM, A_HEAD_DIM), F32)]
        + [pltpu.VMEM((t_len, A_HEAD_DIM), F32)] * 5,
        compiler_params=_cparams("parallel", "arbitrary"),
        name="hgrn2",
    )(z, z, z, z, lb.reshape(1, W_MIX), norm_g.reshape(1, W_MIX), tri)


def _s5_kernel(u_ref, wb_ref, pw_ref, wc_ref, d_ref, gw_ref, gb_ref, o_ref,
               carry_ref, bu_s, y_s):
    i = pl.program_id(0)
    j = pl.program_id(1)
    n_slab = pl.num_programs(1)
    t_len = bu_s.shape[0]
    half = bu_s.shape[1] // 2

    @pl.when(i == 0)
    def _():
        carry_ref[j] = jnp.zeros((SUBLANES, 2 * half), F32)

    u = u_ref[...]
    bu_s[...] = jnp.dot(u.astype(BF16), wb_ref[0], preferred_element_type=F32)
    pw = pw_ref[0]
    p8_re, p8_im = pw[0:8, :half], pw[0:8, half:]
    row = lax.broadcasted_iota(jnp.int32, (SUBLANES, half), 0)

    def body(b, carry):
        c_re, c_im = carry
        r0 = pl.multiple_of(b * SUBLANES, SUBLANES)
        blk = bu_s[pl.ds(r0, SUBLANES), :]
        x_re, x_im = blk[:, :half], blk[:, half:]
        for step, k in enumerate((1, 2, 4)):
            a_re = pw[8 + step:9 + step, :half]
            a_im = pw[8 + step:9 + step, half:]
            s_re = jnp.where(row >= k, pltpu.roll(x_re, k, 0), 0.0)
            s_im = jnp.where(row >= k, pltpu.roll(x_im, k, 0), 0.0)
            x_re, x_im = (x_re + a_re * s_re - a_im * s_im,
                          x_im + a_re * s_im + a_im * s_re)
        x_re, x_im = (x_re + p8_re * c_re - p8_im * c_im,
                      x_im + p8_re * c_im + p8_im * c_re)
        bu_s[pl.ds(r0, SUBLANES), :] = jnp.concatenate([x_re, x_im], axis=1)
        n_re = jnp.broadcast_to(x_re[SUBLANES - 1:SUBLANES, :], x_re.shape)
        n_im = jnp.broadcast_to(x_im[SUBLANES - 1:SUBLANES, :], x_im.shape)
        return n_re, n_im

    c0 = carry_ref[j]
    c_re, c_im = lax.fori_loop(0, t_len // SUBLANES, body, (c0[:, :half], c0[:, half:]))
    carry_ref[j] = jnp.concatenate([c_re, c_im], axis=1)

    y = jnp.dot(bu_s[...].astype(BF16), wc_ref[0], preferred_element_type=F32)
    y = _gelu_tanh(y + d_ref[0] * u)
    y_s[j] = y

    @pl.when(j == n_slab - 1)
    def _():
        yf = jnp.concatenate([y_s[s] for s in range(y_s.shape[0])], axis=1)
        gate = jnp.dot(yf.astype(BF16), gw_ref[...], preferred_element_type=F32) + gb_ref[...]
        o_ref[...] = (yf * _sigmoid(gate)).astype(BF16)


def s5(z, col0, lam_re, lam_im, log_dt, b_re, b_im, c_re, c_im, d_skip, glu_w, glu_b):
    s = z.shape[0]
    t_len = min(S5_T, s)
    groups, n_state = lam_re.shape
    gps = S5_SLAB // S5_GROUP
    n_slab = groups // gps
    half = gps * n_state
    cb = col0 // S5_SLAB
    lam_re = lam_re.astype(F32)
    lam_im = lam_im.astype(F32)
    dt = jnp.exp(log_dt.astype(F32))[:, None]
    mag = jnp.exp(lam_re * dt)
    ang = lam_im * dt
    ab_re = mag * jnp.cos(ang)
    ab_im = mag * jnp.sin(ang)
    den = lam_re * lam_re + lam_im * lam_im
    num_re = ab_re - 1.0
    coef_re = (num_re * lam_re + ab_im * lam_im) / den
    coef_im = (ab_im * lam_re - num_re * lam_im) / den
    br = b_re.astype(F32)
    bi = b_im.astype(F32)
    bb_re = coef_re[..., None] * br - coef_im[..., None] * bi
    bb_im = coef_re[..., None] * bi + coef_im[..., None] * br
    eye = jnp.eye(gps, dtype=F32)

    def blockdiag_in(bb):
        t = bb.reshape(n_slab, gps, n_state, S5_GROUP)
        return jnp.einsum('sgnp,gh->sgphn', t, eye).reshape(n_slab, gps * S5_GROUP, gps * n_state)

    wb = jnp.concatenate([blockdiag_in(bb_re), blockdiag_in(bb_im)], axis=-1).astype(BF16)

    def blockdiag_out(cc):
        t = cc.reshape(n_slab, gps, S5_GROUP, n_state)
        return jnp.einsum('sgpn,gh->sgnhp', t, eye).reshape(n_slab, gps * n_state, gps * S5_GROUP)

    wc = jnp.concatenate([blockdiag_out(c_re.astype(F32)), -blockdiag_out(c_im.astype(F32))],
                         axis=1).astype(BF16)
    expo = jnp.array([1, 2, 3, 4, 5, 6, 7, 8, 1, 2, 4, 0, 0, 0, 0, 0], F32)[:, None, None]
    p_mag = jnp.exp(expo * (lam_re * dt)[None])
    p_re = (p_mag * jnp.cos(expo * ang[None])).reshape(16, n_slab, half)
    p_im = (p_mag * jnp.sin(expo * ang[None])).reshape(16, n_slab, half)
    pw = jnp.moveaxis(jnp.concatenate([p_re, p_im], axis=-1), 1, 0)

    return pl.pallas_call(
        _s5_kernel,
        out_shape=jax.ShapeDtypeStruct((s, W_MIX), BF16),
        grid=(s // t_len, n_slab),
        in_specs=[
            pl.BlockSpec((t_len, S5_SLAB), lambda i, j: (i, cb + j)),
            pl.BlockSpec((1, S5_SLAB, 2 * half), lambda i, j: (j, 0, 0)),
            pl.BlockSpec((1, 16, 2 * half), lambda i, j: (j, 0, 0)),
            pl.BlockSpec((1, 2 * half, S5_SLAB), lambda i, j: (j, 0, 0)),
            pl.BlockSpec((1, 1, S5_SLAB), lambda i, j: (j, 0, 0)),
            pl.BlockSpec((W_MIX, W_MIX), lambda i, j: (0, 0)),
            pl.BlockSpec((1, W_MIX), lambda i, j: (0, 0)),
        ],
        out_specs=pl.BlockSpec((t_len, W_MIX), lambda i, j: (i, 0)),
        scratch_shapes=[pltpu.VMEM((n_slab, SUBLANES, 2 * half), F32),
                        pltpu.VMEM((t_len, 2 * half), F32),
                        pltpu.VMEM((n_slab, t_len, S5_SLAB), F32)],
        compiler_params=_cparams("arbitrary", "arbitrary"),
        name="s5",
    )(z, wb, pw, wc, d_skip.astype(F32).reshape(n_slab, 1, S5_SLAB), glu_w.astype(BF16),
      glu_b.astype(F32).reshape(1, W_MIX))


def _retention_kernel(zq_ref, zk_ref, zv_ref, zg_ref, cos_ref, sin_ref, dmat_ref, qdec_ref, kdec_ref,
                      cdec_ref, ng_ref, o_ref, st_ref):
    @pl.when(pl.program_id(1) == 0)
    def _():
        st_ref[...] = jnp.zeros_like(st_ref)

    half = cos_ref.shape[1]
    cos = cos_ref[...]
    sin = sin_ref[...]

    def rope(t):
        t1, t2 = t[:, :half], t[:, half:]
        return jnp.concatenate([t1 * cos - t2 * sin, t1 * sin + t2 * cos], axis=1)

    q = rope(zq_ref[...])
    k = rope(zk_ref[...]) * (2 * half) ** -0.5
    v = zv_ref[...].astype(BF16)
    qb = q.astype(BF16)
    scores = lax.dot_general(qb, k.astype(BF16), (((1,), (1,)), ((), ())),
                             preferred_element_type=F32) * dmat_ref[0]
    st = st_ref[...]
    o = (jnp.dot(scores.astype(BF16), v, preferred_element_type=F32)
         + jnp.dot(qb, st.astype(BF16), preferred_element_type=F32) * qdec_ref[0])
    kd = (k * kdec_ref[0]).astype(BF16)
    st_ref[...] = cdec_ref[0] * st + lax.dot_general(kd, v, (((0,), (0,)), ((), ())),
                                                     preferred_element_type=F32)
    mu = jnp.mean(o, axis=-1, keepdims=True)
    oc = o - mu
    var = jnp.mean(oc * oc, axis=-1, keepdims=True)
    o = oc * lax.rsqrt(var + EPS)
    o_ref[...] = (o * ng_ref[...] * _silu(zg_ref[...])).astype(BF16)


def retention(z, col0, norm_g):
    s = z.shape[0]
    t_len = min(RET_T, s)
    hd = W_MIX // C_HEADS
    cb = col0 // hd
    pos = jnp.arange(s, dtype=F32)
    inv_freq = ROPE_THETA ** (-jnp.arange(0, hd, 2, dtype=F32) / hd)
    ang = pos[:, None] * inv_freq[None, :]
    cos = jnp.cos(ang)
    sin = jnp.sin(ang)
    log_gamma = jnp.log(1.0 - 2.0 ** (-5.0 - jnp.arange(C_HEADS, dtype=F32)))
    idx = jnp.arange(t_len, dtype=F32)
    rel = idx[:, None] - idx[None, :]
    dmat = jnp.where(rel[None] >= 0, jnp.exp(jnp.maximum(rel, 0.0)[None] * log_gamma[:, None, None]), 0.0)
    qdec = jnp.exp((idx + 1.0)[None, :] * log_gamma[:, None])[..., None]
    kdec = jnp.exp((t_len - 1.0 - idx)[None, :] * log_gamma[:, None])[..., None]
    cdec = jnp.broadcast_to(jnp.exp(t_len * log_gamma)[:, None, None], (C_HEADS, 1, hd))

    def zspec(k):
        return pl.BlockSpec((t_len, hd), lambda h, i: (i, cb + k * C_HEADS + h))

    tab = pl.BlockSpec((t_len, hd // 2), lambda h, i: (i, 0))
    return pl.pallas_call(
        _retention_kernel,
        out_shape=jax.ShapeDtypeStruct((s, W_MIX), BF16),
        grid=(C_HEADS, s // t_len),
        in_specs=[zspec(0), zspec(1), zspec(2), zspec(3), tab, tab,
                  pl.BlockSpec((1, t_len, t_len), lambda h, i: (h, 0, 0)),
                  pl.BlockSpec((1, t_len, 1), lambda h, i: (h, 0, 0)),
                  pl.BlockSpec((1, t_len, 1), lambda h, i: (h, 0, 0)),
                  pl.BlockSpec((1, 1, hd), lambda h, i: (h, 0, 0)),
                  pl.BlockSpec((1, hd), lambda h, i: (0, h))],
        out_specs=pl.BlockSpec((t_len, hd), lambda h, i: (i, h)),
        scratch_shapes=[pltpu.VMEM((hd, hd), F32)],
        compiler_params=_cparams("parallel", "arbitrary"),
        name="retention",
    )(z, z, z, z, cos, sin, dmat, qdec, kdec, cdec, norm_g.reshape(1, W_MIX))


def _rglru_kernel(zg_ref, zx_ref, cw_ref, cb_ref, wa_ref, ba_ref, wx_ref, bx_ref, sp_ref, o_ref,
                  xbuf, h_ref, a_s, u_s):
    t_len = zx_ref.shape[0]

    @pl.when(pl.program_id(0) == 0)
    def _():
        xbuf[0:SUBLANES, :] = jnp.zeros((SUBLANES, W_MIX), F32)
        h_ref[...] = jnp.zeros_like(h_ref)

    xbuf[SUBLANES:, :] = zx_ref[...]
    xc = cb_ref[...]
    for tap in range(CONV_WIDTH):
        off = SUBLANES - (CONV_WIDTH - 1) + tap
        xc = xc + xbuf[off:off + t_len, :] * cw_ref[tap:tap + 1, :]
    xbuf[0:SUBLANES, :] = xbuf[t_len:t_len + SUBLANES, :]
    xcb = xc.astype(BF16)
    n_blk = W_MIX // D_BLOCK
    pre_r = jnp.concatenate(
        [jnp.dot(xcb[:, b * D_BLOCK:(b + 1) * D_BLOCK], wa_ref[b], preferred_element_type=F32)
         for b in range(n_blk)], axis=1)
    pre_i = jnp.concatenate(
        [jnp.dot(xcb[:, b * D_BLOCK:(b + 1) * D_BLOCK], wx_ref[b], preferred_element_type=F32)
         for b in range(n_blk)], axis=1)
    r = _sigmoid(pre_r + ba_ref[...])
    gi = _sigmoid(pre_i + bx_ref[...])
    log_a = -RG_C * r * sp_ref[...]
    a = jnp.exp(log_a)
    a_s[...] = a
    u_s[...] = jnp.sqrt(1.0 - a * a) * (gi * xc)
    row = lax.broadcasted_iota(jnp.int32, (SUBLANES, W_MIX), 0)

    def body(b, h):
        r0 = pl.multiple_of(b * SUBLANES, SUBLANES)
        aa = a_s[pl.ds(r0, SUBLANES), :]
        uu = u_s[pl.ds(r0, SUBLANES), :]
        for k in (1, 2, 4):
            us = jnp.where(row >= k, pltpu.roll(uu, k, 0), 0.0)
            as_ = jnp.where(row >= k, pltpu.roll(aa, k, 0), 1.0)
            uu = uu + aa * us
            aa = aa * as_
        hh = uu + aa * h
        u_s[pl.ds(r0, SUBLANES), :] = hh
        return jnp.broadcast_to(hh[SUBLANES - 1:SUBLANES, :], hh.shape)

    h_ref[...] = lax.fori_loop(0, t_len // SUBLANES, body, h_ref[...])
    o_ref[...] = (_gelu_tanh(zg_ref[...]) * u_s[...]).astype(BF16)


def rglru(z, col0, conv_w, conv_b, w_a, b_a, w_x, b_x, lam):
    s = z.shape[0]
    t_len = min(RG_T, s)
    cb = col0 // W_MIX
    sp = jax.nn.softplus(-lam.astype(F32)).reshape(1, W_MIX)
    row = lambda a: a.astype(F32).reshape(1, W_MIX)
    full = lambda shape: pl.BlockSpec(shape, lambda i: (0,) * len(shape))
    n_blk = W_MIX // D_BLOCK
    return pl.pallas_call(
        _rglru_kernel,
        out_shape=jax.ShapeDtypeStruct((s, W_MIX), BF16),
        grid=(s // t_len,),
        in_specs=[pl.BlockSpec((t_len, W_MIX), lambda i: (i, cb)),
                  pl.BlockSpec((t_len, W_MIX), lambda i: (i, cb + 1)),
                  full((CONV_WIDTH, W_MIX)), full((1, W_MIX)),
                  full((n_blk, D_BLOCK, D_BLOCK)), full((1, W_MIX)),
                  full((n_blk, D_BLOCK, D_BLOCK)), full((1, W_MIX)), full((1, W_MIX))],
        out_specs=pl.BlockSpec((t_len, W_MIX), lambda i: (i, 0)),
        scratch_shapes=[pltpu.VMEM((t_len + SUBLANES, W_MIX), F32),
                        pltpu.VMEM((SUBLANES, W_MIX), F32),
                        pltpu.VMEM((t_len, W_MIX), F32),
                        pltpu.VMEM((t_len, W_MIX), F32)],
        compiler_params=_cparams("arbitrary"),
        name="rglru",
    )(z, z, conv_w.astype(F32), row(conv_b), w_a.astype(BF16), row(b_a), w_x.astype(BF16), row(b_x), sp)


def kernel(x, norm_mix_g, norm_ffn_g, final_norm_g, w_in, w_out, hgrn_lb_logits, hgrn_norm_g, s5_lambda_re, s5_lambda_im, s5_log_dt, s5_b_re, s5_b_im, s5_c_re, s5_c_im, s5_d, s5_glu_w, s5_glu_b, ret_norm_g, rg_conv_w, rg_conv_b, rg_w_a, rg_b_a, rg_w_x, rg_b_x, rg_lambda, ffn_w1, ffn_w3, ffn_w2, router_w, moe_w1, moe_w3, moe_w2):
    b_, s_, d_ = x.shape
    depth = w_in.shape[0]
    xs = x.reshape(b_ * s_, d_).astype(F32)
    lb_p = jax.nn.softmax(hgrn_lb_logits.astype(F32), axis=0)
    lb_all = jnp.cumsum(lb_p, axis=0) - lb_p[0]
    col_a, col_b, col_c, col_d = 0, 4 * W_MIX, 5 * W_MIX, 9 * W_MIX
    for layer in range(depth):
        z = norm_matmul(xs, norm_mix_g[layer], w_in[layer].astype(BF16))
        o_a = hgrn2(z, col_a, lb_all[layer], hgrn_norm_g[layer])
        o_b = s5(z, col_b, s5_lambda_re[layer], s5_lambda_im[layer], s5_log_dt[layer],
                 s5_b_re[layer], s5_b_im[layer], s5_c_re[layer], s5_c_im[layer],
                 s5_d[layer], s5_glu_w[layer], s5_glu_b[layer])
        o_c = retention(z, col_c, ret_norm_g[layer])
        o_d = rglru(z, col_d, rg_conv_w[layer], rg_conv_b[layer], rg_w_a[layer], rg_b_a[layer],
                    rg_w_x[layer], rg_b_x[layer], rg_lambda[layer])
        xs = out_proj((o_a, o_b, o_c, o_d), w_out[layer].astype(BF16), xs)
        if layer % 2 == 0:
            m = layer // 2
            xs = ffn(xs, norm_ffn_g[layer], ffn_w1[m].astype(BF16), ffn_w3[m].astype(BF16),
                     ffn_w2[m].astype(BF16))
        else:
            m = layer // 2
            xs = moe(xs, norm_ffn_g[layer], router_w[m], moe_w1[m], moe_w3[m], moe_w2[m])
    return final_norm(xs, final_norm_g).reshape(b_, s_, d_)
```

```python
import functools
import math

import jax
import jax.numpy as jnp
from jax import lax
from jax.experimental import pallas as pl
from jax.experimental.pallas import tpu as pltpu

F32 = jnp.float32
BF16 = jnp.bfloat16
EPS = 1e-6
HIGHEST = lax.Precision.HIGHEST

V7X_VMEM_BYTES = 64 * 1024 * 1024
VMEM_LIMIT_BYTES = V7X_VMEM_BYTES - 4 * 1024 * 1024
SUBLANES = 8
LANES = 128

A_HEAD_DIM = 128
S5_GROUP = 16
S5_STATE = 64
C_HEADS = 4
ROPE_THETA = 10000.0
D_BLOCK = 128
CONV_WIDTH = 4
RG_C = 8.0
N_EXPERTS = 8
TOP_K = 2
W_MIX = 1024

ROW_TILE = 512
IN_PROJ_TN = 1024
OUT_PROJ_TN = 1024
FFN_TF = 512
MOE_TM = 512
COMBINE_ROWS = 256
HGRN_T = 256
HGRN_C = 16
RET_T = 256
S5_T = 256
S5_SLAB = 128
S5_PW_ROWS = 4 * SUBLANES
RG_T = 256


def _cparams(*sem):
    return pltpu.CompilerParams(dimension_semantics=sem, vmem_limit_bytes=VMEM_LIMIT_BYTES)


def _rms(xf, g):
    return xf * lax.rsqrt(jnp.mean(xf * xf, axis=-1, keepdims=True) + EPS) * g


def _sigmoid(x):
    return 0.5 + 0.5 * jnp.tanh(0.5 * x)


def _silu(x):
    h = 0.5 * x
    return h + h * jnp.tanh(h)


def _gelu_tanh(x):
    c = math.sqrt(2.0 / math.pi)
    return 0.5 * x * (1.0 + jnp.tanh(c * (x + 0.044715 * (x * x * x))))


def _norm_matmul_kernel(x_ref, g_ref, w_ref, o_ref, h_ref):
    @pl.when(pl.program_id(1) == 0)
    def _():
        h_ref[...] = _rms(x_ref[...], g_ref[...]).astype(BF16)

    o_ref[...] = jnp.dot(h_ref[...], w_ref[...], preferred_element_type=F32)


def norm_matmul(x, g, w):
    s, d = x.shape
    n = w.shape[1]
    tm, tn = min(ROW_TILE, s), IN_PROJ_TN
    return pl.pallas_call(
        _norm_matmul_kernel,
        out_shape=jax.ShapeDtypeStruct((s, n), F32),
        grid=(s // tm, n // tn),
        in_specs=[
            pl.BlockSpec((tm, d), lambda i, j: (i, 0), pipeline_mode=pl.Buffered(1)),
            pl.BlockSpec((1, d), lambda i, j: (0, 0)),
            pl.BlockSpec((d, tn), lambda i, j: (0, j)),
        ],
        out_specs=pl.BlockSpec((tm, tn), lambda i, j: (i, j)),
        scratch_shapes=[pltpu.VMEM((tm, d), BF16)],
        compiler_params=_cparams("parallel", "arbitrary"),
        name="norm_in_proj",
    )(x, g.reshape(1, d), w)


def _out_proj_kernel(oa_ref, ob_ref, oc_ref, od_ref, w_ref, x_ref, o_ref):
    acc = x_ref[...]
    for idx, r in enumerate((oa_ref, ob_ref, oc_ref, od_ref)):
        acc = acc + jnp.dot(r[...], w_ref[idx * W_MIX:(idx + 1) * W_MIX, :], preferred_element_type=F32)
    o_ref[...] = acc


def out_proj(parts, w, x):
    s, d = x.shape
    tm, tn = min(ROW_TILE, s), OUT_PROJ_TN
    part_spec = pl.BlockSpec((tm, W_MIX), lambda i, j: (i, 0))
    return pl.pallas_call(
        _out_proj_kernel,
        out_shape=jax.ShapeDtypeStruct((s, d), F32),
        grid=(s // tm, d // tn),
        in_specs=[part_spec, part_spec, part_spec, part_spec,
                  pl.BlockSpec((4 * W_MIX, tn), lambda i, j: (0, j)),
                  pl.BlockSpec((tm, tn), lambda i, j: (i, j))],
        out_specs=pl.BlockSpec((tm, tn), lambda i, j: (i, j)),
        compiler_params=_cparams("parallel", "arbitrary"),
        name="out_proj",
    )(*parts, w, x)


def _ffn_kernel(x_ref, g_ref, w1_ref, w3_ref, w2_ref, o_ref, h_ref):
    @pl.when(pl.program_id(1) == 0)
    def _():
        xf = x_ref[...]
        h_ref[...] = _rms(xf, g_ref[...]).astype(BF16)
        o_ref[...] = xf

    h = h_ref[...]
    a = jnp.dot(h, w1_ref[...], preferred_element_type=F32)
    b = jnp.dot(h, w3_ref[...], preferred_element_type=F32)
    act = (_silu(a) * b).astype(BF16)
    o_ref[...] += jnp.dot(act, w2_ref[...], preferred_element_type=F32)


def ffn(x, g, w1, w3, w2):
    s, d = x.shape
    dff = w1.shape[1]
    tm, tf = min(ROW_TILE, s), FFN_TF
    return pl.pallas_call(
        _ffn_kernel,
        out_shape=jax.ShapeDtypeStruct((s, d), F32),
        grid=(s // tm, dff // tf),
        in_specs=[
            pl.BlockSpec((tm, d), lambda i, f: (i, 0), pipeline_mode=pl.Buffered(1)),
            pl.BlockSpec((1, d), lambda i, f: (0, 0)),
            pl.BlockSpec((d, tf), lambda i, f: (0, f)),
            pl.BlockSpec((d, tf), lambda i, f: (0, f)),
            pl.BlockSpec((tf, d), lambda i, f: (f, 0)),
        ],
        out_specs=pl.BlockSpec((tm, d), lambda i, f: (i, 0), pipeline_mode=pl.Buffered(1)),
        scratch_shapes=[pltpu.VMEM((tm, d), BF16)],
        compiler_params=_cparams("parallel", "arbitrary"),
        name="ffn_swiglu",
    )(x, g.reshape(1, d), w1, w3, w2)


def _router_kernel(x_ref, g_ref, wr_ref, comb_ref):
    h = _rms(x_ref[...], g_ref[...])
    logits = jnp.dot(h, wr_ref[...], precision=HIGHEST, preferred_element_type=F32)
    lane = lax.broadcasted_iota(jnp.int32, logits.shape, 1)
    neg = jnp.float32(-jnp.inf)
    logits = jnp.where(lane < N_EXPERTS, logits, neg)
    v1 = jnp.max(logits, axis=-1, keepdims=True)
    i1 = jnp.min(jnp.where(logits == v1, lane, LANES), axis=-1, keepdims=True)
    rest = jnp.where(lane == i1, neg, logits)
    v2 = jnp.max(rest, axis=-1, keepdims=True)
    i2 = jnp.min(jnp.where(rest == v2, lane, LANES), axis=-1, keepdims=True)
    e2 = jnp.exp(v2 - v1)
    g1 = 1.0 / (1.0 + e2)
    g2 = e2 / (1.0 + e2)
    comb_ref[...] = (jnp.where(lane == 0, i1.astype(F32), 0.0) + jnp.where(lane == 1, i2.astype(F32), 0.0)
                     + jnp.where(lane == 2, g1, 0.0) + jnp.where(lane == 3, g2, 0.0))


def router(x, g, router_w):
    s, d = x.shape
    tm = min(ROW_TILE, s)
    wr = jnp.zeros((d, LANES), F32).at[:, :N_EXPERTS].set(router_w.astype(F32))
    return pl.pallas_call(
        _router_kernel,
        out_shape=jax.ShapeDtypeStruct((s, LANES), F32),
        grid=(s // tm,),
        in_specs=[pl.BlockSpec((tm, d), lambda i: (i, 0)),
                  pl.BlockSpec((1, d), lambda i: (0, 0)),
                  pl.BlockSpec((d, LANES), lambda i: (0, 0))],
        out_specs=pl.BlockSpec((tm, LANES), lambda i: (i, 0)),
        compiler_params=_cparams("parallel"),
        name="moe_router",
    )(x, g.reshape(1, d), wr)


def _row_copy(src_hbm, src_row, dst_ref, dst_row, sem):
    return pltpu.make_async_copy(src_hbm.at[pl.ds(src_row, 1)], dst_ref.at[pl.ds(dst_row, 1)], sem)


def _moe_grouped_kernel(te_ref, nu_ref, src_ref, x_hbm, g_ref, gate_ref, w1_ref, w3_ref, w2_ref, o_ref,
                        xbuf, h_ref, sem):
    i = pl.program_id(0)
    f = pl.program_id(1)
    n_used = nu_ref[0]
    used = i < n_used
    tm = xbuf.shape[0]

    def start_gather(tile):
        def issue(r, c):
            _row_copy(x_hbm, src_ref[tile * tm + r], xbuf, r, sem).start()
            return c

        lax.fori_loop(0, tm, issue, 0, unroll=8)

    @pl.when((i == 0) & (f == 0))
    def _():
        start_gather(0)

    @pl.when(f == 0)
    def _():
        o_ref[...] = jnp.zeros_like(o_ref)

    @pl.when(used & (f == 0))
    def _():
        def wait(r, c):
            _row_copy(x_hbm, 0, xbuf, r, sem).wait()
            return c

        lax.fori_loop(0, tm, wait, 0, unroll=8)
        h_ref[...] = _rms(xbuf[...], g_ref[...]).astype(BF16)

    @pl.when((f == 1) & (i + 1 < n_used))
    def _():
        start_gather(i + 1)

    @pl.when(used)
    def _():
        h = h_ref[...]
        a = jnp.dot(h, w1_ref[0], preferred_element_type=F32)
        b = jnp.dot(h, w3_ref[0], preferred_element_type=F32)
        act = (_silu(a) * b).astype(BF16)
        o_ref[...] += jnp.dot(act, w2_ref[0], preferred_element_type=F32)

    @pl.when(used & (f == pl.num_programs(1) - 1))
    def _():
        o_ref[...] = o_ref[...] * gate_ref[...]


def moe_grouped(x, src, g, gate_sorted, tile_expert, n_used, w1, w3, w2):
    d = x.shape[1]
    p = src.shape[0]
    dff = w1.shape[2]
    tm, tf = MOE_TM, min(FFN_TF, dff // 2)
    nf = dff // tf

    def row_map(i, f, te, nu, sr):
        return (jnp.minimum(i, nu[0] - 1), 0)

    def f_eff(i, f, nu):
        return jnp.where(i < nu[0], f, nf - 1)

    grid_spec = pltpu.PrefetchScalarGridSpec(
        num_scalar_prefetch=3,
        grid=(p // tm, nf),
        in_specs=[
            pl.BlockSpec(memory_space=pl.ANY),
            pl.BlockSpec((1, d), lambda i, f, te, nu, sr: (0, 0)),
            pl.BlockSpec((tm, 1), row_map),
            pl.BlockSpec((1, d, tf), lambda i, f, te, nu, sr: (te[i], 0, f_eff(i, f, nu))),
            pl.BlockSpec((1, d, tf), lambda i, f, te, nu, sr: (te[i], 0, f_eff(i, f, nu))),
            pl.BlockSpec((1, tf, d), lambda i, f, te, nu, sr: (te[i], f_eff(i, f, nu), 0)),
        ],
        out_specs=pl.BlockSpec((tm, d), lambda i, f, te, nu, sr: (i, 0), pipeline_mode=pl.Buffered(1)),
        scratch_shapes=[pltpu.VMEM((tm, d), F32), pltpu.VMEM((tm, d), BF16), pltpu.SemaphoreType.DMA(())],
    )
    return pl.pallas_call(
        _moe_grouped_kernel,
        out_shape=jax.ShapeDtypeStruct((p, d), F32),
        grid_spec=grid_spec,
        compiler_params=_cparams("arbitrary", "arbitrary"),
        name="moe_grouped",
    )(tile_expert, n_used, src, x, g.reshape(1, d), gate_sorted.reshape(p, 1), w1, w3, w2)


def _combine_kernel(dest_ref, x_ref, y_hbm, *rest, final_norm):
    if final_norm:
        fg_ref, o_ref, buf, sem = rest
    else:
        o_ref, buf, sem = rest
    i = pl.program_id(0)
    n_rows = o_ref.shape[0]
    slot = i % 2

    def start_gather(step, sl):
        def issue(r, c):
            for k in range(TOP_K):
                _row_copy(y_hbm, dest_ref[TOP_K * (step * n_rows + r) + k], buf.at[sl, k], r, sem.at[sl]).start()
            return c

        lax.fori_loop(0, n_rows, issue, 0, unroll=4)

    @pl.when(i == 0)
    def _():
        start_gather(0, 0)

    @pl.when(i + 1 < pl.num_programs(0))
    def _():
        start_gather(i + 1, 1 - slot)

    def wait(r, c):
        for k in range(TOP_K):
            _row_copy(y_hbm, 0, buf.at[slot, k], r, sem.at[slot]).wait()
        return c

    lax.fori_loop(0, n_rows, wait, 0, unroll=4)
    acc = x_ref[...]
    for k in range(TOP_K):
        acc = acc + buf[slot, k]
    o_ref[...] = _rms(acc, fg_ref[...]) if final_norm else acc


def combine(x, y, dest, final_g=None):
    s, d = x.shape
    rt = COMBINE_ROWS
    final_norm = final_g is not None
    in_specs = [pl.BlockSpec((rt, d), lambda i, dr: (i, 0)), pl.BlockSpec(memory_space=pl.ANY)]
    args = [x, y]
    if final_norm:
        in_specs.append(pl.BlockSpec((1, d), lambda i, dr: (0, 0)))
        args.append(final_g.astype(F32).reshape(1, d))
    grid_spec = pltpu.PrefetchScalarGridSpec(
        num_scalar_prefetch=1,
        grid=(s // rt,),
        in_specs=in_specs,
        out_specs=pl.BlockSpec((rt, d), lambda i, dr: (i, 0)),
        scratch_shapes=[pltpu.VMEM((2, TOP_K, rt, d), F32), pltpu.SemaphoreType.DMA((2,))],
    )
    return pl.pallas_call(
        functools.partial(_combine_kernel, final_norm=final_norm),
        out_shape=jax.ShapeDtypeStruct((s, d), F32),
        grid_spec=grid_spec,
        compiler_params=_cparams("arbitrary"),
        name="moe_combine",
    )(dest, *args)


def _routing_tables(experts, gates, tm, n_tiles):
    n_assign = experts.size
    e_flat = experts.reshape(n_assign)
    onehot = (e_flat[:, None] == jnp.arange(N_EXPERTS, dtype=jnp.int32)[None, :]).astype(jnp.int32)
    csum = jnp.cumsum(onehot, axis=0)
    rank = jnp.sum(csum * onehot, axis=1) - 1
    counts = csum[-1]
    padded = ((counts + tm - 1) // tm) * tm
    seg_end = jnp.cumsum(padded)
    seg_start = seg_end - padded
    dest = (seg_start[e_flat] + rank).astype(jnp.int32)
    p = n_tiles * tm
    src = jnp.zeros((p,), jnp.int32).at[dest].set(jnp.arange(n_assign, dtype=jnp.int32) // TOP_K)
    gate_sorted = jnp.zeros((p,), F32).at[dest].set(gates.reshape(n_assign))
    n_used = (seg_end[-1] // tm).astype(jnp.int32)
    tile_start = jnp.arange(n_tiles, dtype=jnp.int32) * tm
    tile_e = jnp.sum((tile_start[:, None] >= seg_end[None, :]).astype(jnp.int32), axis=1)
    tile_e = jnp.minimum(tile_e, N_EXPERTS - 1)
    last_e = tile_e[n_used - 1]
    tile_e = jnp.where(jnp.arange(n_tiles) < n_used, tile_e, last_e).astype(jnp.int32)
    return src, dest, gate_sorted, tile_e, n_used.reshape(1)


def moe(x, g, router_w, w1, w3, w2, final_g=None):
    s, _ = x.shape
    info = router(x, g, router_w)
    experts = info[:, 0:TOP_K].astype(jnp.int32)
    gates = info[:, TOP_K:2 * TOP_K]
    tm = MOE_TM
    n_tiles = (TOP_K * s) // tm + N_EXPERTS
    src, dest, gate_sorted, tile_e, n_used = _routing_tables(experts, gates, tm, n_tiles)
    y = moe_grouped(x, src, g, gate_sorted, tile_e, n_used, w1.astype(BF16), w3.astype(BF16), w2.astype(BF16))
    return combine(x, y, dest, final_g)


def _final_norm_kernel(x_ref, g_ref, o_ref):
    o_ref[...] = _rms(x_ref[...], g_ref[...])


def final_norm(x, g):
    s, d = x.shape
    tm = min(ROW_TILE, s)
    return pl.pallas_call(
        _final_norm_kernel,
        out_shape=jax.ShapeDtypeStruct((s, d), F32),
        grid=(s // tm,),
        in_specs=[pl.BlockSpec((tm, d), lambda i: (i, 0)), pl.BlockSpec((1, d), lambda i: (0, 0))],
        out_specs=pl.BlockSpec((tm, d), lambda i: (i, 0)),
        compiler_params=_cparams("parallel"),
        name="final_norm",
    )(x, g.reshape(1, d))


def _hgrn2_kernel(zq_ref, zf_ref, zi_ref, zg_ref, lb_ref, ng_ref, tri_ref, o_ref,
                  st_ref, q_s, k_s, v_s, cum_s, o_s, dec_s, qd_s, kd_s):
    c_len = HGRN_C
    heads, t_len, hd = q_s.shape

    @pl.when(pl.program_id(0) == 0)
    def _():
        st_ref[...] = jnp.zeros_like(st_ref)

    lb = lb_ref[...]
    f = lb + (1.0 - lb) * _sigmoid(zf_ref[...])
    lf = jnp.log(f)
    lf_hi = lf.astype(BF16)
    lf_lo = (lf - lf_hi.astype(F32)).astype(BF16)
    tri = tri_ref[...]
    cum = (jnp.dot(tri, lf_hi, preferred_element_type=F32) + jnp.dot(tri, lf_lo, preferred_element_type=F32))
    cum3 = cum.reshape(t_len // c_len, c_len, heads * hd)
    last3 = cum3[:, c_len - 1:c_len, :]
    last = jnp.broadcast_to(last3, cum3.shape).reshape(t_len, heads * hd)
    dec = jnp.exp(last3.reshape(t_len // c_len, heads * hd))
    q = _silu(zq_ref[...])
    k = 1.0 - f
    qd = (q * jnp.exp(cum)).astype(BF16)
    kd = (k * jnp.exp(last - cum)).astype(BF16)
    v = zi_ref[...]
    for h in range(heads):
        sl = slice(h * hd, (h + 1) * hd)
        q_s[h] = q[:, sl]
        k_s[h] = k[:, sl]
        v_s[h] = v[:, sl]
        cum_s[h] = cum[:, sl]
        dec_s[h] = dec[:, sl]
        qd_s[h] = qd[:, sl]
        kd_s[h] = kd[:, sl]

    half = c_len // 2
    row_lo = lax.broadcasted_iota(jnp.int32, (half, hd), 0)
    lane = lax.broadcasted_iota(jnp.int32, (half, hd), 1)

    def body(c, carry):
        r0 = pl.multiple_of(c * c_len, c_len)
        for h in range(heads):
            rows = pl.ds(r0, c_len)
            q_c = q_s[h, rows, :]
            k_c = k_s[h, rows, :]
            cm = cum_s[h, rows, :]
            v_c = v_s[h, rows, :].astype(BF16)
            m_lo = jnp.zeros((half, hd), F32)
            m_hi = jnp.zeros((half, hd), F32)
            for s in range(c_len):
                ks = k_c[s:s + 1, :]
                cs = cm[s:s + 1, :]
                if s < half:
                    e = jnp.where(row_lo >= s, jnp.exp(cm[:half] - cs), 0.0)
                    col = jnp.sum(q_c[:half] * (ks * e), axis=1, keepdims=True)
                    m_lo = jnp.where(lane == s, col, m_lo)
                    e = jnp.exp(cm[half:] - cs)
                else:
                    e = jnp.where(row_lo >= s - half, jnp.exp(cm[half:] - cs), 0.0)
                col = jnp.sum(q_c[half:] * (ks * e), axis=1, keepdims=True)
                m_hi = jnp.where(lane == s, col, m_hi)
            scores = jnp.concatenate([m_lo, m_hi], axis=0)[:, :c_len].astype(BF16)
            st = st_ref[h]
            o = (jnp.dot(scores, v_c, preferred_element_type=F32)
                 + lax.dot_general(qd_s[h, rows, :], st.astype(BF16), (((1,), (1,)), ((), ())),
                                   preferred_element_type=F32))
            o_s[h, rows, :] = o
            upd = lax.dot_general(v_c, kd_s[h, rows, :], (((0,), (0,)), ((), ())),
                                  preferred_element_type=F32)
            st_ref[h] = st * dec_s[h, pl.ds(c, 1), :] + upd
        return carry

    lax.fori_loop(0, t_len // c_len, body, 0)
    ng = ng_ref[...]
    sg = _silu(zg_ref[...])
    for h in range(heads):
        sl = slice(h * hd, (h + 1) * hd)
        o = o_s[h]
        o = o * lax.rsqrt(jnp.mean(o * o, axis=-1, keepdims=True) + EPS)
        o_ref[:, sl] = (o * ng[:, sl] * sg[:, sl]).astype(BF16)


def hgrn2(z, col0, lb, norm_g):
    s = z.shape[0]
    t_len = min(HGRN_T, s)
    heads = W_MIX // A_HEAD_DIM
    cb = col0 // W_MIX
    r = jnp.arange(t_len)
    same = r[:, None] // HGRN_C == r[None, :] // HGRN_C
    tri = (same & (r[None, :] <= r[:, None])).astype(BF16)

    def zspec(k):
        return pl.BlockSpec((t_len, W_MIX), lambda i: (i, cb + k))

    vec = pl.BlockSpec((1, W_MIX), lambda i: (0, 0))
    sq = pl.BlockSpec((t_len, t_len), lambda i: (0, 0))
    per_head = lambda dt: pltpu.VMEM((heads, t_len, A_HEAD_DIM), dt)
    return pl.pallas_call(
        _hgrn2_kernel,
        out_shape=jax.ShapeDtypeStruct((s, W_MIX), BF16),
        grid=(s // t_len,),
        in_specs=[zspec(0), zspec(1), zspec(2), zspec(3), vec, vec, sq],
        out_specs=pl.BlockSpec((t_len, W_MIX), lambda i: (i, 0)),
        scratch_shapes=[pltpu.VMEM((heads, A_HEAD_DIM, A_HEAD_DIM), F32)]
        + [per_head(F32)] * 5 + [pltpu.VMEM((heads, t_len // HGRN_C, A_HEAD_DIM), F32)] + [per_head(BF16)] * 2,
        compiler_params=_cparams("arbitrary"),
        name="hgrn2",
    )(z, z, z, z, lb.reshape(1, W_MIX), norm_g.reshape(1, W_MIX), tri)


def _s5_kernel(u_ref, wb_ref, pw_ref, wc_ref, d_ref, gw_ref, gb_ref, o_ref,
               carry_ref, bu_s, y_s):
    i = pl.program_id(0)
    j = pl.program_id(1)
    n_slab = pl.num_programs(1)
    t_len = bu_s.shape[0]
    half = bu_s.shape[1] // 2

    @pl.when(i == 0)
    def _():
        carry_ref[j] = jnp.zeros((SUBLANES, 2 * half), F32)

    u = u_ref[...]
    bu_s[...] = jnp.dot(u.astype(BF16), wb_ref[0], preferred_element_type=F32)
    p8_re, p8_im = pw_ref[0, 0:8, :half], pw_ref[0, 0:8, half:]

    def body(b, carry):
        c_re, c_im = carry
        r0 = pl.multiple_of(b * SUBLANES, SUBLANES)
        blk = bu_s[pl.ds(r0, SUBLANES), :]
        x_re, x_im = blk[:, :half], blk[:, half:]
        for step, k in enumerate((1, 2, 4)):
            a_re = pw_ref[0, 8 * (step + 1):8 * (step + 2), :half]
            a_im = pw_ref[0, 8 * (step + 1):8 * (step + 2), half:]
            s_re = pltpu.roll(x_re, k, 0)
            s_im = pltpu.roll(x_im, k, 0)
            x_re, x_im = (x_re + a_re * s_re - a_im * s_im,
                          x_im + a_re * s_im + a_im * s_re)
        x_re, x_im = (x_re + p8_re * c_re - p8_im * c_im,
                      x_im + p8_re * c_im + p8_im * c_re)
        bu_s[pl.ds(r0, SUBLANES), :] = jnp.concatenate([x_re, x_im], axis=1)
        n_re = jnp.broadcast_to(x_re[SUBLANES - 1:SUBLANES, :], x_re.shape)
        n_im = jnp.broadcast_to(x_im[SUBLANES - 1:SUBLANES, :], x_im.shape)
        return n_re, n_im

    c0 = carry_ref[j]
    c_re, c_im = lax.fori_loop(0, t_len // SUBLANES, body, (c0[:, :half], c0[:, half:]))
    carry_ref[j] = jnp.concatenate([c_re, c_im], axis=1)

    y = jnp.dot(bu_s[...].astype(BF16), wc_ref[0], preferred_element_type=F32)
    y = _gelu_tanh(y + d_ref[0] * u)
    y_s[j] = y

    @pl.when(j == n_slab - 1)
    def _():
        yf = jnp.concatenate([y_s[s] for s in range(y_s.shape[0])], axis=1)
        gate = jnp.dot(yf.astype(BF16), gw_ref[...], preferred_element_type=F32) + gb_ref[...]
        o_ref[...] = (yf * _sigmoid(gate)).astype(BF16)


def s5(z, col0, lam_re, lam_im, log_dt, b_re, b_im, c_re, c_im, d_skip, glu_w, glu_b):
    s = z.shape[0]
    t_len = min(S5_T, s)
    groups, n_state = lam_re.shape
    gps = S5_SLAB // S5_GROUP
    n_slab = groups // gps
    half = gps * n_state
    cb = col0 // S5_SLAB
    lam_re = lam_re.astype(F32)
    lam_im = lam_im.astype(F32)
    dt = jnp.exp(log_dt.astype(F32))[:, None]
    mag = jnp.exp(lam_re * dt)
    ang = lam_im * dt
    ab_re = mag * jnp.cos(ang)
    ab_im = mag * jnp.sin(ang)
    den = lam_re * lam_re + lam_im * lam_im
    num_re = ab_re - 1.0
    coef_re = (num_re * lam_re + ab_im * lam_im) / den
    coef_im = (ab_im * lam_re - num_re * lam_im) / den
    br = b_re.astype(F32)
    bi = b_im.astype(F32)
    bb_re = coef_re[..., None] * br - coef_im[..., None] * bi
    bb_im = coef_re[..., None] * bi + coef_im[..., None] * br
    eye = jnp.eye(gps, dtype=F32)

    def blockdiag_in(bb):
        t = bb.reshape(n_slab, gps, n_state, S5_GROUP)
        return jnp.einsum('sgnp,gh->sgphn', t, eye).reshape(n_slab, gps * S5_GROUP, gps * n_state)

    wb = jnp.concatenate([blockdiag_in(bb_re), blockdiag_in(bb_im)], axis=-1).astype(BF16)

    def blockdiag_out(cc):
        t = cc.reshape(n_slab, gps, S5_GROUP, n_state)
        return jnp.einsum('sgpn,gh->sgnhp', t, eye).reshape(n_slab, gps * n_state, gps * S5_GROUP)

    wc = jnp.concatenate([blockdiag_out(c_re.astype(F32)), -blockdiag_out(c_im.astype(F32))],
                         axis=1).astype(BF16)
    r8 = jnp.arange(SUBLANES)
    expo = jnp.concatenate([r8 + 1.0] + [jnp.full((SUBLANES,), float(k)) for k in (1, 2, 4)]).astype(F32)
    keep = jnp.concatenate([jnp.ones((SUBLANES,), F32)] + [(r8 >= k).astype(F32) for k in (1, 2, 4)])
    expo = expo[:, None, None]
    p_mag = jnp.exp(expo * (lam_re * dt)[None]) * keep[:, None, None]
    p_re = (p_mag * jnp.cos(expo * ang[None])).reshape(S5_PW_ROWS, n_slab, half)
    p_im = (p_mag * jnp.sin(expo * ang[None])).reshape(S5_PW_ROWS, n_slab, half)
    pw = jnp.moveaxis(jnp.concatenate([p_re, p_im], axis=-1), 1, 0)

    return pl.pallas_call(
        _s5_kernel,
        out_shape=jax.ShapeDtypeStruct((s, W_MIX), BF16),
        grid=(s // t_len, n_slab),
        in_specs=[
            pl.BlockSpec((t_len, S5_SLAB), lambda i, j: (i, cb + j)),
            pl.BlockSpec((1, S5_SLAB, 2 * half), lambda i, j: (j, 0, 0)),
            pl.BlockSpec((1, S5_PW_ROWS, 2 * half), lambda i, j: (j, 0, 0)),
            pl.BlockSpec((1, 2 * half, S5_SLAB), lambda i, j: (j, 0, 0)),
            pl.BlockSpec((1, 1, S5_SLAB), lambda i, j: (j, 0, 0)),
            pl.BlockSpec((W_MIX, W_MIX), lambda i, j: (0, 0)),
            pl.BlockSpec((1, W_MIX), lambda i, j: (0, 0)),
        ],
        out_specs=pl.BlockSpec((t_len, W_MIX), lambda i, j: (i, 0)),
        scratch_shapes=[pltpu.VMEM((n_slab, SUBLANES, 2 * half), F32),
                        pltpu.VMEM((t_len, 2 * half), F32),
                        pltpu.VMEM((n_slab, t_len, S5_SLAB), F32)],
        compiler_params=_cparams("arbitrary", "arbitrary"),
        name="s5",
    )(z, wb, pw, wc, d_skip.astype(F32).reshape(n_slab, 1, S5_SLAB), glu_w.astype(BF16),
      glu_b.astype(F32).reshape(1, W_MIX))


def _retention_kernel(zq_ref, zk_ref, zv_ref, zg_ref, cos_ref, sin_ref, dmat_ref, qdec_ref, kdec_ref,
                      cdec_ref, ng_ref, o_ref, st_ref):
    @pl.when(pl.program_id(1) == 0)
    def _():
        st_ref[...] = jnp.zeros_like(st_ref)

    half = cos_ref.shape[1]
    cos = cos_ref[...]
    sin = sin_ref[...]

    def rope(t):
        t1, t2 = t[:, :half], t[:, half:]
        return jnp.concatenate([t1 * cos - t2 * sin, t1 * sin + t2 * cos], axis=1)

    q = rope(zq_ref[...])
    k = rope(zk_ref[...]) * (2 * half) ** -0.5
    v = zv_ref[...].astype(BF16)
    qb = q.astype(BF16)
    scores = lax.dot_general(qb, k.astype(BF16), (((1,), (1,)), ((), ())),
                             preferred_element_type=F32) * dmat_ref[0]
    st = st_ref[...]
    o = (jnp.dot(scores.astype(BF16), v, preferred_element_type=F32)
         + jnp.dot(qb, st.astype(BF16), preferred_element_type=F32) * qdec_ref[0])
    kd = (k * kdec_ref[0]).astype(BF16)
    st_ref[...] = cdec_ref[0] * st + lax.dot_general(kd, v, (((0,), (0,)), ((), ())),
                                                     preferred_element_type=F32)
    mu = jnp.mean(o, axis=-1, keepdims=True)
    oc = o - mu
    var = jnp.mean(oc * oc, axis=-1, keepdims=True)
    o = oc * lax.rsqrt(var + EPS)
    o_ref[...] = (o * ng_ref[...] * _silu(zg_ref[...])).astype(BF16)


def retention(z, col0, norm_g):
    s = z.shape[0]
    t_len = min(RET_T, s)
    hd = W_MIX // C_HEADS
    cb = col0 // hd
    pos = jnp.arange(s, dtype=F32)
    inv_freq = ROPE_THETA ** (-jnp.arange(0, hd, 2, dtype=F32) / hd)
    ang = pos[:, None] * inv_freq[None, :]
    cos = jnp.cos(ang)
    sin = jnp.sin(ang)
    log_gamma = jnp.log(1.0 - 2.0 ** (-5.0 - jnp.arange(C_HEADS, dtype=F32)))
    idx = jnp.arange(t_len, dtype=F32)
    rel = idx[:, None] - idx[None, :]
    dmat = jnp.where(rel[None] >= 0, jnp.exp(jnp.maximum(rel, 0.0)[None] * log_gamma[:, None, None]), 0.0)
    qdec = jnp.exp((idx + 1.0)[None, :] * log_gamma[:, None])[..., None]
    kdec = jnp.exp((t_len - 1.0 - idx)[None, :] * log_gamma[:, None])[..., None]
    cdec = jnp.broadcast_to(jnp.exp(t_len * log_gamma)[:, None, None], (C_HEADS, 1, hd))

    def zspec(k):
        return pl.BlockSpec((t_len, hd), lambda h, i: (i, cb + k * C_HEADS + h))

    tab = pl.BlockSpec((t_len, hd // 2), lambda h, i: (i, 0))
    return pl.pallas_call(
        _retention_kernel,
        out_shape=jax.ShapeDtypeStruct((s, W_MIX), BF16),
        grid=(C_HEADS, s // t_len),
        in_specs=[zspec(0), zspec(1), zspec(2), zspec(3), tab, tab,
                  pl.BlockSpec((1, t_len, t_len), lambda h, i: (h, 0, 0)),
                  pl.BlockSpec((1, t_len, 1), lambda h, i: (h, 0, 0)),
                  pl.BlockSpec((1, t_len, 1), lambda h, i: (h, 0, 0)),
                  pl.BlockSpec((1, 1, hd), lambda h, i: (h, 0, 0)),
                  pl.BlockSpec((1, hd), lambda h, i: (0, h))],
        out_specs=pl.BlockSpec((t_len, hd), lambda h, i: (i, h)),
        scratch_shapes=[pltpu.VMEM((hd, hd), F32)],
        compiler_params=_cparams("parallel", "arbitrary"),
        name="retention",
    )(z, z, z, z, cos, sin, dmat, qdec, kdec, cdec, norm_g.reshape(1, W_MIX))


def _rglru_kernel(zg_ref, zx_ref, cw_ref, cb_ref, wa_ref, ba_ref, wx_ref, bx_ref, sp_ref, o_ref,
                  xbuf, h_ref, a_s, u_s):
    t_len = zx_ref.shape[0]

    @pl.when(pl.program_id(0) == 0)
    def _():
        xbuf[0:SUBLANES, :] = jnp.zeros((SUBLANES, W_MIX), F32)
        h_ref[...] = jnp.zeros_like(h_ref)

    xbuf[SUBLANES:, :] = zx_ref[...]
    xc = cb_ref[...]
    for tap in range(CONV_WIDTH):
        off = SUBLANES - (CONV_WIDTH - 1) + tap
        xc = xc + xbuf[off:off + t_len, :] * cw_ref[tap:tap + 1, :]
    xbuf[0:SUBLANES, :] = xbuf[t_len:t_len + SUBLANES, :]
    xcb = xc.astype(BF16)
    n_blk = W_MIX // D_BLOCK
    pre_r = jnp.concatenate(
        [jnp.dot(xcb[:, b * D_BLOCK:(b + 1) * D_BLOCK], wa_ref[b], preferred_element_type=F32)
         for b in range(n_blk)], axis=1)
    pre_i = jnp.concatenate(
        [jnp.dot(xcb[:, b * D_BLOCK:(b + 1) * D_BLOCK], wx_ref[b], preferred_element_type=F32)
         for b in range(n_blk)], axis=1)
    r = _sigmoid(pre_r + ba_ref[...])
    gi = _sigmoid(pre_i + bx_ref[...])
    log_a = -RG_C * r * sp_ref[...]
    a = jnp.exp(log_a)
    a_s[...] = a
    u_s[...] = jnp.sqrt(1.0 - a * a) * (gi * xc)
    row = lax.broadcasted_iota(jnp.int32, (SUBLANES, W_MIX), 0)

    def body(b, h):
        r0 = pl.multiple_of(b * SUBLANES, SUBLANES)
        aa = a_s[pl.ds(r0, SUBLANES), :]
        uu = u_s[pl.ds(r0, SUBLANES), :]
        for k in (1, 2, 4):
            us = jnp.where(row >= k, pltpu.roll(uu, k, 0), 0.0)
            as_ = jnp.where(row >= k, pltpu.roll(aa, k, 0), 1.0)
            uu = uu + aa * us
            aa = aa * as_
        hh = uu + aa * h
        u_s[pl.ds(r0, SUBLANES), :] = hh
        return jnp.broadcast_to(hh[SUBLANES - 1:SUBLANES, :], hh.shape)

    h_ref[...] = lax.fori_loop(0, t_len // SUBLANES, body, h_ref[...])
    o_ref[...] = (_gelu_tanh(zg_ref[...]) * u_s[...]).astype(BF16)


def rglru(z, col0, conv_w, conv_b, w_a, b_a, w_x, b_x, lam):
    s = z.shape[0]
    t_len = min(RG_T, s)
    cb = col0 // W_MIX
    sp = jax.nn.softplus(-lam.astype(F32)).reshape(1, W_MIX)
    row = lambda a: a.astype(F32).reshape(1, W_MIX)
    full = lambda shape: pl.BlockSpec(shape, lambda i: (0,) * len(shape))
    n_blk = W_MIX // D_BLOCK
    return pl.pallas_call(
        _rglru_kernel,
        out_shape=jax.ShapeDtypeStruct((s, W_MIX), BF16),
        grid=(s // t_len,),
        in_specs=[pl.BlockSpec((t_len, W_MIX), lambda i: (i, cb)),
                  pl.BlockSpec((t_len, W_MIX), lambda i: (i, cb + 1)),
                  full((CONV_WIDTH, W_MIX)), full((1, W_MIX)),
                  full((n_blk, D_BLOCK, D_BLOCK)), full((1, W_MIX)),
                  full((n_blk, D_BLOCK, D_BLOCK)), full((1, W_MIX)), full((1, W_MIX))],
        out_specs=pl.BlockSpec((t_len, W_MIX), lambda i: (i, 0)),
        scratch_shapes=[pltpu.VMEM((t_len + SUBLANES, W_MIX), F32),
                        pltpu.VMEM((SUBLANES, W_MIX), F32),
                        pltpu.VMEM((t_len, W_MIX), F32),
                        pltpu.VMEM((t_len, W_MIX), F32)],
        compiler_params=_cparams("arbitrary"),
        name="rglru",
    )(z, z, conv_w.astype(F32), row(conv_b), w_a.astype(BF16), row(b_a), w_x.astype(BF16), row(b_x), sp)


def kernel(x, norm_mix_g, norm_ffn_g, final_norm_g, w_in, w_out, hgrn_lb_logits, hgrn_norm_g, s5_lambda_re, s5_lambda_im, s5_log_dt, s5_b_re, s5_b_im, s5_c_re, s5_c_im, s5_d, s5_glu_w, s5_glu_b, ret_norm_g, rg_conv_w, rg_conv_b, rg_w_a, rg_b_a, rg_w_x, rg_b_x, rg_lambda, ffn_w1, ffn_w3, ffn_w2, router_w, moe_w1, moe_w3, moe_w2):
    b_, s_, d_ = x.shape
    depth = w_in.shape[0]
    xs = x.reshape(b_ * s_, d_).astype(F32)
    lb_p = jax.nn.softmax(hgrn_lb_logits.astype(F32), axis=0)
    lb_all = jnp.cumsum(lb_p, axis=0) - lb_p[0]
    col_a, col_b, col_c, col_d = 0, 4 * W_MIX, 5 * W_MIX, 9 * W_MIX
    for layer in range(depth):
        z = norm_matmul(xs, norm_mix_g[layer], w_in[layer].astype(BF16))
        o_a = hgrn2(z, col_a, lb_all[layer], hgrn_norm_g[layer])
        o_b = s5(z, col_b, s5_lambda_re[layer], s5_lambda_im[layer], s5_log_dt[layer],
                 s5_b_re[layer], s5_b_im[layer], s5_c_re[layer], s5_c_im[layer],
                 s5_d[layer], s5_glu_w[layer], s5_glu_b[layer])
        o_c = retention(z, col_c, ret_norm_g[layer])
        o_d = rglru(z, col_d, rg_conv_w[layer], rg_conv_b[layer], rg_w_a[layer], rg_b_a[layer],
                    rg_w_x[layer], rg_b_x[layer], rg_lambda[layer])
        xs = out_proj((o_a, o_b, o_c, o_d), w_out[layer].astype(BF16), xs)
        if layer % 2 == 0:
            m = layer // 2
            xs = ffn(xs, norm_ffn_g[layer], ffn_w1[m].astype(BF16), ffn_w3[m].astype(BF16),
                     ffn_w2[m].astype(BF16))
        else:
            m = layer // 2
            last = layer == depth - 1
            xs = moe(xs, norm_ffn_g[layer], router_w[m], moe_w1[m], moe_w3[m], moe_w2[m],
                     final_norm_g if last else None)
            if last:
                return xs.reshape(b_, s_, d_)
    return final_norm(xs, final_norm_g).reshape(b_, s_, d_)
```

```python
import functools
import math

import jax
import jax.numpy as jnp
from jax import lax
from jax.experimental import pallas as pl
from jax.experimental.pallas import tpu as pltpu

F32 = jnp.float32
BF16 = jnp.bfloat16
EPS = 1e-6
HIGHEST = lax.Precision.HIGHEST

V7X_VMEM_BYTES = 64 * 1024 * 1024
VMEM_LIMIT_BYTES = V7X_VMEM_BYTES - 4 * 1024 * 1024
SUBLANES = 8
LANES = 128

A_HEAD_DIM = 128
S5_GROUP = 16
S5_STATE = 64
C_HEADS = 4
ROPE_THETA = 10000.0
D_BLOCK = 128
CONV_WIDTH = 4
RG_C = 8.0
N_EXPERTS = 8
TOP_K = 2
W_MIX = 1024

ROW_TILE = 512
IN_PROJ_TN = 1024
OUT_PROJ_TN = 1024
FFN_TF = 512
MOE_TM = 512
COMBINE_ROWS = 256
HGRN_T = 256
HGRN_C = 16
RET_T = 256
S5_T = 256
S5_SLAB = 128
S5_PW_ROWS = 4 * SUBLANES
RG_T = 256


def _cparams(*sem):
    return pltpu.CompilerParams(dimension_semantics=sem, vmem_limit_bytes=VMEM_LIMIT_BYTES)


def _rms(xf, g):
    return xf * lax.rsqrt(jnp.mean(xf * xf, axis=-1, keepdims=True) + EPS) * g


def _sigmoid(x):
    return 0.5 + 0.5 * jnp.tanh(0.5 * x)


def _silu(x):
    h = 0.5 * x
    return h + h * jnp.tanh(h)


def _gelu_tanh(x):
    c = math.sqrt(2.0 / math.pi)
    return 0.5 * x * (1.0 + jnp.tanh(c * (x + 0.044715 * (x * x * x))))


def _norm_matmul_kernel(x_ref, g_ref, w_ref, o_ref, h_ref):
    @pl.when(pl.program_id(1) == 0)
    def _():
        h_ref[...] = _rms(x_ref[...], g_ref[...]).astype(BF16)

    o_ref[...] = jnp.dot(h_ref[...], w_ref[...], preferred_element_type=F32)


def norm_matmul(x, g, w):
    s, d = x.shape
    n = w.shape[1]
    tm, tn = min(ROW_TILE, s), IN_PROJ_TN
    return pl.pallas_call(
        _norm_matmul_kernel,
        out_shape=jax.ShapeDtypeStruct((s, n), F32),
        grid=(s // tm, n // tn),
        in_specs=[
            pl.BlockSpec((tm, d), lambda i, j: (i, 0)),
            pl.BlockSpec((1, d), lambda i, j: (0, 0)),
            pl.BlockSpec((d, tn), lambda i, j: (0, j)),
        ],
        out_specs=pl.BlockSpec((tm, tn), lambda i, j: (i, j)),
        scratch_shapes=[pltpu.VMEM((tm, d), BF16)],
        compiler_params=_cparams("parallel", "arbitrary"),
        name="norm_in_proj",
    )(x, g.reshape(1, d), w)


def _out_proj_kernel(oa_ref, ob_ref, oc_ref, od_ref, w_ref, x_ref, o_ref):
    acc = x_ref[...]
    for idx, r in enumerate((oa_ref, ob_ref, oc_ref, od_ref)):
        acc = acc + jnp.dot(r[...], w_ref[idx * W_MIX:(idx + 1) * W_MIX, :], preferred_element_type=F32)
    o_ref[...] = acc


def out_proj(parts, w, x):
    s, d = x.shape
    tm, tn = min(ROW_TILE, s), OUT_PROJ_TN
    part_spec = pl.BlockSpec((tm, W_MIX), lambda i, j: (i, 0))
    return pl.pallas_call(
        _out_proj_kernel,
        out_shape=jax.ShapeDtypeStruct((s, d), F32),
        grid=(s // tm, d // tn),
        in_specs=[part_spec, part_spec, part_spec, part_spec,
                  pl.BlockSpec((4 * W_MIX, tn), lambda i, j: (0, j)),
                  pl.BlockSpec((tm, tn), lambda i, j: (i, j))],
        out_specs=pl.BlockSpec((tm, tn), lambda i, j: (i, j)),
        compiler_params=_cparams("parallel", "arbitrary"),
        name="out_proj",
    )(*parts, w, x)


def _tile_gate_up(w1, w3, tf):
    *lead, d, f = w1.shape
    nf = f // tf
    both = jnp.concatenate([w1.astype(BF16).reshape(*lead, d, nf, tf), w3.astype(BF16).reshape(*lead, d, nf, tf)],
                           axis=-1)
    return jnp.swapaxes(both, -3, -2)


def _swiglu_step(h, w13, w2):
    tf = w2.shape[0]
    ab = jnp.dot(h, w13, preferred_element_type=F32)
    act = (_silu(ab[:, :tf]) * ab[:, tf:]).astype(BF16)
    return jnp.dot(act, w2, preferred_element_type=F32)


def _ffn_kernel(x_ref, g_ref, w13_ref, w2_ref, o_ref, h_ref):
    @pl.when(pl.program_id(1) == 0)
    def _():
        xf = x_ref[...]
        h_ref[...] = _rms(xf, g_ref[...]).astype(BF16)
        o_ref[...] = xf

    o_ref[...] += _swiglu_step(h_ref[...], w13_ref[0], w2_ref[...])


def ffn(x, g, w1, w3, w2):
    s, d = x.shape
    dff = w1.shape[1]
    tm, tf = min(ROW_TILE, s), FFN_TF
    w13 = _tile_gate_up(w1, w3, tf)
    w2 = w2.astype(BF16)
    return pl.pallas_call(
        _ffn_kernel,
        out_shape=jax.ShapeDtypeStruct((s, d), F32),
        grid=(s // tm, dff // tf),
        in_specs=[
            pl.BlockSpec((tm, d), lambda i, f: (i, 0), pipeline_mode=pl.Buffered(1)),
            pl.BlockSpec((1, d), lambda i, f: (0, 0)),
            pl.BlockSpec((1, d, 2 * tf), lambda i, f: (f, 0, 0)),
            pl.BlockSpec((tf, d), lambda i, f: (f, 0)),
        ],
        out_specs=pl.BlockSpec((tm, d), lambda i, f: (i, 0), pipeline_mode=pl.Buffered(1)),
        scratch_shapes=[pltpu.VMEM((tm, d), BF16)],
        compiler_params=_cparams("parallel", "arbitrary"),
        name="ffn_swiglu",
    )(x, g.reshape(1, d), w13, w2)


def _router_kernel(x_ref, g_ref, wr_ref, comb_ref):
    h = _rms(x_ref[...], g_ref[...])
    logits = jnp.dot(h, wr_ref[...], precision=HIGHEST, preferred_element_type=F32)
    lane = lax.broadcasted_iota(jnp.int32, logits.shape, 1)
    neg = jnp.float32(-jnp.inf)
    logits = jnp.where(lane < N_EXPERTS, logits, neg)
    v1 = jnp.max(logits, axis=-1, keepdims=True)
    i1 = jnp.min(jnp.where(logits == v1, lane, LANES), axis=-1, keepdims=True)
    rest = jnp.where(lane == i1, neg, logits)
    v2 = jnp.max(rest, axis=-1, keepdims=True)
    i2 = jnp.min(jnp.where(rest == v2, lane, LANES), axis=-1, keepdims=True)
    e2 = jnp.exp(v2 - v1)
    g1 = 1.0 / (1.0 + e2)
    g2 = e2 / (1.0 + e2)
    comb_ref[...] = (jnp.where(lane == 0, i1.astype(F32), 0.0) + jnp.where(lane == 1, i2.astype(F32), 0.0)
                     + jnp.where(lane == 2, g1, 0.0) + jnp.where(lane == 3, g2, 0.0))


def router(x, g, router_w):
    s, d = x.shape
    tm = min(ROW_TILE, s)
    wr = jnp.zeros((d, LANES), F32).at[:, :N_EXPERTS].set(router_w.astype(F32))
    return pl.pallas_call(
        _router_kernel,
        out_shape=jax.ShapeDtypeStruct((s, LANES), F32),
        grid=(s // tm,),
        in_specs=[pl.BlockSpec((tm, d), lambda i: (i, 0)),
                  pl.BlockSpec((1, d), lambda i: (0, 0)),
                  pl.BlockSpec((d, LANES), lambda i: (0, 0))],
        out_specs=pl.BlockSpec((tm, LANES), lambda i: (i, 0)),
        compiler_params=_cparams("parallel"),
        name="moe_router",
    )(x, g.reshape(1, d), wr)


def _row_copy(src_hbm, src_row, dst_ref, dst_row, sem):
    return pltpu.make_async_copy(src_hbm.at[pl.ds(src_row, 1)], dst_ref.at[pl.ds(dst_row, 1)], sem)


def _moe_grouped_kernel(te_ref, nu_ref, src_ref, x_hbm, g_ref, gate_ref, w13_ref, w2_ref, o_ref,
                        xbuf, h_ref, sem):
    i = pl.program_id(0)
    f = pl.program_id(1)
    n_used = nu_ref[0]
    used = i < n_used
    tm = xbuf.shape[0]

    def start_gather(tile):
        def issue(r, c):
            _row_copy(x_hbm, src_ref[tile * tm + r], xbuf, r, sem).start()
            return c

        lax.fori_loop(0, tm, issue, 0, unroll=8)

    @pl.when((i == 0) & (f == 0))
    def _():
        start_gather(0)

    @pl.when(jnp.logical_not(used) & (f == 0))
    def _():
        o_ref[...] = jnp.zeros_like(o_ref)

    @pl.when(used & (f == 0))
    def _():
        def wait(r, c):
            _row_copy(x_hbm, 0, xbuf, r, sem).wait()
            return c

        lax.fori_loop(0, tm, wait, 0, unroll=8)
        h_ref[...] = _rms(xbuf[...], g_ref[...]).astype(BF16)
        o_ref[...] = _swiglu_step(h_ref[...], w13_ref[0, 0], w2_ref[0])

    @pl.when((f == 1) & (i + 1 < n_used))
    def _():
        start_gather(i + 1)

    @pl.when(used & (f > 0))
    def _():
        o_ref[...] += _swiglu_step(h_ref[...], w13_ref[0, 0], w2_ref[0])

    @pl.when(used & (f == pl.num_programs(1) - 1))
    def _():
        o_ref[...] = o_ref[...] * gate_ref[...]


def moe_grouped(x, src, g, gate_sorted, tile_expert, n_used, w13, w2):
    d = x.shape[1]
    p = src.shape[0]
    nf, tf = w13.shape[1], w2.shape[1] // w13.shape[1]
    tm = MOE_TM

    def row_map(i, f, te, nu, sr):
        return (jnp.minimum(i, nu[0] - 1), 0)

    def f_eff(i, f, nu):
        return jnp.where(i < nu[0], f, nf - 1)

    grid_spec = pltpu.PrefetchScalarGridSpec(
        num_scalar_prefetch=3,
        grid=(p // tm, nf),
        in_specs=[
            pl.BlockSpec(memory_space=pl.ANY),
            pl.BlockSpec((1, d), lambda i, f, te, nu, sr: (0, 0)),
            pl.BlockSpec((tm, 1), row_map),
            pl.BlockSpec((1, 1, d, 2 * tf), lambda i, f, te, nu, sr: (te[i], f_eff(i, f, nu), 0, 0)),
            pl.BlockSpec((1, tf, d), lambda i, f, te, nu, sr: (te[i], f_eff(i, f, nu), 0)),
        ],
        out_specs=pl.BlockSpec((tm, d), lambda i, f, te, nu, sr: (i, 0), pipeline_mode=pl.Buffered(1)),
        scratch_shapes=[pltpu.VMEM((tm, d), F32), pltpu.VMEM((tm, d), BF16), pltpu.SemaphoreType.DMA(())],
    )
    return pl.pallas_call(
        _moe_grouped_kernel,
        out_shape=jax.ShapeDtypeStruct((p, d), F32),
        grid_spec=grid_spec,
        compiler_params=_cparams("arbitrary", "arbitrary"),
        name="moe_grouped",
    )(tile_expert, n_used, src, x, g.reshape(1, d), gate_sorted.reshape(p, 1), w13, w2)


def _combine_kernel(dest_ref, x_ref, y_hbm, *rest, final_norm):
    if final_norm:
        fg_ref, o_ref, buf, sem = rest
    else:
        o_ref, buf, sem = rest
    i = pl.program_id(0)
    n_rows = o_ref.shape[0]
    slot = i % 2

    def start_gather(step, sl):
        def issue(r, c):
            for k in range(TOP_K):
                _row_copy(y_hbm, dest_ref[TOP_K * (step * n_rows + r) + k], buf.at[sl, k], r, sem.at[sl]).start()
            return c

        lax.fori_loop(0, n_rows, issue, 0, unroll=4)

    @pl.when(i == 0)
    def _():
        start_gather(0, 0)

    @pl.when(i + 1 < pl.num_programs(0))
    def _():
        start_gather(i + 1, 1 - slot)

    def wait(r, c):
        for k in range(TOP_K):
            _row_copy(y_hbm, 0, buf.at[slot, k], r, sem.at[slot]).wait()
        return c

    lax.fori_loop(0, n_rows, wait, 0, unroll=4)
    acc = x_ref[...]
    for k in range(TOP_K):
        acc = acc + buf[slot, k]
    o_ref[...] = _rms(acc, fg_ref[...]) if final_norm else acc


def combine(x, y, dest, final_g=None):
    s, d = x.shape
    rt = COMBINE_ROWS
    final_norm = final_g is not None
    in_specs = [pl.BlockSpec((rt, d), lambda i, dr: (i, 0)), pl.BlockSpec(memory_space=pl.ANY)]
    args = [x, y]
    if final_norm:
        in_specs.append(pl.BlockSpec((1, d), lambda i, dr: (0, 0)))
        args.append(final_g.astype(F32).reshape(1, d))
    grid_spec = pltpu.PrefetchScalarGridSpec(
        num_scalar_prefetch=1,
        grid=(s // rt,),
        in_specs=in_specs,
        out_specs=pl.BlockSpec((rt, d), lambda i, dr: (i, 0)),
        scratch_shapes=[pltpu.VMEM((2, TOP_K, rt, d), F32), pltpu.SemaphoreType.DMA((2,))],
    )
    return pl.pallas_call(
        functools.partial(_combine_kernel, final_norm=final_norm),
        out_shape=jax.ShapeDtypeStruct((s, d), F32),
        grid_spec=grid_spec,
        compiler_params=_cparams("arbitrary"),
        name="moe_combine",
    )(dest, *args)


def _routing_tables(experts, gates, tm, n_tiles):
    n_assign = experts.size
    e_flat = experts.reshape(n_assign)
    onehot = (e_flat[:, None] == jnp.arange(N_EXPERTS, dtype=jnp.int32)[None, :]).astype(jnp.int32)
    csum = jnp.cumsum(onehot, axis=0)
    rank = jnp.sum(csum * onehot, axis=1) - 1
    counts = csum[-1]
    padded = ((counts + tm - 1) // tm) * tm
    seg_end = jnp.cumsum(padded)
    seg_start = seg_end - padded
    dest = (seg_start[e_flat] + rank).astype(jnp.int32)
    p = n_tiles * tm
    src = jnp.zeros((p,), jnp.int32).at[dest].set(jnp.arange(n_assign, dtype=jnp.int32) // TOP_K)
    gate_sorted = jnp.zeros((p,), F32).at[dest].set(gates.reshape(n_assign))
    n_used = (seg_end[-1] // tm).astype(jnp.int32)
    tile_start = jnp.arange(n_tiles, dtype=jnp.int32) * tm
    tile_e = jnp.sum((tile_start[:, None] >= seg_end[None, :]).astype(jnp.int32), axis=1)
    tile_e = jnp.minimum(tile_e, N_EXPERTS - 1)
    last_e = tile_e[n_used - 1]
    tile_e = jnp.where(jnp.arange(n_tiles) < n_used, tile_e, last_e).astype(jnp.int32)
    return src, dest, gate_sorted, tile_e, n_used.reshape(1)


def moe(x, g, router_w, w1, w3, w2, final_g=None):
    s, _ = x.shape
    info = router(x, g, router_w)
    experts = info[:, 0:TOP_K].astype(jnp.int32)
    gates = info[:, TOP_K:2 * TOP_K]
    tm = MOE_TM
    n_tiles = (TOP_K * s) // tm + N_EXPERTS
    src, dest, gate_sorted, tile_e, n_used = _routing_tables(experts, gates, tm, n_tiles)
    tf = min(FFN_TF, w1.shape[2] // 2)
    y = moe_grouped(x, src, g, gate_sorted, tile_e, n_used, _tile_gate_up(w1, w3, tf), w2.astype(BF16))
    return combine(x, y, dest, final_g)


def _final_norm_kernel(x_ref, g_ref, o_ref):
    o_ref[...] = _rms(x_ref[...], g_ref[...])


def final_norm(x, g):
    s, d = x.shape
    tm = min(ROW_TILE, s)
    return pl.pallas_call(
        _final_norm_kernel,
        out_shape=jax.ShapeDtypeStruct((s, d), F32),
        grid=(s // tm,),
        in_specs=[pl.BlockSpec((tm, d), lambda i: (i, 0)), pl.BlockSpec((1, d), lambda i: (0, 0))],
        out_specs=pl.BlockSpec((tm, d), lambda i: (i, 0)),
        compiler_params=_cparams("parallel"),
        name="final_norm",
    )(x, g.reshape(1, d))


def _hgrn2_kernel(zq_ref, zf_ref, zi_ref, zg_ref, lb_ref, ng_ref, tri_ref, o_ref,
                  st_ref, q_s, k_s, v_s, cum_s, o_s, dec_s, qd_s, kd_s):
    c_len = HGRN_C
    heads, t_len, hd = q_s.shape

    @pl.when(pl.program_id(0) == 0)
    def _():
        st_ref[...] = jnp.zeros_like(st_ref)

    lb = lb_ref[...]
    f = lb + (1.0 - lb) * _sigmoid(zf_ref[...])
    lf = jnp.log(f)
    lf_hi = lf.astype(BF16)
    lf_lo = (lf - lf_hi.astype(F32)).astype(BF16)
    tri = tri_ref[...]
    cum = (jnp.dot(tri, lf_hi, preferred_element_type=F32) + jnp.dot(tri, lf_lo, preferred_element_type=F32))
    cum3 = cum.reshape(t_len // c_len, c_len, heads * hd)
    last3 = cum3[:, c_len - 1:c_len, :]
    last = jnp.broadcast_to(last3, cum3.shape).reshape(t_len, heads * hd)
    dec = jnp.exp(last3.reshape(t_len // c_len, heads * hd))
    q = _silu(zq_ref[...])
    k = 1.0 - f
    qd = (q * jnp.exp(cum)).astype(BF16)
    kd = (k * jnp.exp(last - cum)).astype(BF16)
    v = zi_ref[...]
    cum2 = cum * math.log2(math.e)
    for h in range(heads):
        sl = slice(h * hd, (h + 1) * hd)
        q_s[h] = q[:, sl]
        k_s[h] = k[:, sl]
        v_s[h] = v[:, sl]
        cum_s[h] = cum2[:, sl]
        dec_s[h] = dec[:, sl]
        qd_s[h] = qd[:, sl]
        kd_s[h] = kd[:, sl]

    half = c_len // 2
    row = lax.broadcasted_iota(jnp.int32, (half, hd), 0)
    lane = lax.broadcasted_iota(jnp.int32, (half, hd), 1)

    def body(c, carry):
        r0 = pl.multiple_of(c * c_len, c_len)
        rows = pl.ds(r0, c_len)
        for h in range(heads):
            q_c = q_s[h, rows, :]
            k_c = k_s[h, rows, :]
            cm = cum_s[h, rows, :]
            m_lo = jnp.zeros((half, hd), F32)
            m_hi = jnp.zeros((half, hd), F32)
            for s in range(c_len):
                ks = k_c[s:s + 1, :]
                cs = cm[s:s + 1, :]
                if s < half:
                    w = q_c[:half] * (ks * jnp.exp2(jnp.minimum(cm[:half] - cs, 0.0)))
                    m_lo = jnp.where(lane == s, jnp.sum(w, axis=1, keepdims=True), m_lo)
                w = q_c[half:] * (ks * jnp.exp2(jnp.minimum(cm[half:] - cs, 0.0)))
                m_hi = jnp.where(lane == s, jnp.sum(w, axis=1, keepdims=True), m_hi)
            m_lo = jnp.where(row >= lane, m_lo, 0.0)
            m_hi = jnp.where(row + half >= lane, m_hi, 0.0)
            scores = jnp.concatenate([m_lo, m_hi], axis=0)[:, :c_len].astype(BF16)
            v_c = v_s[h, rows, :].astype(BF16)
            st = st_ref[h]
            o = (jnp.dot(scores, v_c, preferred_element_type=F32)
                 + lax.dot_general(qd_s[h, rows, :], st.astype(BF16), (((1,), (1,)), ((), ())),
                                   preferred_element_type=F32))
            o_s[h, rows, :] = o
            upd = lax.dot_general(v_c, kd_s[h, rows, :], (((0,), (0,)), ((), ())),
                                  preferred_element_type=F32)
            st_ref[h] = st * dec_s[h, pl.ds(c, 1), :] + upd
        return carry

    lax.fori_loop(0, t_len // c_len, body, 0)
    ng = ng_ref[...]
    sg = _silu(zg_ref[...])
    for h in range(heads):
        sl = slice(h * hd, (h + 1) * hd)
        o = o_s[h]
        o = o * lax.rsqrt(jnp.mean(o * o, axis=-1, keepdims=True) + EPS)
        o_ref[:, sl] = (o * ng[:, sl] * sg[:, sl]).astype(BF16)


def hgrn2(z, col0, lb, norm_g):
    s = z.shape[0]
    t_len = min(HGRN_T, s)
    heads = W_MIX // A_HEAD_DIM
    cb = col0 // W_MIX
    r = jnp.arange(t_len)
    same = r[:, None] // HGRN_C == r[None, :] // HGRN_C
    tri = (same & (r[None, :] <= r[:, None])).astype(BF16)

    def zspec(k):
        return pl.BlockSpec((t_len, W_MIX), lambda i: (i, cb + k))

    vec = pl.BlockSpec((1, W_MIX), lambda i: (0, 0))
    sq = pl.BlockSpec((t_len, t_len), lambda i: (0, 0))
    per_head = lambda dt: pltpu.VMEM((heads, t_len, A_HEAD_DIM), dt)
    return pl.pallas_call(
        _hgrn2_kernel,
        out_shape=jax.ShapeDtypeStruct((s, W_MIX), BF16),
        grid=(s // t_len,),
        in_specs=[zspec(0), zspec(1), zspec(2), zspec(3), vec, vec, sq],
        out_specs=pl.BlockSpec((t_len, W_MIX), lambda i: (i, 0)),
        scratch_shapes=[pltpu.VMEM((heads, A_HEAD_DIM, A_HEAD_DIM), F32)]
        + [per_head(F32)] * 5 + [pltpu.VMEM((heads, t_len // HGRN_C, A_HEAD_DIM), F32)] + [per_head(BF16)] * 2,
        compiler_params=_cparams("arbitrary"),
        name="hgrn2",
    )(z, z, z, z, lb.reshape(1, W_MIX), norm_g.reshape(1, W_MIX), tri)


def _s5_kernel(u_ref, wb_ref, pw_ref, wc_ref, d_ref, gw_ref, gb_ref, o_ref,
               carry_ref, bu_s, y_s):
    i = pl.program_id(0)
    j = pl.program_id(1)
    n_slab = pl.num_programs(1)
    t_len = bu_s.shape[0]
    half = bu_s.shape[1] // 2

    @pl.when(i == 0)
    def _():
        carry_ref[j] = jnp.zeros((SUBLANES, 2 * half), F32)

    u = u_ref[...]
    bu_s[...] = jnp.dot(u.astype(BF16), wb_ref[0], preferred_element_type=F32)
    p8_re, p8_im = pw_ref[0, 0:8, :half], pw_ref[0, 0:8, half:]

    def body(b, carry):
        c_re, c_im = carry
        r0 = pl.multiple_of(b * SUBLANES, SUBLANES)
        blk = bu_s[pl.ds(r0, SUBLANES), :]
        x_re, x_im = blk[:, :half], blk[:, half:]
        for step, k in enumerate((1, 2, 4)):
            a_re = pw_ref[0, 8 * (step + 1):8 * (step + 2), :half]
            a_im = pw_ref[0, 8 * (step + 1):8 * (step + 2), half:]
            s_re = pltpu.roll(x_re, k, 0)
            s_im = pltpu.roll(x_im, k, 0)
            x_re, x_im = (x_re + a_re * s_re - a_im * s_im,
                          x_im + a_re * s_im + a_im * s_re)
        x_re, x_im = (x_re + p8_re * c_re - p8_im * c_im,
                      x_im + p8_re * c_im + p8_im * c_re)
        bu_s[pl.ds(r0, SUBLANES), :] = jnp.concatenate([x_re, x_im], axis=1)
        n_re = jnp.broadcast_to(x_re[SUBLANES - 1:SUBLANES, :], x_re.shape)
        n_im = jnp.broadcast_to(x_im[SUBLANES - 1:SUBLANES, :], x_im.shape)
        return n_re, n_im

    c0 = carry_ref[j]
    c_re, c_im = lax.fori_loop(0, t_len // SUBLANES, body, (c0[:, :half], c0[:, half:]))
    carry_ref[j] = jnp.concatenate([c_re, c_im], axis=1)

    y = jnp.dot(bu_s[...].astype(BF16), wc_ref[0], preferred_element_type=F32)
    y = _gelu_tanh(y + d_ref[0] * u)
    y_s[j] = y

    @pl.when(j == n_slab - 1)
    def _():
        yf = jnp.concatenate([y_s[s] for s in range(y_s.shape[0])], axis=1)
        gate = jnp.dot(yf.astype(BF16), gw_ref[...], preferred_element_type=F32) + gb_ref[...]
        o_ref[...] = (yf * _sigmoid(gate)).astype(BF16)


def s5(z, col0, lam_re, lam_im, log_dt, b_re, b_im, c_re, c_im, d_skip, glu_w, glu_b):
    s = z.shape[0]
    t_len = min(S5_T, s)
    groups, n_state = lam_re.shape
    gps = S5_SLAB // S5_GROUP
    n_slab = groups // gps
    half = gps * n_state
    cb = col0 // S5_SLAB
    lam_re = lam_re.astype(F32)
    lam_im = lam_im.astype(F32)
    dt = jnp.exp(log_dt.astype(F32))[:, None]
    mag = jnp.exp(lam_re * dt)
    ang = lam_im * dt
    ab_re = mag * jnp.cos(ang)
    ab_im = mag * jnp.sin(ang)
    den = lam_re * lam_re + lam_im * lam_im
    num_re = ab_re - 1.0
    coef_re = (num_re * lam_re + ab_im * lam_im) / den
    coef_im = (ab_im * lam_re - num_re * lam_im) / den
    br = b_re.astype(F32)
    bi = b_im.astype(F32)
    bb_re = coef_re[..., None] * br - coef_im[..., None] * bi
    bb_im = coef_re[..., None] * bi + coef_im[..., None] * br
    eye = jnp.eye(gps, dtype=F32)

    def blockdiag_in(bb):
        t = bb.reshape(n_slab, gps, n_state, S5_GROUP)
        return jnp.einsum('sgnp,gh->sgphn', t, eye).reshape(n_slab, gps * S5_GROUP, gps * n_state)

    wb = jnp.concatenate([blockdiag_in(bb_re), blockdiag_in(bb_im)], axis=-1).astype(BF16)

    def blockdiag_out(cc):
        t = cc.reshape(n_slab, gps, S5_GROUP, n_state)
        return jnp.einsum('sgpn,gh->sgnhp', t, eye).reshape(n_slab, gps * n_state, gps * S5_GROUP)

    wc = jnp.concatenate([blockdiag_out(c_re.astype(F32)), -blockdiag_out(c_im.astype(F32))],
                         axis=1).astype(BF16)
    r8 = jnp.arange(SUBLANES)
    expo = jnp.concatenate([r8 + 1.0] + [jnp.full((SUBLANES,), float(k)) for k in (1, 2, 4)]).astype(F32)
    keep = jnp.concatenate([jnp.ones((SUBLANES,), F32)] + [(r8 >= k).astype(F32) for k in (1, 2, 4)])
    expo = expo[:, None, None]
    p_mag = jnp.exp(expo * (lam_re * dt)[None]) * keep[:, None, None]
    p_re = (p_mag * jnp.cos(expo * ang[None])).reshape(S5_PW_ROWS, n_slab, half)
    p_im = (p_mag * jnp.sin(expo * ang[None])).reshape(S5_PW_ROWS, n_slab, half)
    pw = jnp.moveaxis(jnp.concatenate([p_re, p_im], axis=-1), 1, 0)

    return pl.pallas_call(
        _s5_kernel,
        out_shape=jax.ShapeDtypeStruct((s, W_MIX), BF16),
        grid=(s // t_len, n_slab),
        in_specs=[
            pl.BlockSpec((t_len, S5_SLAB), lambda i, j: (i, cb + j)),
            pl.BlockSpec((1, S5_SLAB, 2 * half), lambda i, j: (j, 0, 0)),
            pl.BlockSpec((1, S5_PW_ROWS, 2 * half), lambda i, j: (j, 0, 0)),
            pl.BlockSpec((1, 2 * half, S5_SLAB), lambda i, j: (j, 0, 0)),
            pl.BlockSpec((1, 1, S5_SLAB), lambda i, j: (j, 0, 0)),
            pl.BlockSpec((W_MIX, W_MIX), lambda i, j: (0, 0)),
            pl.BlockSpec((1, W_MIX), lambda i, j: (0, 0)),
        ],
        out_specs=pl.BlockSpec((t_len, W_MIX), lambda i, j: (i, 0)),
        scratch_shapes=[pltpu.VMEM((n_slab, SUBLANES, 2 * half), F32),
                        pltpu.VMEM((t_len, 2 * half), F32),
                        pltpu.VMEM((n_slab, t_len, S5_SLAB), F32)],
        compiler_params=_cparams("arbitrary", "arbitrary"),
        name="s5",
    )(z, wb, pw, wc, d_skip.astype(F32).reshape(n_slab, 1, S5_SLAB), glu_w.astype(BF16),
      glu_b.astype(F32).reshape(1, W_MIX))


def _retention_kernel(zq_ref, zk_ref, zv_ref, zg_ref, cos_ref, sin_ref, dmat_ref, qdec_ref, kdec_ref,
                      cdec_ref, ng_ref, o_ref, st_ref):
    @pl.when(pl.program_id(0) == 0)
    def _():
        st_ref[...] = jnp.zeros_like(st_ref)

    heads, hd, _ = st_ref.shape
    half = hd // 2
    cos = cos_ref[...]
    sin = sin_ref[...]

    def rope(t):
        t1, t2 = t[:, :half], t[:, half:]
        return jnp.concatenate([t1 * cos - t2 * sin, t1 * sin + t2 * cos], axis=1)

    for h in range(heads):
        sl = slice(h * hd, (h + 1) * hd)
        q = rope(zq_ref[:, sl])
        k = rope(zk_ref[:, sl]) * hd ** -0.5
        v = zv_ref[:, sl].astype(BF16)
        qb = q.astype(BF16)
        scores = lax.dot_general(qb, k.astype(BF16), (((1,), (1,)), ((), ())),
                                 preferred_element_type=F32) * dmat_ref[h]
        st = st_ref[h]
        o = (jnp.dot(scores.astype(BF16), v, preferred_element_type=F32)
             + jnp.dot(qb, st.astype(BF16), preferred_element_type=F32) * qdec_ref[h])
        kd = (k * kdec_ref[h]).astype(BF16)
        st_ref[h] = cdec_ref[h] * st + lax.dot_general(kd, v, (((0,), (0,)), ((), ())),
                                                       preferred_element_type=F32)
        mu = jnp.mean(o, axis=-1, keepdims=True)
        oc = o - mu
        var = jnp.mean(oc * oc, axis=-1, keepdims=True)
        o = oc * lax.rsqrt(var + EPS)
        o_ref[:, sl] = (o * ng_ref[:, sl] * _silu(zg_ref[:, sl])).astype(BF16)


def retention(z, col0, norm_g):
    s = z.shape[0]
    t_len = min(RET_T, s)
    hd = W_MIX // C_HEADS
    pos = jnp.arange(s, dtype=F32)
    inv_freq = ROPE_THETA ** (-jnp.arange(0, hd, 2, dtype=F32) / hd)
    ang = pos[:, None] * inv_freq[None, :]
    cos = jnp.cos(ang)
    sin = jnp.sin(ang)
    log_gamma = jnp.log(1.0 - 2.0 ** (-5.0 - jnp.arange(C_HEADS, dtype=F32)))
    idx = jnp.arange(t_len, dtype=F32)
    rel = idx[:, None] - idx[None, :]
    dmat = jnp.where(rel[None] >= 0, jnp.exp(jnp.maximum(rel, 0.0)[None] * log_gamma[:, None, None]), 0.0)
    qdec = jnp.exp((idx + 1.0)[None, :] * log_gamma[:, None])[..., None]
    kdec = jnp.exp((t_len - 1.0 - idx)[None, :] * log_gamma[:, None])[..., None]
    cdec = jnp.broadcast_to(jnp.exp(t_len * log_gamma)[:, None, None], (C_HEADS, 1, hd))

    cb = col0 // W_MIX

    def zspec(k):
        return pl.BlockSpec((t_len, W_MIX), lambda i: (i, cb + k))

    tab = pl.BlockSpec((t_len, hd // 2), lambda i: (i, 0))
    full = lambda shape: pl.BlockSpec(shape, lambda i: (0,) * len(shape))
    return pl.pallas_call(
        _retention_kernel,
        out_shape=jax.ShapeDtypeStruct((s, W_MIX), BF16),
        grid=(s // t_len,),
        in_specs=[zspec(0), zspec(1), zspec(2), zspec(3), tab, tab,
                  full((C_HEADS, t_len, t_len)), full((C_HEADS, t_len, 1)), full((C_HEADS, t_len, 1)),
                  full((C_HEADS, 1, hd)), full((1, W_MIX))],
        out_specs=pl.BlockSpec((t_len, W_MIX), lambda i: (i, 0)),
        scratch_shapes=[pltpu.VMEM((C_HEADS, hd, hd), F32)],
        compiler_params=_cparams("arbitrary"),
        name="retention",
    )(z, z, z, z, cos, sin, dmat, qdec, kdec, cdec, norm_g.reshape(1, W_MIX))


def _rglru_kernel(zg_ref, zx_ref, cw_ref, cb_ref, wa_ref, ba_ref, wx_ref, bx_ref, sp_ref, o_ref,
                  xbuf, h_ref, a_s, u_s):
    t_len = zx_ref.shape[0]

    @pl.when(pl.program_id(0) == 0)
    def _():
        xbuf[0:SUBLANES, :] = jnp.zeros((SUBLANES, W_MIX), F32)
        h_ref[...] = jnp.zeros_like(h_ref)

    xbuf[SUBLANES:, :] = zx_ref[...]
    xc = cb_ref[...]
    for tap in range(CONV_WIDTH):
        off = SUBLANES - (CONV_WIDTH - 1) + tap
        xc = xc + xbuf[off:off + t_len, :] * cw_ref[tap:tap + 1, :]
    xbuf[0:SUBLANES, :] = xbuf[t_len:t_len + SUBLANES, :]
    xcb = xc.astype(BF16)
    n_blk = W_MIX // D_BLOCK
    pre_r = jnp.concatenate(
        [jnp.dot(xcb[:, b * D_BLOCK:(b + 1) * D_BLOCK], wa_ref[b], preferred_element_type=F32)
         for b in range(n_blk)], axis=1)
    pre_i = jnp.concatenate(
        [jnp.dot(xcb[:, b * D_BLOCK:(b + 1) * D_BLOCK], wx_ref[b], preferred_element_type=F32)
         for b in range(n_blk)], axis=1)
    r = _sigmoid(pre_r + ba_ref[...])
    gi = _sigmoid(pre_i + bx_ref[...])
    log_a = -RG_C * r * sp_ref[...]
    a = jnp.exp(log_a)
    a_s[...] = a
    u_s[...] = jnp.sqrt(1.0 - a * a) * (gi * xc)
    row = lax.broadcasted_iota(jnp.int32, (SUBLANES, W_MIX), 0)

    def body(b, h):
        r0 = pl.multiple_of(b * SUBLANES, SUBLANES)
        aa = a_s[pl.ds(r0, SUBLANES), :]
        uu = u_s[pl.ds(r0, SUBLANES), :]
        for k in (1, 2, 4):
            us = jnp.where(row >= k, pltpu.roll(uu, k, 0), 0.0)
            as_ = jnp.where(row >= k, pltpu.roll(aa, k, 0), 1.0)
            uu = uu + aa * us
            aa = aa * as_
        hh = uu + aa * h
        u_s[pl.ds(r0, SUBLANES), :] = hh
        return jnp.broadcast_to(hh[SUBLANES - 1:SUBLANES, :], hh.shape)

    h_ref[...] = lax.fori_loop(0, t_len // SUBLANES, body, h_ref[...])
    o_ref[...] = (_gelu_tanh(zg_ref[...]) * u_s[...]).astype(BF16)


def rglru(z, col0, conv_w, conv_b, w_a, b_a, w_x, b_x, lam):
    s = z.shape[0]
    t_len = min(RG_T, s)
    cb = col0 // W_MIX
    sp = jax.nn.softplus(-lam.astype(F32)).reshape(1, W_MIX)
    row = lambda a: a.astype(F32).reshape(1, W_MIX)
    full = lambda shape: pl.BlockSpec(shape, lambda i: (0,) * len(shape))
    n_blk = W_MIX // D_BLOCK
    return pl.pallas_call(
        _rglru_kernel,
        out_shape=jax.ShapeDtypeStruct((s, W_MIX), BF16),
        grid=(s // t_len,),
        in_specs=[pl.BlockSpec((t_len, W_MIX), lambda i: (i, cb)),
                  pl.BlockSpec((t_len, W_MIX), lambda i: (i, cb + 1)),
                  full((CONV_WIDTH, W_MIX)), full((1, W_MIX)),
                  full((n_blk, D_BLOCK, D_BLOCK)), full((1, W_MIX)),
                  full((n_blk, D_BLOCK, D_BLOCK)), full((1, W_MIX)), full((1, W_MIX))],
        out_specs=pl.BlockSpec((t_len, W_MIX), lambda i: (i, 0)),
        scratch_shapes=[pltpu.VMEM((t_len + SUBLANES, W_MIX), F32),
                        pltpu.VMEM((SUBLANES, W_MIX), F32),
                        pltpu.VMEM((t_len, W_MIX), F32),
                        pltpu.VMEM((t_len, W_MIX), F32)],
        compiler_params=_cparams("arbitrary"),
        name="rglru",
    )(z, z, conv_w.astype(F32), row(conv_b), w_a.astype(BF16), row(b_a), w_x.astype(BF16), row(b_x), sp)


def kernel(x, norm_mix_g, norm_ffn_g, final_norm_g, w_in, w_out, hgrn_lb_logits, hgrn_norm_g, s5_lambda_re, s5_lambda_im, s5_log_dt, s5_b_re, s5_b_im, s5_c_re, s5_c_im, s5_d, s5_glu_w, s5_glu_b, ret_norm_g, rg_conv_w, rg_conv_b, rg_w_a, rg_b_a, rg_w_x, rg_b_x, rg_lambda, ffn_w1, ffn_w3, ffn_w2, router_w, moe_w1, moe_w3, moe_w2):
    b_, s_, d_ = x.shape
    depth = w_in.shape[0]
    xs = x.reshape(b_ * s_, d_).astype(F32)
    lb_p = jax.nn.softmax(hgrn_lb_logits.astype(F32), axis=0)
    lb_all = jnp.cumsum(lb_p, axis=0) - lb_p[0]
    col_a, col_b, col_c, col_d = 0, 4 * W_MIX, 5 * W_MIX, 9 * W_MIX
    for layer in range(depth):
        z = norm_matmul(xs, norm_mix_g[layer], w_in[layer].astype(BF16))
        o_a = hgrn2(z, col_a, lb_all[layer], hgrn_norm_g[layer])
        o_b = s5(z, col_b, s5_lambda_re[layer], s5_lambda_im[layer], s5_log_dt[layer],
                 s5_b_re[layer], s5_b_im[layer], s5_c_re[layer], s5_c_im[layer],
                 s5_d[layer], s5_glu_w[layer], s5_glu_b[layer])
        o_c = retention(z, col_c, ret_norm_g[layer])
        o_d = rglru(z, col_d, rg_conv_w[layer], rg_conv_b[layer], rg_w_a[layer], rg_b_a[layer],
                    rg_w_x[layer], rg_b_x[layer], rg_lambda[layer])
        xs = out_proj((o_a, o_b, o_c, o_d), w_out[layer].astype(BF16), xs)
        if layer % 2 == 0:
            m = layer // 2
            xs = ffn(xs, norm_ffn_g[layer], ffn_w1[m], ffn_w3[m], ffn_w2[m])
        else:
            m = layer // 2
            last = layer == depth - 1
            xs = moe(xs, norm_ffn_g[layer], router_w[m], moe_w1[m], moe_w3[m], moe_w2[m],
                     final_norm_g if last else None)
            if last:
                return xs.reshape(b_, s_, d_)
    return final_norm(xs, final_norm_g).reshape(b_, s_, d_)
```

```python
import functools
import math

import jax
import jax.numpy as jnp
from jax import lax
from jax.experimental import pallas as pl
from jax.experimental.pallas import tpu as pltpu

F32 = jnp.float32
BF16 = jnp.bfloat16
EPS = 1e-6
HIGHEST = lax.Precision.HIGHEST

V7X_VMEM_BYTES = 64 * 1024 * 1024
VMEM_LIMIT_BYTES = V7X_VMEM_BYTES - 4 * 1024 * 1024
SUBLANES = 8
LANES = 128

A_HEAD_DIM = 128
S5_GROUP = 16
S5_STATE = 64
C_HEADS = 4
ROPE_THETA = 10000.0
D_BLOCK = 128
CONV_WIDTH = 4
RG_C = 8.0
N_EXPERTS = 8
TOP_K = 2
W_MIX = 1024

ROW_TILE = 512
IN_PROJ_TN = 1024
OUT_PROJ_TN = 1024
FFN_TF = 512
MOE_TM = 512
COMBINE_ROWS = 256
HGRN_T = 256
HGRN_C = 16
RET_T = 256
S5_T = 256
S5_SLAB = 128
S5_PW_ROWS = 4 * SUBLANES
RG_T = 256
CAST_COLS = 1024
CAST_ROW_ALIGN = 16
CAST_INLINE_ROWS = 256
CAST_CHUNK_ROWS = 64


def _cparams(*sem):
    return pltpu.CompilerParams(dimension_semantics=sem, vmem_limit_bytes=VMEM_LIMIT_BYTES)


def _rms(xf, g):
    return xf * lax.rsqrt(jnp.mean(xf * xf, axis=-1, keepdims=True) + EPS) * g


def _sigmoid(x):
    return 0.5 + 0.5 * jnp.tanh(0.5 * x)


def _silu(x):
    h = 0.5 * x
    return h + h * jnp.tanh(h)


def _gelu_tanh(x):
    c = math.sqrt(2.0 / math.pi)
    return 0.5 * x * (1.0 + jnp.tanh(c * (x + 0.044715 * (x * x * x))))


def _cast_views(casts, n_steps, step_index):
    views, specs, shapes = [], [], []
    for w in casts:
        rows = w.size // (n_steps * CAST_COLS)
        assert rows % CAST_ROW_ALIGN == 0 and rows * n_steps * CAST_COLS == w.size, w.shape
        views.append(w.reshape(n_steps * rows, CAST_COLS))
        specs.append(pl.BlockSpec((rows, CAST_COLS), lambda *g: (step_index(*g), 0)))
        shapes.append(jax.ShapeDtypeStruct((n_steps * rows, CAST_COLS), BF16))
    return views, specs, shapes


def _cast_blocks(src_refs, dst_refs):
    for src, dst in zip(src_refs, dst_refs):
        rows = src.shape[0]
        if rows <= CAST_INLINE_ROWS:
            dst[...] = src[...].astype(BF16)
        else:
            def body(c, carry, src=src, dst=dst):
                sl = pl.ds(pl.multiple_of(c * CAST_CHUNK_ROWS, CAST_CHUNK_ROWS), CAST_CHUNK_ROWS)
                dst[sl, :] = src[sl, :].astype(BF16)
                return carry

            assert rows % CAST_CHUNK_ROWS == 0
            lax.fori_loop(0, rows // CAST_CHUNK_ROWS, body, 0)


def _with_casts(body, n_in, n_cast):
    def kern(*refs):
        ins = refs[:n_in]
        cast_in = refs[n_in:n_in + n_cast]
        out = refs[n_in + n_cast]
        cast_out = refs[n_in + n_cast + 1:n_in + 2 * n_cast + 1]
        scratch = refs[n_in + 2 * n_cast + 1:]
        body(*ins, out, *scratch)
        _cast_blocks(cast_in, cast_out)

    return kern


def _mixer_call(body, n_steps, step_index, grid, in_specs, out_spec, out_shape, scratch_shapes, sem, name,
                args, casts):
    views, cspecs, cshapes = _cast_views(casts, n_steps, step_index)
    outs = pl.pallas_call(
        _with_casts(body, len(in_specs), len(casts)),
        out_shape=[out_shape] + cshapes,
        grid=grid,
        in_specs=list(in_specs) + cspecs,
        out_specs=[out_spec] + cspecs,
        scratch_shapes=scratch_shapes,
        compiler_params=_cparams(*sem),
        name=name,
    )(*args, *views)
    return outs[0], tuple(o.reshape(w.shape) for o, w in zip(outs[1:], casts))


def _norm_matmul_kernel(x_ref, g_ref, w_ref, o_ref, h_ref):
    @pl.when(pl.program_id(1) == 0)
    def _():
        h_ref[...] = _rms(x_ref[...], g_ref[...]).astype(BF16)

    o_ref[...] = jnp.dot(h_ref[...], w_ref[...], preferred_element_type=F32)


def norm_matmul(x, g, w):
    s, d = x.shape
    n = w.shape[1]
    tm, tn = min(ROW_TILE, s), IN_PROJ_TN
    return pl.pallas_call(
        _norm_matmul_kernel,
        out_shape=jax.ShapeDtypeStruct((s, n), F32),
        grid=(s // tm, n // tn),
        in_specs=[
            pl.BlockSpec((tm, d), lambda i, j: (i, 0)),
            pl.BlockSpec((1, d), lambda i, j: (0, 0)),
            pl.BlockSpec((d, tn), lambda i, j: (0, j)),
        ],
        out_specs=pl.BlockSpec((tm, tn), lambda i, j: (i, j)),
        scratch_shapes=[pltpu.VMEM((tm, d), BF16)],
        compiler_params=_cparams("parallel", "arbitrary"),
        name="norm_in_proj",
    )(x, g.reshape(1, d), w)


def _out_proj_kernel(oa_ref, ob_ref, oc_ref, od_ref, w_ref, x_ref, o_ref):
    acc = x_ref[...]
    for idx, r in enumerate((oa_ref, ob_ref, oc_ref, od_ref)):
        acc = acc + jnp.dot(r[...], w_ref[idx * W_MIX:(idx + 1) * W_MIX, :], preferred_element_type=F32)
    o_ref[...] = acc


def out_proj(parts, w, x):
    s, d = x.shape
    tm, tn = min(ROW_TILE, s), OUT_PROJ_TN
    part_spec = pl.BlockSpec((tm, W_MIX), lambda i, j: (i, 0))
    return pl.pallas_call(
        _out_proj_kernel,
        out_shape=jax.ShapeDtypeStruct((s, d), F32),
        grid=(s // tm, d // tn),
        in_specs=[part_spec, part_spec, part_spec, part_spec,
                  pl.BlockSpec((4 * W_MIX, tn), lambda i, j: (0, j)),
                  pl.BlockSpec((tm, tn), lambda i, j: (i, j))],
        out_specs=pl.BlockSpec((tm, tn), lambda i, j: (i, j)),
        compiler_params=_cparams("parallel", "arbitrary"),
        name="out_proj",
    )(*parts, w, x)


def _swiglu_step(h, w1, w3, w2):
    a = jnp.dot(h, w1, preferred_element_type=F32)
    b = jnp.dot(h, w3, preferred_element_type=F32)
    act = (_silu(a) * b).astype(BF16)
    return jnp.dot(act, w2, preferred_element_type=F32)


def _ffn_kernel(x_ref, g_ref, w1_ref, w3_ref, w2_ref, o_ref, h_ref):
    @pl.when(pl.program_id(1) == 0)
    def _():
        xf = x_ref[...]
        h_ref[...] = _rms(xf, g_ref[...]).astype(BF16)
        o_ref[...] = xf

    o_ref[...] += _swiglu_step(h_ref[...], w1_ref[...], w3_ref[...], w2_ref[...])


def ffn(x, g, w1, w3, w2):
    s, d = x.shape
    dff = w1.shape[1]
    tm, tf = min(ROW_TILE, s), FFN_TF
    return pl.pallas_call(
        _ffn_kernel,
        out_shape=jax.ShapeDtypeStruct((s, d), F32),
        grid=(s // tm, dff // tf),
        in_specs=[
            pl.BlockSpec((tm, d), lambda i, f: (i, 0), pipeline_mode=pl.Buffered(1)),
            pl.BlockSpec((1, d), lambda i, f: (0, 0)),
            pl.BlockSpec((d, tf), lambda i, f: (0, f)),
            pl.BlockSpec((d, tf), lambda i, f: (0, f)),
            pl.BlockSpec((tf, d), lambda i, f: (f, 0)),
        ],
        out_specs=pl.BlockSpec((tm, d), lambda i, f: (i, 0), pipeline_mode=pl.Buffered(1)),
        scratch_shapes=[pltpu.VMEM((tm, d), BF16)],
        compiler_params=_cparams("parallel", "arbitrary"),
        name="ffn_swiglu",
    )(x, g.reshape(1, d), w1, w3, w2)


def _router_kernel(x_ref, g_ref, wr_ref, comb_ref):
    h = _rms(x_ref[...], g_ref[...])
    logits = jnp.dot(h, wr_ref[...], precision=HIGHEST, preferred_element_type=F32)
    lane = lax.broadcasted_iota(jnp.int32, logits.shape, 1)
    neg = jnp.float32(-jnp.inf)
    logits = jnp.where(lane < N_EXPERTS, logits, neg)
    v1 = jnp.max(logits, axis=-1, keepdims=True)
    i1 = jnp.min(jnp.where(logits == v1, lane, LANES), axis=-1, keepdims=True)
    rest = jnp.where(lane == i1, neg, logits)
    v2 = jnp.max(rest, axis=-1, keepdims=True)
    i2 = jnp.min(jnp.where(rest == v2, lane, LANES), axis=-1, keepdims=True)
    e2 = jnp.exp(v2 - v1)
    g1 = 1.0 / (1.0 + e2)
    g2 = e2 / (1.0 + e2)
    comb_ref[...] = (jnp.where(lane == 0, i1.astype(F32), 0.0) + jnp.where(lane == 1, i2.astype(F32), 0.0)
                     + jnp.where(lane == 2, g1, 0.0) + jnp.where(lane == 3, g2, 0.0))


def router(x, g, router_w):
    s, d = x.shape
    tm = min(ROW_TILE, s)
    wr = jnp.zeros((d, LANES), F32).at[:, :N_EXPERTS].set(router_w.astype(F32))
    return pl.pallas_call(
        _router_kernel,
        out_shape=jax.ShapeDtypeStruct((s, LANES), F32),
        grid=(s // tm,),
        in_specs=[pl.BlockSpec((tm, d), lambda i: (i, 0)),
                  pl.BlockSpec((1, d), lambda i: (0, 0)),
                  pl.BlockSpec((d, LANES), lambda i: (0, 0))],
        out_specs=pl.BlockSpec((tm, LANES), lambda i: (i, 0)),
        compiler_params=_cparams("parallel"),
        name="moe_router",
    )(x, g.reshape(1, d), wr)


def _row_copy(src_hbm, src_row, dst_ref, dst_row, sem):
    return pltpu.make_async_copy(src_hbm.at[pl.ds(src_row, 1)], dst_ref.at[pl.ds(dst_row, 1)], sem)


def _moe_grouped_kernel(te_ref, nu_ref, src_ref, x_hbm, g_ref, gate_ref, w1_ref, w3_ref, w2_ref, o_ref,
                        xbuf, h_ref, sem):
    i = pl.program_id(0)
    f = pl.program_id(1)
    n_used = nu_ref[0]
    used = i < n_used
    tm = xbuf.shape[0]

    def start_gather(tile):
        def issue(r, c):
            _row_copy(x_hbm, src_ref[tile * tm + r], xbuf, r, sem).start()
            return c

        lax.fori_loop(0, tm, issue, 0, unroll=8)

    @pl.when((i == 0) & (f == 0))
    def _():
        start_gather(0)

    @pl.when(jnp.logical_not(used) & (f == 0))
    def _():
        o_ref[...] = jnp.zeros_like(o_ref)

    @pl.when(used & (f == 0))
    def _():
        def wait(r, c):
            _row_copy(x_hbm, 0, xbuf, r, sem).wait()
            return c

        lax.fori_loop(0, tm, wait, 0, unroll=8)
        h_ref[...] = _rms(xbuf[...], g_ref[...]).astype(BF16)
        o_ref[...] = _swiglu_step(h_ref[...], w1_ref[0], w3_ref[0], w2_ref[0])

    @pl.when((f == 1) & (i + 1 < n_used))
    def _():
        start_gather(i + 1)

    @pl.when(used & (f > 0))
    def _():
        o_ref[...] += _swiglu_step(h_ref[...], w1_ref[0], w3_ref[0], w2_ref[0])

    @pl.when(used & (f == pl.num_programs(1) - 1))
    def _():
        o_ref[...] = o_ref[...] * gate_ref[...]


def moe_grouped(x, src, g, gate_sorted, tile_expert, n_used, w1, w3, w2):
    d = x.shape[1]
    p = src.shape[0]
    dff = w1.shape[2]
    tm, tf = MOE_TM, min(FFN_TF, dff // 2)
    nf = dff // tf

    def row_map(i, f, te, nu, sr):
        return (jnp.minimum(i, nu[0] - 1), 0)

    def f_eff(i, f, nu):
        return jnp.where(i < nu[0], f, nf - 1)

    grid_spec = pltpu.PrefetchScalarGridSpec(
        num_scalar_prefetch=3,
        grid=(p // tm, nf),
        in_specs=[
            pl.BlockSpec(memory_space=pl.ANY),
            pl.BlockSpec((1, d), lambda i, f, te, nu, sr: (0, 0)),
            pl.BlockSpec((tm, 1), row_map),
            pl.BlockSpec((1, d, tf), lambda i, f, te, nu, sr: (te[i], 0, f_eff(i, f, nu))),
            pl.BlockSpec((1, d, tf), lambda i, f, te, nu, sr: (te[i], 0, f_eff(i, f, nu))),
            pl.BlockSpec((1, tf, d), lambda i, f, te, nu, sr: (te[i], f_eff(i, f, nu), 0)),
        ],
        out_specs=pl.BlockSpec((tm, d), lambda i, f, te, nu, sr: (i, 0), pipeline_mode=pl.Buffered(1)),
        scratch_shapes=[pltpu.VMEM((tm, d), F32), pltpu.VMEM((tm, d), BF16), pltpu.SemaphoreType.DMA(())],
    )
    return pl.pallas_call(
        _moe_grouped_kernel,
        out_shape=jax.ShapeDtypeStruct((p, d), F32),
        grid_spec=grid_spec,
        compiler_params=_cparams("arbitrary", "arbitrary"),
        name="moe_grouped",
    )(tile_expert, n_used, src, x, g.reshape(1, d), gate_sorted.reshape(p, 1), w1, w3, w2)


def _combine_kernel(dest_ref, x_ref, y_hbm, *rest, final_norm):
    if final_norm:
        fg_ref, o_ref, buf, sem = rest
    else:
        o_ref, buf, sem = rest
    i = pl.program_id(0)
    n_rows = o_ref.shape[0]
    slot = i % 2

    def start_gather(step, sl):
        def issue(r, c):
            for k in range(TOP_K):
                _row_copy(y_hbm, dest_ref[TOP_K * (step * n_rows + r) + k], buf.at[sl, k], r, sem.at[sl]).start()
            return c

        lax.fori_loop(0, n_rows, issue, 0, unroll=4)

    @pl.when(i == 0)
    def _():
        start_gather(0, 0)

    @pl.when(i + 1 < pl.num_programs(0))
    def _():
        start_gather(i + 1, 1 - slot)

    def wait(r, c):
        for k in range(TOP_K):
            _row_copy(y_hbm, 0, buf.at[slot, k], r, sem.at[slot]).wait()
        return c

    lax.fori_loop(0, n_rows, wait, 0, unroll=4)
    acc = x_ref[...]
    for k in range(TOP_K):
        acc = acc + buf[slot, k]
    o_ref[...] = _rms(acc, fg_ref[...]) if final_norm else acc


def combine(x, y, dest, final_g=None):
    s, d = x.shape
    rt = COMBINE_ROWS
    final_norm = final_g is not None
    in_specs = [pl.BlockSpec((rt, d), lambda i, dr: (i, 0)), pl.BlockSpec(memory_space=pl.ANY)]
    args = [x, y]
    if final_norm:
        in_specs.append(pl.BlockSpec((1, d), lambda i, dr: (0, 0)))
        args.append(final_g.astype(F32).reshape(1, d))
    grid_spec = pltpu.PrefetchScalarGridSpec(
        num_scalar_prefetch=1,
        grid=(s // rt,),
        in_specs=in_specs,
        out_specs=pl.BlockSpec((rt, d), lambda i, dr: (i, 0)),
        scratch_shapes=[pltpu.VMEM((2, TOP_K, rt, d), F32), pltpu.SemaphoreType.DMA((2,))],
    )
    return pl.pallas_call(
        functools.partial(_combine_kernel, final_norm=final_norm),
        out_shape=jax.ShapeDtypeStruct((s, d), F32),
        grid_spec=grid_spec,
        compiler_params=_cparams("arbitrary"),
        name="moe_combine",
    )(dest, *args)


def _routing_tables(experts, gates, tm, n_tiles):
    n_assign = experts.size
    e_flat = experts.reshape(n_assign)
    onehot = (e_flat[:, None] == jnp.arange(N_EXPERTS, dtype=jnp.int32)[None, :]).astype(jnp.int32)
    csum = jnp.cumsum(onehot, axis=0)
    rank = jnp.sum(csum * onehot, axis=1) - 1
    counts = csum[-1]
    padded = ((counts + tm - 1) // tm) * tm
    seg_end = jnp.cumsum(padded)
    seg_start = seg_end - padded
    dest = (seg_start[e_flat] + rank).astype(jnp.int32)
    p = n_tiles * tm
    src = jnp.zeros((p,), jnp.int32).at[dest].set(jnp.arange(n_assign, dtype=jnp.int32) // TOP_K)
    gate_sorted = jnp.zeros((p,), F32).at[dest].set(gates.reshape(n_assign))
    n_used = (seg_end[-1] // tm).astype(jnp.int32)
    tile_start = jnp.arange(n_tiles, dtype=jnp.int32) * tm
    tile_e = jnp.sum((tile_start[:, None] >= seg_end[None, :]).astype(jnp.int32), axis=1)
    tile_e = jnp.minimum(tile_e, N_EXPERTS - 1)
    last_e = tile_e[n_used - 1]
    tile_e = jnp.where(jnp.arange(n_tiles) < n_used, tile_e, last_e).astype(jnp.int32)
    return src, dest, gate_sorted, tile_e, n_used.reshape(1)


def moe(x, g, router_w, w1, w3, w2, final_g=None):
    s, _ = x.shape
    info = router(x, g, router_w)
    experts = info[:, 0:TOP_K].astype(jnp.int32)
    gates = info[:, TOP_K:2 * TOP_K]
    tm = MOE_TM
    n_tiles = (TOP_K * s) // tm + N_EXPERTS
    src, dest, gate_sorted, tile_e, n_used = _routing_tables(experts, gates, tm, n_tiles)
    y = moe_grouped(x, src, g, gate_sorted, tile_e, n_used, w1.astype(BF16), w3.astype(BF16), w2.astype(BF16))
    return combine(x, y, dest, final_g)


def _final_norm_kernel(x_ref, g_ref, o_ref):
    o_ref[...] = _rms(x_ref[...], g_ref[...])


def final_norm(x, g):
    s, d = x.shape
    tm = min(ROW_TILE, s)
    return pl.pallas_call(
        _final_norm_kernel,
        out_shape=jax.ShapeDtypeStruct((s, d), F32),
        grid=(s // tm,),
        in_specs=[pl.BlockSpec((tm, d), lambda i: (i, 0)), pl.BlockSpec((1, d), lambda i: (0, 0))],
        out_specs=pl.BlockSpec((tm, d), lambda i: (i, 0)),
        compiler_params=_cparams("parallel"),
        name="final_norm",
    )(x, g.reshape(1, d))


def _hgrn2_kernel(zq_ref, zf_ref, zi_ref, zg_ref, lb_ref, ng_ref, tri_ref, o_ref,
                  st_ref, q_s, k_s, v_s, cum_s, o_s, dec_s, qd_s, kd_s):
    c_len = HGRN_C
    heads, t_len, hd = q_s.shape

    @pl.when(pl.program_id(0) == 0)
    def _():
        st_ref[...] = jnp.zeros_like(st_ref)

    lb = lb_ref[...]
    f = lb + (1.0 - lb) * _sigmoid(zf_ref[...])
    lf = jnp.log(f)
    lf_hi = lf.astype(BF16)
    lf_lo = (lf - lf_hi.astype(F32)).astype(BF16)
    tri = tri_ref[...]
    cum = (jnp.dot(tri, lf_hi, preferred_element_type=F32) + jnp.dot(tri, lf_lo, preferred_element_type=F32))
    cum3 = cum.reshape(t_len // c_len, c_len, heads * hd)
    last3 = cum3[:, c_len - 1:c_len, :]
    last = jnp.broadcast_to(last3, cum3.shape).reshape(t_len, heads * hd)
    dec = jnp.exp(last3.reshape(t_len // c_len, heads * hd))
    q = _silu(zq_ref[...])
    k = 1.0 - f
    qd = (q * jnp.exp(cum)).astype(BF16)
    kd = (k * jnp.exp(last - cum)).astype(BF16)
    v = zi_ref[...]
    cum2 = cum * math.log2(math.e)
    for h in range(heads):
        sl = slice(h * hd, (h + 1) * hd)
        q_s[h] = q[:, sl]
        k_s[h] = k[:, sl]
        v_s[h] = v[:, sl]
        cum_s[h] = cum2[:, sl]
        dec_s[h] = dec[:, sl]
        qd_s[h] = qd[:, sl]
        kd_s[h] = kd[:, sl]

    half = c_len // 2
    row = lax.broadcasted_iota(jnp.int32, (half, hd), 0)
    lane = lax.broadcasted_iota(jnp.int32, (half, hd), 1)

    def body(c, carry):
        r0 = pl.multiple_of(c * c_len, c_len)
        rows = pl.ds(r0, c_len)
        for h in range(heads):
            q_c = q_s[h, rows, :]
            k_c = k_s[h, rows, :]
            cm = cum_s[h, rows, :]
            m_lo = jnp.zeros((half, hd), F32)
            m_hi = jnp.zeros((half, hd), F32)
            for s in range(c_len):
                ks = k_c[s:s + 1, :]
                cs = cm[s:s + 1, :]
                if s < half:
                    w = q_c[:half] * (ks * jnp.exp2(jnp.minimum(cm[:half] - cs, 0.0)))
                    m_lo = jnp.where(lane == s, jnp.sum(w, axis=1, keepdims=True), m_lo)
                w = q_c[half:] * (ks * jnp.exp2(jnp.minimum(cm[half:] - cs, 0.0)))
                m_hi = jnp.where(lane == s, jnp.sum(w, axis=1, keepdims=True), m_hi)
            m_lo = jnp.where(row >= lane, m_lo, 0.0)
            m_hi = jnp.where(row + half >= lane, m_hi, 0.0)
            scores = jnp.concatenate([m_lo, m_hi], axis=0)[:, :c_len].astype(BF16)
            v_c = v_s[h, rows, :].astype(BF16)
            st = st_ref[h]
            o = (jnp.dot(scores, v_c, preferred_element_type=F32)
                 + lax.dot_general(qd_s[h, rows, :], st.astype(BF16), (((1,), (1,)), ((), ())),
                                   preferred_element_type=F32))
            o_s[h, rows, :] = o
            upd = lax.dot_general(v_c, kd_s[h, rows, :], (((0,), (0,)), ((), ())),
                                  preferred_element_type=F32)
            st_ref[h] = st * dec_s[h, pl.ds(c, 1), :] + upd
        return carry

    lax.fori_loop(0, t_len // c_len, body, 0)
    ng = ng_ref[...]
    sg = _silu(zg_ref[...])
    for h in range(heads):
        sl = slice(h * hd, (h + 1) * hd)
        o = o_s[h]
        o = o * lax.rsqrt(jnp.mean(o * o, axis=-1, keepdims=True) + EPS)
        o_ref[:, sl] = (o * ng[:, sl] * sg[:, sl]).astype(BF16)


def hgrn2(z, col0, lb, norm_g, casts=()):
    s = z.shape[0]
    t_len = min(HGRN_T, s)
    heads = W_MIX // A_HEAD_DIM
    cb = col0 // W_MIX
    r = jnp.arange(t_len)
    same = r[:, None] // HGRN_C == r[None, :] // HGRN_C
    tri = (same & (r[None, :] <= r[:, None])).astype(BF16)

    def zspec(k):
        return pl.BlockSpec((t_len, W_MIX), lambda i: (i, cb + k))

    vec = pl.BlockSpec((1, W_MIX), lambda i: (0, 0))
    sq = pl.BlockSpec((t_len, t_len), lambda i: (0, 0))
    per_head = lambda dt: pltpu.VMEM((heads, t_len, A_HEAD_DIM), dt)
    return _mixer_call(
        _hgrn2_kernel, s // t_len, lambda i: i,
        grid=(s // t_len,),
        in_specs=[zspec(0), zspec(1), zspec(2), zspec(3), vec, vec, sq],
        out_spec=pl.BlockSpec((t_len, W_MIX), lambda i: (i, 0)),
        out_shape=jax.ShapeDtypeStruct((s, W_MIX), BF16),
        scratch_shapes=[pltpu.VMEM((heads, A_HEAD_DIM, A_HEAD_DIM), F32)]
        + [per_head(F32)] * 5 + [pltpu.VMEM((heads, t_len // HGRN_C, A_HEAD_DIM), F32)] + [per_head(BF16)] * 2,
        sem=("arbitrary",), name="hgrn2",
        args=(z, z, z, z, lb.reshape(1, W_MIX), norm_g.reshape(1, W_MIX), tri), casts=casts)


def _s5_kernel(u_ref, wb_ref, pw_ref, wc_ref, d_ref, gw_ref, gb_ref, o_ref,
               carry_ref, bu_s, y_s):
    i = pl.program_id(0)
    j = pl.program_id(1)
    n_slab = pl.num_programs(1)
    t_len = bu_s.shape[0]
    half = bu_s.shape[1] // 2

    @pl.when(i == 0)
    def _():
        carry_ref[j] = jnp.zeros((SUBLANES, 2 * half), F32)

    u = u_ref[...]
    bu_s[...] = jnp.dot(u.astype(BF16), wb_ref[0], preferred_element_type=F32)
    p8_re, p8_im = pw_ref[0, 0:8, :half], pw_ref[0, 0:8, half:]

    def body(b, carry):
        c_re, c_im = carry
        r0 = pl.multiple_of(b * SUBLANES, SUBLANES)
        blk = bu_s[pl.ds(r0, SUBLANES), :]
        x_re, x_im = blk[:, :half], blk[:, half:]
        for step, k in enumerate((1, 2, 4)):
            a_re = pw_ref[0, 8 * (step + 1):8 * (step + 2), :half]
            a_im = pw_ref[0, 8 * (step + 1):8 * (step + 2), half:]
            s_re = pltpu.roll(x_re, k, 0)
            s_im = pltpu.roll(x_im, k, 0)
            x_re, x_im = (x_re + a_re * s_re - a_im * s_im,
                          x_im + a_re * s_im + a_im * s_re)
        x_re, x_im = (x_re + p8_re * c_re - p8_im * c_im,
                      x_im + p8_re * c_im + p8_im * c_re)
        bu_s[pl.ds(r0, SUBLANES), :] = jnp.concatenate([x_re, x_im], axis=1)
        n_re = jnp.broadcast_to(x_re[SUBLANES - 1:SUBLANES, :], x_re.shape)
        n_im = jnp.broadcast_to(x_im[SUBLANES - 1:SUBLANES, :], x_im.shape)
        return n_re, n_im

    c0 = carry_ref[j]
    c_re, c_im = lax.fori_loop(0, t_len // SUBLANES, body, (c0[:, :half], c0[:, half:]))
    carry_ref[j] = jnp.concatenate([c_re, c_im], axis=1)

    y = jnp.dot(bu_s[...].astype(BF16), wc_ref[0], preferred_element_type=F32)
    y = _gelu_tanh(y + d_ref[0] * u)
    y_s[j] = y

    @pl.when(j == n_slab - 1)
    def _():
        yf = jnp.concatenate([y_s[s] for s in range(y_s.shape[0])], axis=1)
        gate = jnp.dot(yf.astype(BF16), gw_ref[...], preferred_element_type=F32) + gb_ref[...]
        o_ref[...] = (yf * _sigmoid(gate)).astype(BF16)


def s5(z, col0, lam_re, lam_im, log_dt, b_re, b_im, c_re, c_im, d_skip, glu_w, glu_b, casts=()):
    s = z.shape[0]
    t_len = min(S5_T, s)
    groups, n_state = lam_re.shape
    gps = S5_SLAB // S5_GROUP
    n_slab = groups // gps
    half = gps * n_state
    cb = col0 // S5_SLAB
    lam_re = lam_re.astype(F32)
    lam_im = lam_im.astype(F32)
    dt = jnp.exp(log_dt.astype(F32))[:, None]
    mag = jnp.exp(lam_re * dt)
    ang = lam_im * dt
    ab_re = mag * jnp.cos(ang)
    ab_im = mag * jnp.sin(ang)
    den = lam_re * lam_re + lam_im * lam_im
    num_re = ab_re - 1.0
    coef_re = (num_re * lam_re + ab_im * lam_im) / den
    coef_im = (ab_im * lam_re - num_re * lam_im) / den
    br = b_re.astype(F32)
    bi = b_im.astype(F32)
    bb_re = coef_re[..., None] * br - coef_im[..., None] * bi
    bb_im = coef_re[..., None] * bi + coef_im[..., None] * br
    eye = jnp.eye(gps, dtype=F32)

    def blockdiag_in(bb):
        t = bb.reshape(n_slab, gps, n_state, S5_GROUP)
        return jnp.einsum('sgnp,gh->sgphn', t, eye).reshape(n_slab, gps * S5_GROUP, gps * n_state)

    wb = jnp.concatenate([blockdiag_in(bb_re), blockdiag_in(bb_im)], axis=-1).astype(BF16)

    def blockdiag_out(cc):
        t = cc.reshape(n_slab, gps, S5_GROUP, n_state)
        return jnp.einsum('sgpn,gh->sgnhp', t, eye).reshape(n_slab, gps * n_state, gps * S5_GROUP)

    wc = jnp.concatenate([blockdiag_out(c_re.astype(F32)), -blockdiag_out(c_im.astype(F32))],
                         axis=1).astype(BF16)
    r8 = jnp.arange(SUBLANES)
    expo = jnp.concatenate([r8 + 1.0] + [jnp.full((SUBLANES,), float(k)) for k in (1, 2, 4)]).astype(F32)
    keep = jnp.concatenate([jnp.ones((SUBLANES,), F32)] + [(r8 >= k).astype(F32) for k in (1, 2, 4)])
    expo = expo[:, None, None]
    p_mag = jnp.exp(expo * (lam_re * dt)[None]) * keep[:, None, None]
    p_re = (p_mag * jnp.cos(expo * ang[None])).reshape(S5_PW_ROWS, n_slab, half)
    p_im = (p_mag * jnp.sin(expo * ang[None])).reshape(S5_PW_ROWS, n_slab, half)
    pw = jnp.moveaxis(jnp.concatenate([p_re, p_im], axis=-1), 1, 0)

    return _mixer_call(
        _s5_kernel, (s // t_len) * n_slab, lambda i, j: i * n_slab + j,
        grid=(s // t_len, n_slab),
        in_specs=[
            pl.BlockSpec((t_len, S5_SLAB), lambda i, j: (i, cb + j)),
            pl.BlockSpec((1, S5_SLAB, 2 * half), lambda i, j: (j, 0, 0)),
            pl.BlockSpec((1, S5_PW_ROWS, 2 * half), lambda i, j: (j, 0, 0)),
            pl.BlockSpec((1, 2 * half, S5_SLAB), lambda i, j: (j, 0, 0)),
            pl.BlockSpec((1, 1, S5_SLAB), lambda i, j: (j, 0, 0)),
            pl.BlockSpec((W_MIX, W_MIX), lambda i, j: (0, 0)),
            pl.BlockSpec((1, W_MIX), lambda i, j: (0, 0)),
        ],
        out_spec=pl.BlockSpec((t_len, W_MIX), lambda i, j: (i, 0)),
        out_shape=jax.ShapeDtypeStruct((s, W_MIX), BF16),
        scratch_shapes=[pltpu.VMEM((n_slab, SUBLANES, 2 * half), F32),
                        pltpu.VMEM((t_len, 2 * half), F32),
                        pltpu.VMEM((n_slab, t_len, S5_SLAB), F32)],
        sem=("arbitrary", "arbitrary"), name="s5",
        args=(z, wb, pw, wc, d_skip.astype(F32).reshape(n_slab, 1, S5_SLAB), glu_w.astype(BF16),
              glu_b.astype(F32).reshape(1, W_MIX)), casts=casts)


def _retention_kernel(zq_ref, zk_ref, zv_ref, zg_ref, cos_ref, sin_ref, dmat_ref, qdec_ref, kdec_ref,
                      cdec_ref, ng_ref, o_ref, st_ref):
    @pl.when(pl.program_id(0) == 0)
    def _():
        st_ref[...] = jnp.zeros_like(st_ref)

    heads, hd, _ = st_ref.shape
    half = hd // 2
    cos = cos_ref[...]
    sin = sin_ref[...]

    def rope(t):
        t1, t2 = t[:, :half], t[:, half:]
        return jnp.concatenate([t1 * cos - t2 * sin, t1 * sin + t2 * cos], axis=1)

    for h in range(heads):
        sl = slice(h * hd, (h + 1) * hd)
        q = rope(zq_ref[:, sl])
        k = rope(zk_ref[:, sl]) * hd ** -0.5
        v = zv_ref[:, sl].astype(BF16)
        qb = q.astype(BF16)
        scores = lax.dot_general(qb, k.astype(BF16), (((1,), (1,)), ((), ())),
                                 preferred_element_type=F32) * dmat_ref[h]
        st = st_ref[h]
        o = (jnp.dot(scores.astype(BF16), v, preferred_element_type=F32)
             + jnp.dot(qb, st.astype(BF16), preferred_element_type=F32) * qdec_ref[h])
        kd = (k * kdec_ref[h]).astype(BF16)
        st_ref[h] = cdec_ref[h] * st + lax.dot_general(kd, v, (((0,), (0,)), ((), ())),
                                                       preferred_element_type=F32)
        mu = jnp.mean(o, axis=-1, keepdims=True)
        oc = o - mu
        var = jnp.mean(oc * oc, axis=-1, keepdims=True)
        o = oc * lax.rsqrt(var + EPS)
        o_ref[:, sl] = (o * ng_ref[:, sl] * _silu(zg_ref[:, sl])).astype(BF16)


def retention(z, col0, norm_g, casts=()):
    s = z.shape[0]
    t_len = min(RET_T, s)
    hd = W_MIX // C_HEADS
    pos = jnp.arange(s, dtype=F32)
    inv_freq = ROPE_THETA ** (-jnp.arange(0, hd, 2, dtype=F32) / hd)
    ang = pos[:, None] * inv_freq[None, :]
    cos = jnp.cos(ang)
    sin = jnp.sin(ang)
    log_gamma = jnp.log(1.0 - 2.0 ** (-5.0 - jnp.arange(C_HEADS, dtype=F32)))
    idx = jnp.arange(t_len, dtype=F32)
    rel = idx[:, None] - idx[None, :]
    dmat = jnp.where(rel[None] >= 0, jnp.exp(jnp.maximum(rel, 0.0)[None] * log_gamma[:, None, None]), 0.0)
    qdec = jnp.exp((idx + 1.0)[None, :] * log_gamma[:, None])[..., None]
    kdec = jnp.exp((t_len - 1.0 - idx)[None, :] * log_gamma[:, None])[..., None]
    cdec = jnp.broadcast_to(jnp.exp(t_len * log_gamma)[:, None, None], (C_HEADS, 1, hd))

    cb = col0 // W_MIX

    def zspec(k):
        return pl.BlockSpec((t_len, W_MIX), lambda i: (i, cb + k))

    tab = pl.BlockSpec((t_len, hd // 2), lambda i: (i, 0))
    full = lambda shape: pl.BlockSpec(shape, lambda i: (0,) * len(shape))
    return _mixer_call(
        _retention_kernel, s // t_len, lambda i: i,
        grid=(s // t_len,),
        in_specs=[zspec(0), zspec(1), zspec(2), zspec(3), tab, tab,
                  full((C_HEADS, t_len, t_len)), full((C_HEADS, t_len, 1)), full((C_HEADS, t_len, 1)),
                  full((C_HEADS, 1, hd)), full((1, W_MIX))],
        out_spec=pl.BlockSpec((t_len, W_MIX), lambda i: (i, 0)),
        out_shape=jax.ShapeDtypeStruct((s, W_MIX), BF16),
        scratch_shapes=[pltpu.VMEM((C_HEADS, hd, hd), F32)],
        sem=("arbitrary",), name="retention",
        args=(z, z, z, z, cos, sin, dmat, qdec, kdec, cdec, norm_g.reshape(1, W_MIX)), casts=casts)


def _rglru_kernel(zg_ref, zx_ref, cw_ref, cb_ref, wa_ref, ba_ref, wx_ref, bx_ref, sp_ref, o_ref,
                  xbuf, h_ref, a_s, u_s):
    t_len = zx_ref.shape[0]

    @pl.when(pl.program_id(0) == 0)
    def _():
        xbuf[0:SUBLANES, :] = jnp.zeros((SUBLANES, W_MIX), F32)
        h_ref[...] = jnp.zeros_like(h_ref)

    xbuf[SUBLANES:, :] = zx_ref[...]
    xc = cb_ref[...]
    for tap in range(CONV_WIDTH):
        off = SUBLANES - (CONV_WIDTH - 1) + tap
        xc = xc + xbuf[off:off + t_len, :] * cw_ref[tap:tap + 1, :]
    xbuf[0:SUBLANES, :] = xbuf[t_len:t_len + SUBLANES, :]
    xcb = xc.astype(BF16)
    n_blk = W_MIX // D_BLOCK
    pre_r = jnp.concatenate(
        [jnp.dot(xcb[:, b * D_BLOCK:(b + 1) * D_BLOCK], wa_ref[b], preferred_element_type=F32)
         for b in range(n_blk)], axis=1)
    pre_i = jnp.concatenate(
        [jnp.dot(xcb[:, b * D_BLOCK:(b + 1) * D_BLOCK], wx_ref[b], preferred_element_type=F32)
         for b in range(n_blk)], axis=1)
    r = _sigmoid(pre_r + ba_ref[...])
    gi = _sigmoid(pre_i + bx_ref[...])
    log_a = -RG_C * r * sp_ref[...]
    a = jnp.exp(log_a)
    a_s[...] = a
    u_s[...] = jnp.sqrt(1.0 - a * a) * (gi * xc)
    row = lax.broadcasted_iota(jnp.int32, (SUBLANES, W_MIX), 0)

    def body(b, h):
        r0 = pl.multiple_of(b * SUBLANES, SUBLANES)
        aa = a_s[pl.ds(r0, SUBLANES), :]
        uu = u_s[pl.ds(r0, SUBLANES), :]
        for k in (1, 2, 4):
            us = jnp.where(row >= k, pltpu.roll(uu, k, 0), 0.0)
            as_ = jnp.where(row >= k, pltpu.roll(aa, k, 0), 1.0)
            uu = uu + aa * us
            aa = aa * as_
        hh = uu + aa * h
        u_s[pl.ds(r0, SUBLANES), :] = hh
        return jnp.broadcast_to(hh[SUBLANES - 1:SUBLANES, :], hh.shape)

    h_ref[...] = lax.fori_loop(0, t_len // SUBLANES, body, h_ref[...])
    o_ref[...] = (_gelu_tanh(zg_ref[...]) * u_s[...]).astype(BF16)


def rglru(z, col0, conv_w, conv_b, w_a, b_a, w_x, b_x, lam, casts=()):
    s = z.shape[0]
    t_len = min(RG_T, s)
    cb = col0 // W_MIX
    sp = jax.nn.softplus(-lam.astype(F32)).reshape(1, W_MIX)
    row = lambda a: a.astype(F32).reshape(1, W_MIX)
    full = lambda shape: pl.BlockSpec(shape, lambda i: (0,) * len(shape))
    n_blk = W_MIX // D_BLOCK
    return _mixer_call(
        _rglru_kernel, s // t_len, lambda i: i,
        grid=(s // t_len,),
        in_specs=[pl.BlockSpec((t_len, W_MIX), lambda i: (i, cb)),
                  pl.BlockSpec((t_len, W_MIX), lambda i: (i, cb + 1)),
                  full((CONV_WIDTH, W_MIX)), full((1, W_MIX)),
                  full((n_blk, D_BLOCK, D_BLOCK)), full((1, W_MIX)),
                  full((n_blk, D_BLOCK, D_BLOCK)), full((1, W_MIX)), full((1, W_MIX))],
        out_spec=pl.BlockSpec((t_len, W_MIX), lambda i: (i, 0)),
        out_shape=jax.ShapeDtypeStruct((s, W_MIX), BF16),
        scratch_shapes=[pltpu.VMEM((t_len + SUBLANES, W_MIX), F32),
                        pltpu.VMEM((SUBLANES, W_MIX), F32),
                        pltpu.VMEM((t_len, W_MIX), F32),
                        pltpu.VMEM((t_len, W_MIX), F32)],
        sem=("arbitrary",), name="rglru",
        args=(z, z, conv_w.astype(F32), row(conv_b), w_a.astype(BF16), row(b_a), w_x.astype(BF16), row(b_x), sp),
        casts=casts)


def kernel(x, norm_mix_g, norm_ffn_g, final_norm_g, w_in, w_out, hgrn_lb_logits, hgrn_norm_g, s5_lambda_re, s5_lambda_im, s5_log_dt, s5_b_re, s5_b_im, s5_c_re, s5_c_im, s5_d, s5_glu_w, s5_glu_b, ret_norm_g, rg_conv_w, rg_conv_b, rg_w_a, rg_b_a, rg_w_x, rg_b_x, rg_lambda, ffn_w1, ffn_w3, ffn_w2, router_w, moe_w1, moe_w3, moe_w2):
    b_, s_, d_ = x.shape
    depth = w_in.shape[0]
    xs = x.reshape(b_ * s_, d_).astype(F32)
    lb_p = jax.nn.softmax(hgrn_lb_logits.astype(F32), axis=0)
    lb_all = jnp.cumsum(lb_p, axis=0) - lb_p[0]
    col_a, col_b, col_c, col_d = 0, 4 * W_MIX, 5 * W_MIX, 9 * W_MIX
    w_in_b = w_in[0].astype(BF16)
    for layer in range(depth):
        m = layer // 2
        dense = layer % 2 == 0
        ch = (ffn_w1[m], ffn_w3[m], ffn_w2[m]) if dense else (moe_w1[m], moe_w3[m], moe_w2[m])
        nxt = (w_in[layer + 1],) if layer + 1 < depth else ()
        z = norm_matmul(xs, norm_mix_g[layer], w_in_b)
        o_a, (w1_b,) = hgrn2(z, col_a, lb_all[layer], hgrn_norm_g[layer], casts=(ch[0],))
        o_b, (w3_b,) = s5(z, col_b, s5_lambda_re[layer], s5_lambda_im[layer], s5_log_dt[layer],
                          s5_b_re[layer], s5_b_im[layer], s5_c_re[layer], s5_c_im[layer],
                          s5_d[layer], s5_glu_w[layer], s5_glu_b[layer], casts=(ch[1],))
        o_c, (w2_b,) = retention(z, col_c, ret_norm_g[layer], casts=(ch[2],))
        o_d, rest = rglru(z, col_d, rg_conv_w[layer], rg_conv_b[layer], rg_w_a[layer], rg_b_a[layer],
                          rg_w_x[layer], rg_b_x[layer], rg_lambda[layer], casts=(w_out[layer],) + nxt)
        xs = out_proj((o_a, o_b, o_c, o_d), rest[0], xs)
        if nxt:
            w_in_b = rest[1]
        if dense:
            xs = ffn(xs, norm_ffn_g[layer], w1_b, w3_b, w2_b)
        else:
            last = layer == depth - 1
            xs = moe(xs, norm_ffn_g[layer], router_w[m], w1_b, w3_b, w2_b, final_norm_g if last else None)
            if last:
                return xs.reshape(b_, s_, d_)
    return final_norm(xs, final_norm_g).reshape(b_, s_, d_)
```

```python
import functools
import math

import jax
import jax.numpy as jnp
from jax import lax
from jax.experimental import pallas as pl
from jax.experimental.pallas import tpu as pltpu

F32 = jnp.float32
BF16 = jnp.bfloat16
EPS = 1e-6
HIGHEST = lax.Precision.HIGHEST

V7X_VMEM_BYTES = 64 * 1024 * 1024
VMEM_LIMIT_BYTES = V7X_VMEM_BYTES - 4 * 1024 * 1024
SUBLANES = 8
LANES = 128

A_HEAD_DIM = 128
S5_GROUP = 16
S5_STATE = 64
C_HEADS = 4
ROPE_THETA = 10000.0
D_BLOCK = 128
CONV_WIDTH = 4
RG_C = 8.0
N_EXPERTS = 8
TOP_K = 2
W_MIX = 1024

ROW_TILE = 512
IN_PROJ_TN = 1024
OUT_PROJ_TN = 1024
FFN_TF = 512
MOE_TM = 512
COMBINE_ROWS = 256
HGRN_T = 256
HGRN_C = 16
RET_T = 256
S5_T = 256
S5_SLAB = 128
S5_PW_ROWS = 4 * SUBLANES
RG_T = 256
CAST_ROW_ALIGN = 16
CAST_INLINE_ELEMS = 256 * 1024


def _cparams(*sem):
    return pltpu.CompilerParams(dimension_semantics=sem, vmem_limit_bytes=VMEM_LIMIT_BYTES)


def _rms(xf, g):
    return xf * lax.rsqrt(jnp.mean(xf * xf, axis=-1, keepdims=True) + EPS) * g


def _sigmoid(x):
    return 0.5 + 0.5 * jnp.tanh(0.5 * x)


def _silu(x):
    h = 0.5 * x
    return h + h * jnp.tanh(h)


def _gelu_tanh(x):
    c = math.sqrt(2.0 / math.pi)
    return 0.5 * x * (1.0 + jnp.tanh(c * (x + 0.044715 * (x * x * x))))


def _cast_specs(casts, n_steps, step_index):
    in_specs, out_specs, shapes = [], [], []
    for w, lead in casts:
        *mid, rows, cols = w.shape[1:]
        per = n_steps // math.prod(mid)
        assert per * math.prod(mid) == n_steps, (w.shape, n_steps)
        nr = next(n for n in range(per, 0, -1)
                  if per % n == 0 and rows % (n * CAST_ROW_ALIGN) == 0 and cols % (per // n * LANES) == 0)
        nc = per // nr
        blk = (1,) * len(mid) + (rows // nr, cols // nc)

        def pos(*g, mid=tuple(mid), nr=nr, nc=nc):
            t = step_index(*g)
            idx = [(t // nc) % nr, t % nc]
            t = t // (nr * nc)
            for size in reversed(mid):
                idx.insert(0, t % size)
                t = t // size
            return tuple(idx)

        in_specs.append(pl.BlockSpec((1,) + blk, lambda *g, pos=pos, lead=lead: (lead,) + pos(*g)))
        out_specs.append(pl.BlockSpec(blk, pos))
        shapes.append(jax.ShapeDtypeStruct(w.shape[1:], BF16))
    return in_specs, out_specs, shapes


def _cast_blocks(src_refs, dst_refs):
    for src, dst in zip(src_refs, dst_refs):
        rows, cols = dst.shape[-2:]
        lead = (0,) * (len(dst.shape) - 2)
        if rows * cols <= CAST_INLINE_ELEMS:
            dst[...] = src[0].astype(BF16)
        else:
            def body(c, carry, src=src, dst=dst, lead=lead):
                sl = pl.ds(pl.multiple_of(c * CAST_ROW_ALIGN, CAST_ROW_ALIGN), CAST_ROW_ALIGN)
                dst[lead + (sl, slice(None))] = src[(0,) + lead + (sl, slice(None))].astype(BF16)
                return carry

            lax.fori_loop(0, rows // CAST_ROW_ALIGN, body, 0)


def _with_casts(body, n_in, n_cast):
    def kern(*refs):
        ins = refs[:n_in]
        cast_in = refs[n_in:n_in + n_cast]
        out = refs[n_in + n_cast]
        cast_out = refs[n_in + n_cast + 1:n_in + 2 * n_cast + 1]
        scratch = refs[n_in + 2 * n_cast + 1:]
        body(*ins, out, *scratch)
        _cast_blocks(cast_in, cast_out)

    return kern


def _mixer_call(body, n_steps, step_index, grid, in_specs, out_spec, out_shape, scratch_shapes, sem, name,
                args, casts):
    cin, cout, cshapes = _cast_specs(casts, n_steps, step_index)
    outs = pl.pallas_call(
        _with_casts(body, len(in_specs), len(casts)),
        out_shape=[out_shape] + cshapes,
        grid=grid,
        in_specs=list(in_specs) + cin,
        out_specs=[out_spec] + cout,
        scratch_shapes=scratch_shapes,
        compiler_params=_cparams(*sem),
        name=name,
    )(*args, *[w for w, _ in casts])
    return outs[0], tuple(outs[1:])


def _norm_matmul_kernel(x_ref, g_ref, w_ref, o_ref, h_ref):
    @pl.when(pl.program_id(1) == 0)
    def _():
        h_ref[...] = _rms(x_ref[...], g_ref[...]).astype(BF16)

    o_ref[...] = jnp.dot(h_ref[...], w_ref[...], preferred_element_type=F32)


def norm_matmul(x, g, w):
    s, d = x.shape
    n = w.shape[1]
    tm, tn = min(ROW_TILE, s), IN_PROJ_TN
    return pl.pallas_call(
        _norm_matmul_kernel,
        out_shape=jax.ShapeDtypeStruct((s, n), F32),
        grid=(s // tm, n // tn),
        in_specs=[
            pl.BlockSpec((tm, d), lambda i, j: (i, 0)),
            pl.BlockSpec((1, d), lambda i, j: (0, 0)),
            pl.BlockSpec((d, tn), lambda i, j: (0, j)),
        ],
        out_specs=pl.BlockSpec((tm, tn), lambda i, j: (i, j)),
        scratch_shapes=[pltpu.VMEM((tm, d), BF16)],
        compiler_params=_cparams("parallel", "arbitrary"),
        name="norm_in_proj",
    )(x, g.reshape(1, d), w)


def _out_proj_kernel(oa_ref, ob_ref, oc_ref, od_ref, w_ref, x_ref, o_ref):
    acc = x_ref[...]
    for idx, r in enumerate((oa_ref, ob_ref, oc_ref, od_ref)):
        acc = acc + jnp.dot(r[...], w_ref[idx * W_MIX:(idx + 1) * W_MIX, :], preferred_element_type=F32)
    o_ref[...] = acc


def out_proj(parts, w, x):
    s, d = x.shape
    tm, tn = min(ROW_TILE, s), OUT_PROJ_TN
    part_spec = pl.BlockSpec((tm, W_MIX), lambda i, j: (i, 0))
    return pl.pallas_call(
        _out_proj_kernel,
        out_shape=jax.ShapeDtypeStruct((s, d), F32),
        grid=(s // tm, d // tn),
        in_specs=[part_spec, part_spec, part_spec, part_spec,
                  pl.BlockSpec((4 * W_MIX, tn), lambda i, j: (0, j)),
                  pl.BlockSpec((tm, tn), lambda i, j: (i, j))],
        out_specs=pl.BlockSpec((tm, tn), lambda i, j: (i, j)),
        compiler_params=_cparams("parallel", "arbitrary"),
        name="out_proj",
    )(*parts, w, x)


def _swiglu_step(h, w1, w3, w2):
    a = jnp.dot(h, w1, preferred_element_type=F32)
    b = jnp.dot(h, w3, preferred_element_type=F32)
    act = (_silu(a) * b).astype(BF16)
    return jnp.dot(act, w2, preferred_element_type=F32)


def _ffn_kernel(x_ref, g_ref, w1_ref, w3_ref, w2_ref, o_ref, h_ref):
    @pl.when(pl.program_id(1) == 0)
    def _():
        xf = x_ref[...]
        h_ref[...] = _rms(xf, g_ref[...]).astype(BF16)
        o_ref[...] = xf

    o_ref[...] += _swiglu_step(h_ref[...], w1_ref[...], w3_ref[...], w2_ref[...])


def ffn(x, g, w1, w3, w2):
    s, d = x.shape
    dff = w1.shape[1]
    tm, tf = min(ROW_TILE, s), FFN_TF
    return pl.pallas_call(
        _ffn_kernel,
        out_shape=jax.ShapeDtypeStruct((s, d), F32),
        grid=(s // tm, dff // tf),
        in_specs=[
            pl.BlockSpec((tm, d), lambda i, f: (i, 0), pipeline_mode=pl.Buffered(1)),
            pl.BlockSpec((1, d), lambda i, f: (0, 0)),
            pl.BlockSpec((d, tf), lambda i, f: (0, f)),
            pl.BlockSpec((d, tf), lambda i, f: (0, f)),
            pl.BlockSpec((tf, d), lambda i, f: (f, 0)),
        ],
        out_specs=pl.BlockSpec((tm, d), lambda i, f: (i, 0), pipeline_mode=pl.Buffered(1)),
        scratch_shapes=[pltpu.VMEM((tm, d), BF16)],
        compiler_params=_cparams("parallel", "arbitrary"),
        name="ffn_swiglu",
    )(x, g.reshape(1, d), w1, w3, w2)


def _router_kernel(x_ref, g_ref, wr_ref, comb_ref):
    h = _rms(x_ref[...], g_ref[...])
    logits = jnp.dot(h, wr_ref[...], precision=HIGHEST, preferred_element_type=F32)
    lane = lax.broadcasted_iota(jnp.int32, logits.shape, 1)
    neg = jnp.float32(-jnp.inf)
    logits = jnp.where(lane < N_EXPERTS, logits, neg)
    v1 = jnp.max(logits, axis=-1, keepdims=True)
    i1 = jnp.min(jnp.where(logits == v1, lane, LANES), axis=-1, keepdims=True)
    rest = jnp.where(lane == i1, neg, logits)
    v2 = jnp.max(rest, axis=-1, keepdims=True)
    i2 = jnp.min(jnp.where(rest == v2, lane, LANES), axis=-1, keepdims=True)
    e2 = jnp.exp(v2 - v1)
    g1 = 1.0 / (1.0 + e2)
    g2 = e2 / (1.0 + e2)
    comb_ref[...] = (jnp.where(lane == 0, i1.astype(F32), 0.0) + jnp.where(lane == 1, i2.astype(F32), 0.0)
                     + jnp.where(lane == 2, g1, 0.0) + jnp.where(lane == 3, g2, 0.0))


def router(x, g, router_w):
    s, d = x.shape
    tm = min(ROW_TILE, s)
    wr = jnp.zeros((d, LANES), F32).at[:, :N_EXPERTS].set(router_w.astype(F32))
    return pl.pallas_call(
        _router_kernel,
        out_shape=jax.ShapeDtypeStruct((s, LANES), F32),
        grid=(s // tm,),
        in_specs=[pl.BlockSpec((tm, d), lambda i: (i, 0)),
                  pl.BlockSpec((1, d), lambda i: (0, 0)),
                  pl.BlockSpec((d, LANES), lambda i: (0, 0))],
        out_specs=pl.BlockSpec((tm, LANES), lambda i: (i, 0)),
        compiler_params=_cparams("parallel"),
        name="moe_router",
    )(x, g.reshape(1, d), wr)


def _row_copy(src_hbm, src_row, dst_ref, dst_row, sem):
    return pltpu.make_async_copy(src_hbm.at[pl.ds(src_row, 1)], dst_ref.at[pl.ds(dst_row, 1)], sem)


def _moe_grouped_kernel(te_ref, nu_ref, src_ref, x_hbm, g_ref, gate_ref, w1_ref, w3_ref, w2_ref, o_ref,
                        xbuf, h_ref, sem):
    i = pl.program_id(0)
    f = pl.program_id(1)
    n_used = nu_ref[0]
    used = i < n_used
    tm = xbuf.shape[0]

    def start_gather(tile):
        def issue(r, c):
            _row_copy(x_hbm, src_ref[tile * tm + r], xbuf, r, sem).start()
            return c

        lax.fori_loop(0, tm, issue, 0, unroll=8)

    @pl.when((i == 0) & (f == 0))
    def _():
        start_gather(0)

    @pl.when(jnp.logical_not(used) & (f == 0))
    def _():
        o_ref[...] = jnp.zeros_like(o_ref)

    @pl.when(used & (f == 0))
    def _():
        def wait(r, c):
            _row_copy(x_hbm, 0, xbuf, r, sem).wait()
            return c

        lax.fori_loop(0, tm, wait, 0, unroll=8)
        h_ref[...] = _rms(xbuf[...], g_ref[...]).astype(BF16)
        o_ref[...] = _swiglu_step(h_ref[...], w1_ref[0], w3_ref[0], w2_ref[0])

    @pl.when((f == 1) & (i + 1 < n_used))
    def _():
        start_gather(i + 1)

    @pl.when(used & (f > 0))
    def _():
        o_ref[...] += _swiglu_step(h_ref[...], w1_ref[0], w3_ref[0], w2_ref[0])

    @pl.when(used & (f == pl.num_programs(1) - 1))
    def _():
        o_ref[...] = o_ref[...] * gate_ref[...]


def moe_grouped(x, src, g, gate_sorted, tile_expert, n_used, w1, w3, w2):
    d = x.shape[1]
    p = src.shape[0]
    dff = w1.shape[2]
    tm, tf = MOE_TM, min(FFN_TF, dff // 2)
    nf = dff // tf

    def row_map(i, f, te, nu, sr):
        return (jnp.minimum(i, nu[0] - 1), 0)

    def f_eff(i, f, nu):
        return jnp.where(i < nu[0], f, nf - 1)

    grid_spec = pltpu.PrefetchScalarGridSpec(
        num_scalar_prefetch=3,
        grid=(p // tm, nf),
        in_specs=[
            pl.BlockSpec(memory_space=pl.ANY),
            pl.BlockSpec((1, d), lambda i, f, te, nu, sr: (0, 0)),
            pl.BlockSpec((tm, 1), row_map),
            pl.BlockSpec((1, d, tf), lambda i, f, te, nu, sr: (te[i], 0, f_eff(i, f, nu))),
            pl.BlockSpec((1, d, tf), lambda i, f, te, nu, sr: (te[i], 0, f_eff(i, f, nu))),
            pl.BlockSpec((1, tf, d), lambda i, f, te, nu, sr: (te[i], f_eff(i, f, nu), 0)),
        ],
        out_specs=pl.BlockSpec((tm, d), lambda i, f, te, nu, sr: (i, 0), pipeline_mode=pl.Buffered(1)),
        scratch_shapes=[pltpu.VMEM((tm, d), F32), pltpu.VMEM((tm, d), BF16), pltpu.SemaphoreType.DMA(())],
    )
    return pl.pallas_call(
        _moe_grouped_kernel,
        out_shape=jax.ShapeDtypeStruct((p, d), F32),
        grid_spec=grid_spec,
        compiler_params=_cparams("arbitrary", "arbitrary"),
        name="moe_grouped",
    )(tile_expert, n_used, src, x, g.reshape(1, d), gate_sorted.reshape(p, 1), w1, w3, w2)


def _combine_kernel(dest_ref, x_ref, y_hbm, *rest, final_norm):
    if final_norm:
        fg_ref, o_ref, buf, sem = rest
    else:
        o_ref, buf, sem = rest
    i = pl.program_id(0)
    n_rows = o_ref.shape[0]
    slot = i % 2

    def start_gather(step, sl):
        def issue(r, c):
            for k in range(TOP_K):
                _row_copy(y_hbm, dest_ref[TOP_K * (step * n_rows + r) + k], buf.at[sl, k], r, sem.at[sl]).start()
            return c

        lax.fori_loop(0, n_rows, issue, 0, unroll=4)

    @pl.when(i == 0)
    def _():
        start_gather(0, 0)

    @pl.when(i + 1 < pl.num_programs(0))
    def _():
        start_gather(i + 1, 1 - slot)

    def wait(r, c):
        for k in range(TOP_K):
            _row_copy(y_hbm, 0, buf.at[slot, k], r, sem.at[slot]).wait()
        return c

    lax.fori_loop(0, n_rows, wait, 0, unroll=4)
    acc = x_ref[...]
    for k in range(TOP_K):
        acc = acc + buf[slot, k]
    o_ref[...] = _rms(acc, fg_ref[...]) if final_norm else acc


def combine(x, y, dest, final_g=None):
    s, d = x.shape
    rt = COMBINE_ROWS
    final_norm = final_g is not None
    in_specs = [pl.BlockSpec((rt, d), lambda i, dr: (i, 0)), pl.BlockSpec(memory_space=pl.ANY)]
    args = [x, y]
    if final_norm:
        in_specs.append(pl.BlockSpec((1, d), lambda i, dr: (0, 0)))
        args.append(final_g.astype(F32).reshape(1, d))
    grid_spec = pltpu.PrefetchScalarGridSpec(
        num_scalar_prefetch=1,
        grid=(s // rt,),
        in_specs=in_specs,
        out_specs=pl.BlockSpec((rt, d), lambda i, dr: (i, 0)),
        scratch_shapes=[pltpu.VMEM((2, TOP_K, rt, d), F32), pltpu.SemaphoreType.DMA((2,))],
    )
    return pl.pallas_call(
        functools.partial(_combine_kernel, final_norm=final_norm),
        out_shape=jax.ShapeDtypeStruct((s, d), F32),
        grid_spec=grid_spec,
        compiler_params=_cparams("arbitrary"),
        name="moe_combine",
    )(dest, *args)


def _routing_tables(experts, gates, tm, n_tiles):
    n_assign = experts.size
    e_flat = experts.reshape(n_assign)
    onehot = (e_flat[:, None] == jnp.arange(N_EXPERTS, dtype=jnp.int32)[None, :]).astype(jnp.int32)
    csum = jnp.cumsum(onehot, axis=0)
    rank = jnp.sum(csum * onehot, axis=1) - 1
    counts = csum[-1]
    padded = ((counts + tm - 1) // tm) * tm
    seg_end = jnp.cumsum(padded)
    seg_start = seg_end - padded
    dest = (seg_start[e_flat] + rank).astype(jnp.int32)
    p = n_tiles * tm
    src = jnp.zeros((p,), jnp.int32).at[dest].set(jnp.arange(n_assign, dtype=jnp.int32) // TOP_K)
    gate_sorted = jnp.zeros((p,), F32).at[dest].set(gates.reshape(n_assign))
    n_used = (seg_end[-1] // tm).astype(jnp.int32)
    tile_start = jnp.arange(n_tiles, dtype=jnp.int32) * tm
    tile_e = jnp.sum((tile_start[:, None] >= seg_end[None, :]).astype(jnp.int32), axis=1)
    tile_e = jnp.minimum(tile_e, N_EXPERTS - 1)
    last_e = tile_e[n_used - 1]
    tile_e = jnp.where(jnp.arange(n_tiles) < n_used, tile_e, last_e).astype(jnp.int32)
    return src, dest, gate_sorted, tile_e, n_used.reshape(1)


def moe(x, g, router_w, w1, w3, w2, final_g=None):
    s, _ = x.shape
    info = router(x, g, router_w)
    experts = info[:, 0:TOP_K].astype(jnp.int32)
    gates = info[:, TOP_K:2 * TOP_K]
    tm = MOE_TM
    n_tiles = (TOP_K * s) // tm + N_EXPERTS
    src, dest, gate_sorted, tile_e, n_used = _routing_tables(experts, gates, tm, n_tiles)
    y = moe_grouped(x, src, g, gate_sorted, tile_e, n_used, w1.astype(BF16), w3.astype(BF16), w2.astype(BF16))
    return combine(x, y, dest, final_g)


def _final_norm_kernel(x_ref, g_ref, o_ref):
    o_ref[...] = _rms(x_ref[...], g_ref[...])


def final_norm(x, g):
    s, d = x.shape
    tm = min(ROW_TILE, s)
    return pl.pallas_call(
        _final_norm_kernel,
        out_shape=jax.ShapeDtypeStruct((s, d), F32),
        grid=(s // tm,),
        in_specs=[pl.BlockSpec((tm, d), lambda i: (i, 0)), pl.BlockSpec((1, d), lambda i: (0, 0))],
        out_specs=pl.BlockSpec((tm, d), lambda i: (i, 0)),
        compiler_params=_cparams("parallel"),
        name="final_norm",
    )(x, g.reshape(1, d))


def _hgrn2_kernel(zq_ref, zf_ref, zi_ref, zg_ref, lb_ref, ng_ref, tri_ref, o_ref,
                  st_ref, q_s, k_s, v_s, cum_s, o_s, dec_s, qd_s, kd_s):
    c_len = HGRN_C
    heads, t_len, hd = q_s.shape

    @pl.when(pl.program_id(0) == 0)
    def _():
        st_ref[...] = jnp.zeros_like(st_ref)

    lb = lb_ref[...]
    f = lb + (1.0 - lb) * _sigmoid(zf_ref[...])
    lf = jnp.log(f)
    lf_hi = lf.astype(BF16)
    lf_lo = (lf - lf_hi.astype(F32)).astype(BF16)
    tri = tri_ref[...]
    cum = (jnp.dot(tri, lf_hi, preferred_element_type=F32) + jnp.dot(tri, lf_lo, preferred_element_type=F32))
    cum3 = cum.reshape(t_len // c_len, c_len, heads * hd)
    last3 = cum3[:, c_len - 1:c_len, :]
    last = jnp.broadcast_to(last3, cum3.shape).reshape(t_len, heads * hd)
    dec = jnp.exp(last3.reshape(t_len // c_len, heads * hd))
    q = _silu(zq_ref[...])
    k = 1.0 - f
    qd = (q * jnp.exp(cum)).astype(BF16)
    kd = (k * jnp.exp(last - cum)).astype(BF16)
    v = zi_ref[...]
    cum2 = cum * math.log2(math.e)
    for h in range(heads):
        sl = slice(h * hd, (h + 1) * hd)
        q_s[h] = q[:, sl]
        k_s[h] = k[:, sl]
        v_s[h] = v[:, sl]
        cum_s[h] = cum2[:, sl]
        dec_s[h] = dec[:, sl]
        qd_s[h] = qd[:, sl]
        kd_s[h] = kd[:, sl]

    half = c_len // 2
    row = lax.broadcasted_iota(jnp.int32, (half, hd), 0)
    lane = lax.broadcasted_iota(jnp.int32, (half, hd), 1)

    def body(c, carry):
        r0 = pl.multiple_of(c * c_len, c_len)
        rows = pl.ds(r0, c_len)
        for h in range(heads):
            q_c = q_s[h, rows, :]
            k_c = k_s[h, rows, :]
            cm = cum_s[h, rows, :]
            m_lo = jnp.zeros((half, hd), F32)
            m_hi = jnp.zeros((half, hd), F32)
            for s in range(c_len):
                ks = k_c[s:s + 1, :]
                cs = cm[s:s + 1, :]
                if s < half:
                    w = q_c[:half] * (ks * jnp.exp2(jnp.minimum(cm[:half] - cs, 0.0)))
                    m_lo = jnp.where(lane == s, jnp.sum(w, axis=1, keepdims=True), m_lo)
                w = q_c[half:] * (ks * jnp.exp2(jnp.minimum(cm[half:] - cs, 0.0)))
                m_hi = jnp.where(lane == s, jnp.sum(w, axis=1, keepdims=True), m_hi)
            m_lo = jnp.where(row >= lane, m_lo, 0.0)
            m_hi = jnp.where(row + half >= lane, m_hi, 0.0)
            scores = jnp.concatenate([m_lo, m_hi], axis=0)[:, :c_len].astype(BF16)
            v_c = v_s[h, rows, :].astype(BF16)
            st = st_ref[h]
            o = (jnp.dot(scores, v_c, preferred_element_type=F32)
                 + lax.dot_general(qd_s[h, rows, :], st.astype(BF16), (((1,), (1,)), ((), ())),
                                   preferred_element_type=F32))
            o_s[h, rows, :] = o
            upd = lax.dot_general(v_c, kd_s[h, rows, :], (((0,), (0,)), ((), ())),
                                  preferred_element_type=F32)
            st_ref[h] = st * dec_s[h, pl.ds(c, 1), :] + upd
        return carry

    lax.fori_loop(0, t_len // c_len, body, 0)
    ng = ng_ref[...]
    sg = _silu(zg_ref[...])
    for h in range(heads):
        sl = slice(h * hd, (h + 1) * hd)
        o = o_s[h]
        o = o * lax.rsqrt(jnp.mean(o * o, axis=-1, keepdims=True) + EPS)
        o_ref[:, sl] = (o * ng[:, sl] * sg[:, sl]).astype(BF16)


def hgrn2(z, col0, lb, norm_g, casts=()):
    s = z.shape[0]
    t_len = min(HGRN_T, s)
    heads = W_MIX // A_HEAD_DIM
    cb = col0 // W_MIX
    r = jnp.arange(t_len)
    same = r[:, None] // HGRN_C == r[None, :] // HGRN_C
    tri = (same & (r[None, :] <= r[:, None])).astype(BF16)

    def zspec(k):
        return pl.BlockSpec((t_len, W_MIX), lambda i: (i, cb + k))

    vec = pl.BlockSpec((1, W_MIX), lambda i: (0, 0))
    sq = pl.BlockSpec((t_len, t_len), lambda i: (0, 0))
    per_head = lambda dt: pltpu.VMEM((heads, t_len, A_HEAD_DIM), dt)
    return _mixer_call(
        _hgrn2_kernel, s // t_len, lambda i: i,
        grid=(s // t_len,),
        in_specs=[zspec(0), zspec(1), zspec(2), zspec(3), vec, vec, sq],
        out_spec=pl.BlockSpec((t_len, W_MIX), lambda i: (i, 0)),
        out_shape=jax.ShapeDtypeStruct((s, W_MIX), BF16),
        scratch_shapes=[pltpu.VMEM((heads, A_HEAD_DIM, A_HEAD_DIM), F32)]
        + [per_head(F32)] * 5 + [pltpu.VMEM((heads, t_len // HGRN_C, A_HEAD_DIM), F32)] + [per_head(BF16)] * 2,
        sem=("arbitrary",), name="hgrn2",
        args=(z, z, z, z, lb.reshape(1, W_MIX), norm_g.reshape(1, W_MIX), tri), casts=casts)


def _s5_kernel(u_ref, wb_ref, pw_ref, wc_ref, d_ref, gw_ref, gb_ref, o_ref,
               carry_ref, bu_s, y_s):
    i = pl.program_id(0)
    j = pl.program_id(1)
    n_slab = pl.num_programs(1)
    t_len = bu_s.shape[0]
    half = bu_s.shape[1] // 2

    @pl.when(i == 0)
    def _():
        carry_ref[j] = jnp.zeros((SUBLANES, 2 * half), F32)

    u = u_ref[...]
    bu_s[...] = jnp.dot(u.astype(BF16), wb_ref[0], preferred_element_type=F32)
    p8_re, p8_im = pw_ref[0, 0:8, :half], pw_ref[0, 0:8, half:]

    def body(b, carry):
        c_re, c_im = carry
        r0 = pl.multiple_of(b * SUBLANES, SUBLANES)
        blk = bu_s[pl.ds(r0, SUBLANES), :]
        x_re, x_im = blk[:, :half], blk[:, half:]
        for step, k in enumerate((1, 2, 4)):
            a_re = pw_ref[0, 8 * (step + 1):8 * (step + 2), :half]
            a_im = pw_ref[0, 8 * (step + 1):8 * (step + 2), half:]
            s_re = pltpu.roll(x_re, k, 0)
            s_im = pltpu.roll(x_im, k, 0)
            x_re, x_im = (x_re + a_re * s_re - a_im * s_im,
                          x_im + a_re * s_im + a_im * s_re)
        x_re, x_im = (x_re + p8_re * c_re - p8_im * c_im,
                      x_im + p8_re * c_im + p8_im * c_re)
        bu_s[pl.ds(r0, SUBLANES), :] = jnp.concatenate([x_re, x_im], axis=1)
        n_re = jnp.broadcast_to(x_re[SUBLANES - 1:SUBLANES, :], x_re.shape)
        n_im = jnp.broadcast_to(x_im[SUBLANES - 1:SUBLANES, :], x_im.shape)
        return n_re, n_im

    c0 = carry_ref[j]
    c_re, c_im = lax.fori_loop(0, t_len // SUBLANES, body, (c0[:, :half], c0[:, half:]))
    carry_ref[j] = jnp.concatenate([c_re, c_im], axis=1)

    y = jnp.dot(bu_s[...].astype(BF16), wc_ref[0], preferred_element_type=F32)
    y = _gelu_tanh(y + d_ref[0] * u)
    y_s[j] = y

    @pl.when(j == n_slab - 1)
    def _():
        yf = jnp.concatenate([y_s[s] for s in range(y_s.shape[0])], axis=1)
        gate = jnp.dot(yf.astype(BF16), gw_ref[...], preferred_element_type=F32) + gb_ref[...]
        o_ref[...] = (yf * _sigmoid(gate)).astype(BF16)


def s5(z, col0, lam_re, lam_im, log_dt, b_re, b_im, c_re, c_im, d_skip, glu_w, glu_b, casts=()):
    s = z.shape[0]
    t_len = min(S5_T, s)
    groups, n_state = lam_re.shape
    gps = S5_SLAB // S5_GROUP
    n_slab = groups // gps
    half = gps * n_state
    cb = col0 // S5_SLAB
    lam_re = lam_re.astype(F32)
    lam_im = lam_im.astype(F32)
    dt = jnp.exp(log_dt.astype(F32))[:, None]
    mag = jnp.exp(lam_re * dt)
    ang = lam_im * dt
    ab_re = mag * jnp.cos(ang)
    ab_im = mag * jnp.sin(ang)
    den = lam_re * lam_re + lam_im * lam_im
    num_re = ab_re - 1.0
    coef_re = (num_re * lam_re + ab_im * lam_im) / den
    coef_im = (ab_im * lam_re - num_re * lam_im) / den
    br = b_re.astype(F32)
    bi = b_im.astype(F32)
    bb_re = coef_re[..., None] * br - coef_im[..., None] * bi
    bb_im = coef_re[..., None] * bi + coef_im[..., None] * br
    eye = jnp.eye(gps, dtype=F32)

    def blockdiag_in(bb):
        t = bb.reshape(n_slab, gps, n_state, S5_GROUP)
        return jnp.einsum('sgnp,gh->sgphn', t, eye).reshape(n_slab, gps * S5_GROUP, gps * n_state)

    wb = jnp.concatenate([blockdiag_in(bb_re), blockdiag_in(bb_im)], axis=-1).astype(BF16)

    def blockdiag_out(cc):
        t = cc.reshape(n_slab, gps, S5_GROUP, n_state)
        return jnp.einsum('sgpn,gh->sgnhp', t, eye).reshape(n_slab, gps * n_state, gps * S5_GROUP)

    wc = jnp.concatenate([blockdiag_out(c_re.astype(F32)), -blockdiag_out(c_im.astype(F32))],
                         axis=1).astype(BF16)
    r8 = jnp.arange(SUBLANES)
    expo = jnp.concatenate([r8 + 1.0] + [jnp.full((SUBLANES,), float(k)) for k in (1, 2, 4)]).astype(F32)
    keep = jnp.concatenate([jnp.ones((SUBLANES,), F32)] + [(r8 >= k).astype(F32) for k in (1, 2, 4)])
    expo = expo[:, None, None]
    p_mag = jnp.exp(expo * (lam_re * dt)[None]) * keep[:, None, None]
    p_re = (p_mag * jnp.cos(expo * ang[None])).reshape(S5_PW_ROWS, n_slab, half)
    p_im = (p_mag * jnp.sin(expo * ang[None])).reshape(S5_PW_ROWS, n_slab, half)
    pw = jnp.moveaxis(jnp.concatenate([p_re, p_im], axis=-1), 1, 0)

    return _mixer_call(
        _s5_kernel, (s // t_len) * n_slab, lambda i, j: i * n_slab + j,
        grid=(s // t_len, n_slab),
        in_specs=[
            pl.BlockSpec((t_len, S5_SLAB), lambda i, j: (i, cb + j)),
            pl.BlockSpec((1, S5_SLAB, 2 * half), lambda i, j: (j, 0, 0)),
            pl.BlockSpec((1, S5_PW_ROWS, 2 * half), lambda i, j: (j, 0, 0)),
            pl.BlockSpec((1, 2 * half, S5_SLAB), lambda i, j: (j, 0, 0)),
            pl.BlockSpec((1, 1, S5_SLAB), lambda i, j: (j, 0, 0)),
            pl.BlockSpec((W_MIX, W_MIX), lambda i, j: (0, 0)),
            pl.BlockSpec((1, W_MIX), lambda i, j: (0, 0)),
        ],
        out_spec=pl.BlockSpec((t_len, W_MIX), lambda i, j: (i, 0)),
        out_shape=jax.ShapeDtypeStruct((s, W_MIX), BF16),
        scratch_shapes=[pltpu.VMEM((n_slab, SUBLANES, 2 * half), F32),
                        pltpu.VMEM((t_len, 2 * half), F32),
                        pltpu.VMEM((n_slab, t_len, S5_SLAB), F32)],
        sem=("arbitrary", "arbitrary"), name="s5",
        args=(z, wb, pw, wc, d_skip.astype(F32).reshape(n_slab, 1, S5_SLAB), glu_w.astype(BF16),
              glu_b.astype(F32).reshape(1, W_MIX)), casts=casts)


def _retention_kernel(zq_ref, zk_ref, zv_ref, zg_ref, cos_ref, sin_ref, dmat_ref, qdec_ref, kdec_ref,
                      cdec_ref, ng_ref, o_ref, st_ref):
    @pl.when(pl.program_id(0) == 0)
    def _():
        st_ref[...] = jnp.zeros_like(st_ref)

    heads, hd, _ = st_ref.shape
    half = hd // 2
    cos = cos_ref[...]
    sin = sin_ref[...]

    def rope(t):
        t1, t2 = t[:, :half], t[:, half:]
        return jnp.concatenate([t1 * cos - t2 * sin, t1 * sin + t2 * cos], axis=1)

    for h in range(heads):
        sl = slice(h * hd, (h + 1) * hd)
        q = rope(zq_ref[:, sl])
        k = rope(zk_ref[:, sl]) * hd ** -0.5
        v = zv_ref[:, sl].astype(BF16)
        qb = q.astype(BF16)
        scores = lax.dot_general(qb, k.astype(BF16), (((1,), (1,)), ((), ())),
                                 preferred_element_type=F32) * dmat_ref[h]
        st = st_ref[h]
        o = (jnp.dot(scores.astype(BF16), v, preferred_element_type=F32)
             + jnp.dot(qb, st.astype(BF16), preferred_element_type=F32) * qdec_ref[h])
        kd = (k * kdec_ref[h]).astype(BF16)
        st_ref[h] = cdec_ref[h] * st + lax.dot_general(kd, v, (((0,), (0,)), ((), ())),
                                                       preferred_element_type=F32)
        mu = jnp.mean(o, axis=-1, keepdims=True)
        oc = o - mu
        var = jnp.mean(oc * oc, axis=-1, keepdims=True)
        o = oc * lax.rsqrt(var + EPS)
        o_ref[:, sl] = (o * ng_ref[:, sl] * _silu(zg_ref[:, sl])).astype(BF16)


def retention(z, col0, norm_g, casts=()):
    s = z.shape[0]
    t_len = min(RET_T, s)
    hd = W_MIX // C_HEADS
    pos = jnp.arange(s, dtype=F32)
    inv_freq = ROPE_THETA ** (-jnp.arange(0, hd, 2, dtype=F32) / hd)
    ang = pos[:, None] * inv_freq[None, :]
    cos = jnp.cos(ang)
    sin = jnp.sin(ang)
    log_gamma = jnp.log(1.0 - 2.0 ** (-5.0 - jnp.arange(C_HEADS, dtype=F32)))
    idx = jnp.arange(t_len, dtype=F32)
    rel = idx[:, None] - idx[None, :]
    dmat = jnp.where(rel[None] >= 0, jnp.exp(jnp.maximum(rel, 0.0)[None] * log_gamma[:, None, None]), 0.0)
    qdec = jnp.exp((idx + 1.0)[None, :] * log_gamma[:, None])[..., None]
    kdec = jnp.exp((t_len - 1.0 - idx)[None, :] * log_gamma[:, None])[..., None]
    cdec = jnp.broadcast_to(jnp.exp(t_len * log_gamma)[:, None, None], (C_HEADS, 1, hd))

    cb = col0 // W_MIX

    def zspec(k):
        return pl.BlockSpec((t_len, W_MIX), lambda i: (i, cb + k))

    tab = pl.BlockSpec((t_len, hd // 2), lambda i: (i, 0))
    full = lambda shape: pl.BlockSpec(shape, lambda i: (0,) * len(shape))
    return _mixer_call(
        _retention_kernel, s // t_len, lambda i: i,
        grid=(s // t_len,),
        in_specs=[zspec(0), zspec(1), zspec(2), zspec(3), tab, tab,
                  full((C_HEADS, t_len, t_len)), full((C_HEADS, t_len, 1)), full((C_HEADS, t_len, 1)),
                  full((C_HEADS, 1, hd)), full((1, W_MIX))],
        out_spec=pl.BlockSpec((t_len, W_MIX), lambda i: (i, 0)),
        out_shape=jax.ShapeDtypeStruct((s, W_MIX), BF16),
        scratch_shapes=[pltpu.VMEM((C_HEADS, hd, hd), F32)],
        sem=("arbitrary",), name="retention",
        args=(z, z, z, z, cos, sin, dmat, qdec, kdec, cdec, norm_g.reshape(1, W_MIX)), casts=casts)


def _rglru_kernel(zg_ref, zx_ref, cw_ref, cb_ref, wa_ref, ba_ref, wx_ref, bx_ref, sp_ref, o_ref,
                  xbuf, h_ref, a_s, u_s):
    t_len = zx_ref.shape[0]

    @pl.when(pl.program_id(0) == 0)
    def _():
        xbuf[0:SUBLANES, :] = jnp.zeros((SUBLANES, W_MIX), F32)
        h_ref[...] = jnp.zeros_like(h_ref)

    xbuf[SUBLANES:, :] = zx_ref[...]
    xc = cb_ref[...]
    for tap in range(CONV_WIDTH):
        off = SUBLANES - (CONV_WIDTH - 1) + tap
        xc = xc + xbuf[off:off + t_len, :] * cw_ref[tap:tap + 1, :]
    xbuf[0:SUBLANES, :] = xbuf[t_len:t_len + SUBLANES, :]
    xcb = xc.astype(BF16)
    n_blk = W_MIX // D_BLOCK
    pre_r = jnp.concatenate(
        [jnp.dot(xcb[:, b * D_BLOCK:(b + 1) * D_BLOCK], wa_ref[b], preferred_element_type=F32)
         for b in range(n_blk)], axis=1)
    pre_i = jnp.concatenate(
        [jnp.dot(xcb[:, b * D_BLOCK:(b + 1) * D_BLOCK], wx_ref[b], preferred_element_type=F32)
         for b in range(n_blk)], axis=1)
    r = _sigmoid(pre_r + ba_ref[...])
    gi = _sigmoid(pre_i + bx_ref[...])
    log_a = -RG_C * r * sp_ref[...]
    a = jnp.exp(log_a)
    a_s[...] = a
    u_s[...] = jnp.sqrt(1.0 - a * a) * (gi * xc)
    row = lax.broadcasted_iota(jnp.int32, (SUBLANES, W_MIX), 0)

    def body(b, h):
        r0 = pl.multiple_of(b * SUBLANES, SUBLANES)
        aa = a_s[pl.ds(r0, SUBLANES), :]
        uu = u_s[pl.ds(r0, SUBLANES), :]
        for k in (1, 2, 4):
            us = jnp.where(row >= k, pltpu.roll(uu, k, 0), 0.0)
            as_ = jnp.where(row >= k, pltpu.roll(aa, k, 0), 1.0)
            uu = uu + aa * us
            aa = aa * as_
        hh = uu + aa * h
        u_s[pl.ds(r0, SUBLANES), :] = hh
        return jnp.broadcast_to(hh[SUBLANES - 1:SUBLANES, :], hh.shape)

    h_ref[...] = lax.fori_loop(0, t_len // SUBLANES, body, h_ref[...])
    o_ref[...] = (_gelu_tanh(zg_ref[...]) * u_s[...]).astype(BF16)


def rglru(z, col0, conv_w, conv_b, w_a, b_a, w_x, b_x, lam, casts=()):
    s = z.shape[0]
    t_len = min(RG_T, s)
    cb = col0 // W_MIX
    sp = jax.nn.softplus(-lam.astype(F32)).reshape(1, W_MIX)
    row = lambda a: a.astype(F32).reshape(1, W_MIX)
    full = lambda shape: pl.BlockSpec(shape, lambda i: (0,) * len(shape))
    n_blk = W_MIX // D_BLOCK
    return _mixer_call(
        _rglru_kernel, s // t_len, lambda i: i,
        grid=(s // t_len,),
        in_specs=[pl.BlockSpec((t_len, W_MIX), lambda i: (i, cb)),
                  pl.BlockSpec((t_len, W_MIX), lambda i: (i, cb + 1)),
                  full((CONV_WIDTH, W_MIX)), full((1, W_MIX)),
                  full((n_blk, D_BLOCK, D_BLOCK)), full((1, W_MIX)),
                  full((n_blk, D_BLOCK, D_BLOCK)), full((1, W_MIX)), full((1, W_MIX))],
        out_spec=pl.BlockSpec((t_len, W_MIX), lambda i: (i, 0)),
        out_shape=jax.ShapeDtypeStruct((s, W_MIX), BF16),
        scratch_shapes=[pltpu.VMEM((t_len + SUBLANES, W_MIX), F32),
                        pltpu.VMEM((SUBLANES, W_MIX), F32),
                        pltpu.VMEM((t_len, W_MIX), F32),
                        pltpu.VMEM((t_len, W_MIX), F32)],
        sem=("arbitrary",), name="rglru",
        args=(z, z, conv_w.astype(F32), row(conv_b), w_a.astype(BF16), row(b_a), w_x.astype(BF16), row(b_x), sp),
        casts=casts)


def kernel(x, norm_mix_g, norm_ffn_g, final_norm_g, w_in, w_out, hgrn_lb_logits, hgrn_norm_g, s5_lambda_re, s5_lambda_im, s5_log_dt, s5_b_re, s5_b_im, s5_c_re, s5_c_im, s5_d, s5_glu_w, s5_glu_b, ret_norm_g, rg_conv_w, rg_conv_b, rg_w_a, rg_b_a, rg_w_x, rg_b_x, rg_lambda, ffn_w1, ffn_w3, ffn_w2, router_w, moe_w1, moe_w3, moe_w2):
    b_, s_, d_ = x.shape
    depth = w_in.shape[0]
    xs = x.reshape(b_ * s_, d_).astype(F32)
    lb_p = jax.nn.softmax(hgrn_lb_logits.astype(F32), axis=0)
    lb_all = jnp.cumsum(lb_p, axis=0) - lb_p[0]
    col_a, col_b, col_c, col_d = 0, 4 * W_MIX, 5 * W_MIX, 9 * W_MIX
    w_in_b = w_in[0].astype(BF16)
    for layer in range(depth):
        m = layer // 2
        dense = layer % 2 == 0
        ch = (ffn_w1, ffn_w3, ffn_w2) if dense else (moe_w1, moe_w3, moe_w2)
        nxt = ((w_in, layer + 1),) if layer + 1 < depth else ()
        z = norm_matmul(xs, norm_mix_g[layer], w_in_b)
        o_a, (w1_b,) = hgrn2(z, col_a, lb_all[layer], hgrn_norm_g[layer], casts=((ch[0], m),))
        o_b, (w3_b,) = s5(z, col_b, s5_lambda_re[layer], s5_lambda_im[layer], s5_log_dt[layer],
                          s5_b_re[layer], s5_b_im[layer], s5_c_re[layer], s5_c_im[layer],
                          s5_d[layer], s5_glu_w[layer], s5_glu_b[layer], casts=((ch[1], m),))
        o_c, (w2_b,) = retention(z, col_c, ret_norm_g[layer], casts=((ch[2], m),))
        o_d, rest = rglru(z, col_d, rg_conv_w[layer], rg_conv_b[layer], rg_w_a[layer], rg_b_a[layer],
                          rg_w_x[layer], rg_b_x[layer], rg_lambda[layer], casts=((w_out, layer),) + nxt)
        xs = out_proj((o_a, o_b, o_c, o_d), rest[0], xs)
        if nxt:
            w_in_b = rest[1]
        if dense:
            xs = ffn(xs, norm_ffn_g[layer], w1_b, w3_b, w2_b)
        else:
            last = layer == depth - 1
            xs = moe(xs, norm_ffn_g[layer], router_w[m], w1_b, w3_b, w2_b, final_norm_g if last else None)
            if last:
                return xs.reshape(b_, s_, d_)
    return final_norm(xs, final_norm_g).reshape(b_, s_, d_)
```

```python
import functools
import math

import jax
import jax.numpy as jnp
from jax import lax
from jax.experimental import pallas as pl
from jax.experimental.pallas import tpu as pltpu

F32 = jnp.float32
BF16 = jnp.bfloat16
EPS = 1e-6
HIGHEST = lax.Precision.HIGHEST

V7X_VMEM_BYTES = 64 * 1024 * 1024
VMEM_LIMIT_BYTES = V7X_VMEM_BYTES - 4 * 1024 * 1024
SUBLANES = 8
LANES = 128

A_HEAD_DIM = 128
S5_GROUP = 16
S5_STATE = 64
C_HEADS = 4
ROPE_THETA = 10000.0
D_BLOCK = 128
CONV_WIDTH = 4
RG_C = 8.0
N_EXPERTS = 8
TOP_K = 2
W_MIX = 1024

ROW_TILE = 512
IN_PROJ_TN = 1024
OUT_PROJ_TN = 1024
FFN_TF = 512
MOE_TM = 512
COMBINE_ROWS = 256
HGRN_T = 256
HGRN_C = 16
RET_T = 256
S5_T = 512
S5_SLAB = 128
S5_PW_ROWS = 4 * SUBLANES
RG_T = 512
CAST_ROW_ALIGN = 16
CAST_INLINE_ELEMS = 256 * 1024


def _cparams(*sem):
    return pltpu.CompilerParams(dimension_semantics=sem, vmem_limit_bytes=VMEM_LIMIT_BYTES)


def _rms(xf, g):
    return xf * lax.rsqrt(jnp.mean(xf * xf, axis=-1, keepdims=True) + EPS) * g


def _sigmoid(x):
    return 0.5 + 0.5 * jnp.tanh(0.5 * x)


def _silu(x):
    h = 0.5 * x
    return h + h * jnp.tanh(h)


def _gelu_tanh(x):
    c = math.sqrt(2.0 / math.pi)
    return 0.5 * x * (1.0 + jnp.tanh(c * (x + 0.044715 * (x * x * x))))


def _cast_specs(casts, n_steps, step_index):
    in_specs, out_specs, shapes = [], [], []
    for w, lead in casts:
        *mid, rows, cols = w.shape[1:]
        per = n_steps // math.prod(mid)
        assert per * math.prod(mid) == n_steps, (w.shape, n_steps)
        nr = next(n for n in range(per, 0, -1)
                  if per % n == 0 and rows % (n * CAST_ROW_ALIGN) == 0 and cols % (per // n * LANES) == 0)
        nc = per // nr
        blk = (1,) * len(mid) + (rows // nr, cols // nc)

        def pos(*g, mid=tuple(mid), nr=nr, nc=nc):
            t = step_index(*g)
            idx = [(t // nc) % nr, t % nc]
            t = t // (nr * nc)
            for size in reversed(mid):
                idx.insert(0, t % size)
                t = t // size
            return tuple(idx)

        in_specs.append(pl.BlockSpec((1,) + blk, lambda *g, pos=pos, lead=lead: (lead,) + pos(*g)))
        out_specs.append(pl.BlockSpec(blk, pos))
        shapes.append(jax.ShapeDtypeStruct(w.shape[1:], BF16))
    return in_specs, out_specs, shapes


def _cast_blocks(src_refs, dst_refs):
    for src, dst in zip(src_refs, dst_refs):
        rows, cols = dst.shape[-2:]
        lead = (0,) * (len(dst.shape) - 2)
        if rows * cols <= CAST_INLINE_ELEMS:
            dst[...] = src[0].astype(BF16)
        else:
            def body(c, carry, src=src, dst=dst, lead=lead):
                sl = pl.ds(pl.multiple_of(c * CAST_ROW_ALIGN, CAST_ROW_ALIGN), CAST_ROW_ALIGN)
                dst[lead + (sl, slice(None))] = src[(0,) + lead + (sl, slice(None))].astype(BF16)
                return carry

            lax.fori_loop(0, rows // CAST_ROW_ALIGN, body, 0)


def _with_casts(body, n_in, n_cast):
    def kern(*refs):
        ins = refs[:n_in]
        cast_in = refs[n_in:n_in + n_cast]
        out = refs[n_in + n_cast]
        cast_out = refs[n_in + n_cast + 1:n_in + 2 * n_cast + 1]
        scratch = refs[n_in + 2 * n_cast + 1:]
        body(*ins, out, *scratch)
        _cast_blocks(cast_in, cast_out)

    return kern


def _mixer_call(body, n_steps, step_index, grid, in_specs, out_spec, out_shape, scratch_shapes, sem, name,
                args, casts):
    cin, cout, cshapes = _cast_specs(casts, n_steps, step_index)
    outs = pl.pallas_call(
        _with_casts(body, len(in_specs), len(casts)),
        out_shape=[out_shape] + cshapes,
        grid=grid,
        in_specs=list(in_specs) + cin,
        out_specs=[out_spec] + cout,
        scratch_shapes=scratch_shapes,
        compiler_params=_cparams(*sem),
        name=name,
    )(*args, *[w for w, _ in casts])
    return outs[0], tuple(outs[1:])


def _norm_matmul_kernel(x_ref, g_ref, w_ref, o_ref, h_ref):
    @pl.when(pl.program_id(1) == 0)
    def _():
        h_ref[...] = _rms(x_ref[...], g_ref[...]).astype(BF16)

    o_ref[...] = jnp.dot(h_ref[...], w_ref[...], preferred_element_type=F32)


def norm_matmul(x, g, w):
    s, d = x.shape
    n = w.shape[1]
    tm, tn = min(ROW_TILE, s), IN_PROJ_TN
    return pl.pallas_call(
        _norm_matmul_kernel,
        out_shape=jax.ShapeDtypeStruct((s, n), F32),
        grid=(s // tm, n // tn),
        in_specs=[
            pl.BlockSpec((tm, d), lambda i, j: (i, 0)),
            pl.BlockSpec((1, d), lambda i, j: (0, 0)),
            pl.BlockSpec((d, tn), lambda i, j: (0, j)),
        ],
        out_specs=pl.BlockSpec((tm, tn), lambda i, j: (i, j)),
        scratch_shapes=[pltpu.VMEM((tm, d), BF16)],
        compiler_params=_cparams("parallel", "arbitrary"),
        name="norm_in_proj",
    )(x, g.reshape(1, d), w)


def _out_proj_kernel(oa_ref, ob_ref, oc_ref, od_ref, w_ref, x_ref, o_ref):
    acc = x_ref[...]
    for idx, r in enumerate((oa_ref, ob_ref, oc_ref, od_ref)):
        acc = acc + jnp.dot(r[...], w_ref[idx * W_MIX:(idx + 1) * W_MIX, :], preferred_element_type=F32)
    o_ref[...] = acc


def out_proj(parts, w, x, casts=()):
    s, d = x.shape
    tm, tn = min(ROW_TILE, s), OUT_PROJ_TN
    n_col = d // tn
    part_spec = pl.BlockSpec((tm, W_MIX), lambda i, j: (i, 0))
    return _mixer_call(
        _out_proj_kernel, (s // tm) * n_col, lambda i, j: i * n_col + j,
        grid=(s // tm, n_col),
        in_specs=[part_spec, part_spec, part_spec, part_spec,
                  pl.BlockSpec((4 * W_MIX, tn), lambda i, j: (0, j)),
                  pl.BlockSpec((tm, tn), lambda i, j: (i, j))],
        out_spec=pl.BlockSpec((tm, tn), lambda i, j: (i, j)),
        out_shape=jax.ShapeDtypeStruct((s, d), F32),
        scratch_shapes=[], sem=("arbitrary", "arbitrary"), name="out_proj",
        args=(*parts, w, x), casts=casts)


def _swiglu_step(h, w1, w3, w2):
    a = jnp.dot(h, w1, preferred_element_type=F32)
    b = jnp.dot(h, w3, preferred_element_type=F32)
    act = (_silu(a) * b).astype(BF16)
    return jnp.dot(act, w2, preferred_element_type=F32)


def _ffn_kernel(x_ref, g_ref, w1_ref, w3_ref, w2_ref, o_ref, h_ref):
    @pl.when(pl.program_id(1) == 0)
    def _():
        xf = x_ref[...]
        h_ref[...] = _rms(xf, g_ref[...]).astype(BF16)
        o_ref[...] = xf

    o_ref[...] += _swiglu_step(h_ref[...], w1_ref[...], w3_ref[...], w2_ref[...])


def ffn(x, g, w1, w3, w2):
    s, d = x.shape
    dff = w1.shape[1]
    tm, tf = min(ROW_TILE, s), FFN_TF
    return pl.pallas_call(
        _ffn_kernel,
        out_shape=jax.ShapeDtypeStruct((s, d), F32),
        grid=(s // tm, dff // tf),
        in_specs=[
            pl.BlockSpec((tm, d), lambda i, f: (i, 0), pipeline_mode=pl.Buffered(1)),
            pl.BlockSpec((1, d), lambda i, f: (0, 0)),
            pl.BlockSpec((d, tf), lambda i, f: (0, f)),
            pl.BlockSpec((d, tf), lambda i, f: (0, f)),
            pl.BlockSpec((tf, d), lambda i, f: (f, 0)),
        ],
        out_specs=pl.BlockSpec((tm, d), lambda i, f: (i, 0), pipeline_mode=pl.Buffered(1)),
        scratch_shapes=[pltpu.VMEM((tm, d), BF16)],
        compiler_params=_cparams("parallel", "arbitrary"),
        name="ffn_swiglu",
    )(x, g.reshape(1, d), w1, w3, w2)


def _router_kernel(x_ref, g_ref, wr_ref, comb_ref):
    h = _rms(x_ref[...], g_ref[...])
    logits = jnp.dot(h, wr_ref[...], precision=HIGHEST, preferred_element_type=F32)
    lane = lax.broadcasted_iota(jnp.int32, logits.shape, 1)
    neg = jnp.float32(-jnp.inf)
    logits = jnp.where(lane < N_EXPERTS, logits, neg)
    v1 = jnp.max(logits, axis=-1, keepdims=True)
    i1 = jnp.min(jnp.where(logits == v1, lane, LANES), axis=-1, keepdims=True)
    rest = jnp.where(lane == i1, neg, logits)
    v2 = jnp.max(rest, axis=-1, keepdims=True)
    i2 = jnp.min(jnp.where(rest == v2, lane, LANES), axis=-1, keepdims=True)
    e2 = jnp.exp(v2 - v1)
    g1 = 1.0 / (1.0 + e2)
    g2 = e2 / (1.0 + e2)
    comb_ref[...] = (jnp.where(lane == 0, i1.astype(F32), 0.0) + jnp.where(lane == 1, i2.astype(F32), 0.0)
                     + jnp.where(lane == 2, g1, 0.0) + jnp.where(lane == 3, g2, 0.0))


def router(x, g, router_w):
    s, d = x.shape
    tm = min(ROW_TILE, s)
    wr = jnp.zeros((d, LANES), F32).at[:, :N_EXPERTS].set(router_w.astype(F32))
    return pl.pallas_call(
        _router_kernel,
        out_shape=jax.ShapeDtypeStruct((s, LANES), F32),
        grid=(s // tm,),
        in_specs=[pl.BlockSpec((tm, d), lambda i: (i, 0)),
                  pl.BlockSpec((1, d), lambda i: (0, 0)),
                  pl.BlockSpec((d, LANES), lambda i: (0, 0))],
        out_specs=pl.BlockSpec((tm, LANES), lambda i: (i, 0)),
        compiler_params=_cparams("parallel"),
        name="moe_router",
    )(x, g.reshape(1, d), wr)


def _row_copy(src_hbm, src_row, dst_ref, dst_row, sem):
    return pltpu.make_async_copy(src_hbm.at[pl.ds(src_row, 1)], dst_ref.at[pl.ds(dst_row, 1)], sem)


def _moe_grouped_kernel(te_ref, nu_ref, src_ref, x_hbm, g_ref, gate_ref, w1_ref, w3_ref, w2_ref, o_ref,
                        xbuf, h_ref, sem):
    i = pl.program_id(0)
    f = pl.program_id(1)
    n_used = nu_ref[0]
    used = i < n_used
    tm = xbuf.shape[0]

    def start_gather(tile):
        def issue(r, c):
            _row_copy(x_hbm, src_ref[tile * tm + r], xbuf, r, sem).start()
            return c

        lax.fori_loop(0, tm, issue, 0, unroll=8)

    @pl.when((i == 0) & (f == 0))
    def _():
        start_gather(0)

    @pl.when(jnp.logical_not(used) & (f == 0))
    def _():
        o_ref[...] = jnp.zeros_like(o_ref)

    @pl.when(used & (f == 0))
    def _():
        def wait(r, c):
            _row_copy(x_hbm, 0, xbuf, r, sem).wait()
            return c

        lax.fori_loop(0, tm, wait, 0, unroll=8)
        h_ref[...] = _rms(xbuf[...], g_ref[...]).astype(BF16)
        o_ref[...] = _swiglu_step(h_ref[...], w1_ref[0], w3_ref[0], w2_ref[0])

    @pl.when((f == 1) & (i + 1 < n_used))
    def _():
        start_gather(i + 1)

    @pl.when(used & (f > 0))
    def _():
        o_ref[...] += _swiglu_step(h_ref[...], w1_ref[0], w3_ref[0], w2_ref[0])

    @pl.when(used & (f == pl.num_programs(1) - 1))
    def _():
        o_ref[...] = o_ref[...] * gate_ref[...]


def moe_grouped(x, src, g, gate_sorted, tile_expert, n_used, w1, w3, w2):
    d = x.shape[1]
    p = src.shape[0]
    dff = w1.shape[2]
    tm, tf = MOE_TM, min(FFN_TF, dff // 2)
    nf = dff // tf

    def row_map(i, f, te, nu, sr):
        return (jnp.minimum(i, nu[0] - 1), 0)

    def f_eff(i, f, nu):
        return jnp.where(i < nu[0], f, nf - 1)

    grid_spec = pltpu.PrefetchScalarGridSpec(
        num_scalar_prefetch=3,
        grid=(p // tm, nf),
        in_specs=[
            pl.BlockSpec(memory_space=pl.ANY),
            pl.BlockSpec((1, d), lambda i, f, te, nu, sr: (0, 0)),
            pl.BlockSpec((tm, 1), row_map),
            pl.BlockSpec((1, d, tf), lambda i, f, te, nu, sr: (te[i], 0, f_eff(i, f, nu))),
            pl.BlockSpec((1, d, tf), lambda i, f, te, nu, sr: (te[i], 0, f_eff(i, f, nu))),
            pl.BlockSpec((1, tf, d), lambda i, f, te, nu, sr: (te[i], f_eff(i, f, nu), 0)),
        ],
        out_specs=pl.BlockSpec((tm, d), lambda i, f, te, nu, sr: (i, 0), pipeline_mode=pl.Buffered(1)),
        scratch_shapes=[pltpu.VMEM((tm, d), F32), pltpu.VMEM((tm, d), BF16), pltpu.SemaphoreType.DMA(())],
    )
    return pl.pallas_call(
        _moe_grouped_kernel,
        out_shape=jax.ShapeDtypeStruct((p, d), F32),
        grid_spec=grid_spec,
        compiler_params=_cparams("arbitrary", "arbitrary"),
        name="moe_grouped",
    )(tile_expert, n_used, src, x, g.reshape(1, d), gate_sorted.reshape(p, 1), w1, w3, w2)


def _combine_kernel(dest_ref, x_ref, y_hbm, *rest, final_norm):
    if final_norm:
        fg_ref, o_ref, buf, sem = rest
    else:
        o_ref, buf, sem = rest
    i = pl.program_id(0)
    n_rows = o_ref.shape[0]
    slot = i % 2

    def start_gather(step, sl):
        def issue(r, c):
            for k in range(TOP_K):
                _row_copy(y_hbm, dest_ref[TOP_K * (step * n_rows + r) + k], buf.at[sl, k], r, sem.at[sl]).start()
            return c

        lax.fori_loop(0, n_rows, issue, 0, unroll=4)

    @pl.when(i == 0)
    def _():
        start_gather(0, 0)

    @pl.when(i + 1 < pl.num_programs(0))
    def _():
        start_gather(i + 1, 1 - slot)

    def wait(r, c):
        for k in range(TOP_K):
            _row_copy(y_hbm, 0, buf.at[slot, k], r, sem.at[slot]).wait()
        return c

    lax.fori_loop(0, n_rows, wait, 0, unroll=4)
    acc = x_ref[...]
    for k in range(TOP_K):
        acc = acc + buf[slot, k]
    o_ref[...] = _rms(acc, fg_ref[...]) if final_norm else acc


def combine(x, y, dest, final_g=None):
    s, d = x.shape
    rt = COMBINE_ROWS
    final_norm = final_g is not None
    in_specs = [pl.BlockSpec((rt, d), lambda i, dr: (i, 0)), pl.BlockSpec(memory_space=pl.ANY)]
    args = [x, y]
    if final_norm:
        in_specs.append(pl.BlockSpec((1, d), lambda i, dr: (0, 0)))
        args.append(final_g.astype(F32).reshape(1, d))
    grid_spec = pltpu.PrefetchScalarGridSpec(
        num_scalar_prefetch=1,
        grid=(s // rt,),
        in_specs=in_specs,
        out_specs=pl.BlockSpec((rt, d), lambda i, dr: (i, 0)),
        scratch_shapes=[pltpu.VMEM((2, TOP_K, rt, d), F32), pltpu.SemaphoreType.DMA((2,))],
    )
    return pl.pallas_call(
        functools.partial(_combine_kernel, final_norm=final_norm),
        out_shape=jax.ShapeDtypeStruct((s, d), F32),
        grid_spec=grid_spec,
        compiler_params=_cparams("arbitrary"),
        name="moe_combine",
    )(dest, *args)


def _routing_tables(experts, gates, tm, n_tiles):
    n_assign = experts.size
    e_flat = experts.reshape(n_assign)
    onehot = (e_flat[:, None] == jnp.arange(N_EXPERTS, dtype=jnp.int32)[None, :]).astype(jnp.int32)
    csum = jnp.cumsum(onehot, axis=0)
    rank = jnp.sum(csum * onehot, axis=1) - 1
    counts = csum[-1]
    padded = ((counts + tm - 1) // tm) * tm
    seg_end = jnp.cumsum(padded)
    seg_start = seg_end - padded
    dest = (seg_start[e_flat] + rank).astype(jnp.int32)
    p = n_tiles * tm
    src = jnp.zeros((p,), jnp.int32).at[dest].set(jnp.arange(n_assign, dtype=jnp.int32) // TOP_K)
    gate_sorted = jnp.zeros((p,), F32).at[dest].set(gates.reshape(n_assign))
    n_used = (seg_end[-1] // tm).astype(jnp.int32)
    tile_start = jnp.arange(n_tiles, dtype=jnp.int32) * tm
    tile_e = jnp.sum((tile_start[:, None] >= seg_end[None, :]).astype(jnp.int32), axis=1)
    tile_e = jnp.minimum(tile_e, N_EXPERTS - 1)
    last_e = tile_e[n_used - 1]
    tile_e = jnp.where(jnp.arange(n_tiles) < n_used, tile_e, last_e).astype(jnp.int32)
    return src, dest, gate_sorted, tile_e, n_used.reshape(1)


def moe(x, g, router_w, w1, w3, w2, final_g=None):
    s, _ = x.shape
    info = router(x, g, router_w)
    experts = info[:, 0:TOP_K].astype(jnp.int32)
    gates = info[:, TOP_K:2 * TOP_K]
    tm = MOE_TM
    n_tiles = (TOP_K * s) // tm + N_EXPERTS
    src, dest, gate_sorted, tile_e, n_used = _routing_tables(experts, gates, tm, n_tiles)
    y = moe_grouped(x, src, g, gate_sorted, tile_e, n_used, w1.astype(BF16), w3.astype(BF16), w2.astype(BF16))
    return combine(x, y, dest, final_g)


def _final_norm_kernel(x_ref, g_ref, o_ref):
    o_ref[...] = _rms(x_ref[...], g_ref[...])


def final_norm(x, g):
    s, d = x.shape
    tm = min(ROW_TILE, s)
    return pl.pallas_call(
        _final_norm_kernel,
        out_shape=jax.ShapeDtypeStruct((s, d), F32),
        grid=(s // tm,),
        in_specs=[pl.BlockSpec((tm, d), lambda i: (i, 0)), pl.BlockSpec((1, d), lambda i: (0, 0))],
        out_specs=pl.BlockSpec((tm, d), lambda i: (i, 0)),
        compiler_params=_cparams("parallel"),
        name="final_norm",
    )(x, g.reshape(1, d))


def _hgrn2_kernel(zq_ref, zf_ref, zi_ref, zg_ref, lb_ref, ng_ref, tri_ref, o_ref,
                  st_ref, q_s, k_s, v_s, cum_s, o_s, dec_s, qd_s, kd_s):
    c_len = HGRN_C
    heads, t_len, hd = q_s.shape

    @pl.when(pl.program_id(0) == 0)
    def _():
        st_ref[...] = jnp.zeros_like(st_ref)

    lb = lb_ref[...]
    f = lb + (1.0 - lb) * _sigmoid(zf_ref[...])
    lf = jnp.log(f)
    lf_hi = lf.astype(BF16)
    lf_lo = (lf - lf_hi.astype(F32)).astype(BF16)
    tri = tri_ref[...]
    cum = (jnp.dot(tri, lf_hi, preferred_element_type=F32) + jnp.dot(tri, lf_lo, preferred_element_type=F32))
    cum3 = cum.reshape(t_len // c_len, c_len, heads * hd)
    last3 = cum3[:, c_len - 1:c_len, :]
    last = jnp.broadcast_to(last3, cum3.shape).reshape(t_len, heads * hd)
    dec = jnp.exp(last3.reshape(t_len // c_len, heads * hd))
    q = _silu(zq_ref[...])
    k = 1.0 - f
    qd = (q * jnp.exp(cum)).astype(BF16)
    kd = (k * jnp.exp(last - cum)).astype(BF16)
    v = zi_ref[...]
    cum2 = cum * math.log2(math.e)
    for h in range(heads):
        sl = slice(h * hd, (h + 1) * hd)
        q_s[h] = q[:, sl]
        k_s[h] = k[:, sl]
        v_s[h] = v[:, sl]
        cum_s[h] = cum2[:, sl]
        dec_s[h] = dec[:, sl]
        qd_s[h] = qd[:, sl]
        kd_s[h] = kd[:, sl]

    half = c_len // 2
    row = lax.broadcasted_iota(jnp.int32, (half, hd), 0)
    lane = lax.broadcasted_iota(jnp.int32, (half, hd), 1)

    def body(c, carry):
        r0 = pl.multiple_of(c * c_len, c_len)
        rows = pl.ds(r0, c_len)
        for h in range(heads):
            q_c = q_s[h, rows, :]
            k_c = k_s[h, rows, :]
            cm = cum_s[h, rows, :]
            m_lo = jnp.zeros((half, hd), F32)
            m_hi = jnp.zeros((half, hd), F32)
            for s in range(c_len):
                ks = k_c[s:s + 1, :]
                cs = cm[s:s + 1, :]
                if s < half:
                    w = q_c[:half] * (ks * jnp.exp2(jnp.minimum(cm[:half] - cs, 0.0)))
                    m_lo = jnp.where(lane == s, jnp.sum(w, axis=1, keepdims=True), m_lo)
                w = q_c[half:] * (ks * jnp.exp2(jnp.minimum(cm[half:] - cs, 0.0)))
                m_hi = jnp.where(lane == s, jnp.sum(w, axis=1, keepdims=True), m_hi)
            m_lo = jnp.where(row >= lane, m_lo, 0.0)
            m_hi = jnp.where(row + half >= lane, m_hi, 0.0)
            scores = jnp.concatenate([m_lo, m_hi], axis=0)[:, :c_len].astype(BF16)
            v_c = v_s[h, rows, :].astype(BF16)
            st = st_ref[h]
            o = (jnp.dot(scores, v_c, preferred_element_type=F32)
                 + lax.dot_general(qd_s[h, rows, :], st.astype(BF16), (((1,), (1,)), ((), ())),
                                   preferred_element_type=F32))
            o_s[h, rows, :] = o
            upd = lax.dot_general(v_c, kd_s[h, rows, :], (((0,), (0,)), ((), ())),
                                  preferred_element_type=F32)
            st_ref[h] = st * dec_s[h, pl.ds(c, 1), :] + upd
        return carry

    lax.fori_loop(0, t_len // c_len, body, 0)
    ng = ng_ref[...]
    sg = _silu(zg_ref[...])
    for h in range(heads):
        sl = slice(h * hd, (h + 1) * hd)
        o = o_s[h]
        o = o * lax.rsqrt(jnp.mean(o * o, axis=-1, keepdims=True) + EPS)
        o_ref[:, sl] = (o * ng[:, sl] * sg[:, sl]).astype(BF16)


def hgrn2(z, col0, lb, norm_g, casts=()):
    s = z.shape[0]
    t_len = min(HGRN_T, s)
    heads = W_MIX // A_HEAD_DIM
    cb = col0 // W_MIX
    r = jnp.arange(t_len)
    same = r[:, None] // HGRN_C == r[None, :] // HGRN_C
    tri = (same & (r[None, :] <= r[:, None])).astype(BF16)

    def zspec(k):
        return pl.BlockSpec((t_len, W_MIX), lambda i: (i, cb + k))

    vec = pl.BlockSpec((1, W_MIX), lambda i: (0, 0))
    sq = pl.BlockSpec((t_len, t_len), lambda i: (0, 0))
    per_head = lambda dt: pltpu.VMEM((heads, t_len, A_HEAD_DIM), dt)
    return _mixer_call(
        _hgrn2_kernel, s // t_len, lambda i: i,
        grid=(s // t_len,),
        in_specs=[zspec(0), zspec(1), zspec(2), zspec(3), vec, vec, sq],
        out_spec=pl.BlockSpec((t_len, W_MIX), lambda i: (i, 0)),
        out_shape=jax.ShapeDtypeStruct((s, W_MIX), BF16),
        scratch_shapes=[pltpu.VMEM((heads, A_HEAD_DIM, A_HEAD_DIM), F32)]
        + [per_head(F32)] * 5 + [pltpu.VMEM((heads, t_len // HGRN_C, A_HEAD_DIM), F32)] + [per_head(BF16)] * 2,
        sem=("arbitrary",), name="hgrn2",
        args=(z, z, z, z, lb.reshape(1, W_MIX), norm_g.reshape(1, W_MIX), tri), casts=casts)


def _s5_kernel(u_ref, wb_ref, pw_ref, wc_ref, d_ref, gw_ref, gb_ref, o_ref,
               carry_ref, bu_s, y_s):
    i = pl.program_id(0)
    j = pl.program_id(1)
    n_slab = pl.num_programs(1)
    t_len = bu_s.shape[0]
    half = bu_s.shape[1] // 2

    @pl.when(i == 0)
    def _():
        carry_ref[j] = jnp.zeros((SUBLANES, 2 * half), F32)

    u = u_ref[...]
    bu_s[...] = jnp.dot(u.astype(BF16), wb_ref[0], preferred_element_type=F32)
    p8_re, p8_im = pw_ref[0, 0:8, :half], pw_ref[0, 0:8, half:]

    def body(b, carry):
        c_re, c_im = carry
        r0 = pl.multiple_of(b * SUBLANES, SUBLANES)
        blk = bu_s[pl.ds(r0, SUBLANES), :]
        x_re, x_im = blk[:, :half], blk[:, half:]
        for step, k in enumerate((1, 2, 4)):
            a_re = pw_ref[0, 8 * (step + 1):8 * (step + 2), :half]
            a_im = pw_ref[0, 8 * (step + 1):8 * (step + 2), half:]
            s_re = pltpu.roll(x_re, k, 0)
            s_im = pltpu.roll(x_im, k, 0)
            x_re, x_im = (x_re + a_re * s_re - a_im * s_im,
                          x_im + a_re * s_im + a_im * s_re)
        x_re, x_im = (x_re + p8_re * c_re - p8_im * c_im,
                      x_im + p8_re * c_im + p8_im * c_re)
        bu_s[pl.ds(r0, SUBLANES), :] = jnp.concatenate([x_re, x_im], axis=1)
        n_re = jnp.broadcast_to(x_re[SUBLANES - 1:SUBLANES, :], x_re.shape)
        n_im = jnp.broadcast_to(x_im[SUBLANES - 1:SUBLANES, :], x_im.shape)
        return n_re, n_im

    c0 = carry_ref[j]
    c_re, c_im = lax.fori_loop(0, t_len // SUBLANES, body, (c0[:, :half], c0[:, half:]))
    carry_ref[j] = jnp.concatenate([c_re, c_im], axis=1)

    y = jnp.dot(bu_s[...].astype(BF16), wc_ref[0], preferred_element_type=F32)
    y = _gelu_tanh(y + d_ref[0] * u)
    y_s[j] = y

    @pl.when(j == n_slab - 1)
    def _():
        yf = jnp.concatenate([y_s[s] for s in range(y_s.shape[0])], axis=1)
        gate = jnp.dot(yf.astype(BF16), gw_ref[...], preferred_element_type=F32) + gb_ref[...]
        o_ref[...] = (yf * _sigmoid(gate)).astype(BF16)


def s5(z, col0, lam_re, lam_im, log_dt, b_re, b_im, c_re, c_im, d_skip, glu_w, glu_b, casts=()):
    s = z.shape[0]
    t_len = min(S5_T, s)
    groups, n_state = lam_re.shape
    gps = S5_SLAB // S5_GROUP
    n_slab = groups // gps
    half = gps * n_state
    cb = col0 // S5_SLAB
    lam_re = lam_re.astype(F32)
    lam_im = lam_im.astype(F32)
    dt = jnp.exp(log_dt.astype(F32))[:, None]
    mag = jnp.exp(lam_re * dt)
    ang = lam_im * dt
    ab_re = mag * jnp.cos(ang)
    ab_im = mag * jnp.sin(ang)
    den = lam_re * lam_re + lam_im * lam_im
    num_re = ab_re - 1.0
    coef_re = (num_re * lam_re + ab_im * lam_im) / den
    coef_im = (ab_im * lam_re - num_re * lam_im) / den
    br = b_re.astype(F32)
    bi = b_im.astype(F32)
    bb_re = coef_re[..., None] * br - coef_im[..., None] * bi
    bb_im = coef_re[..., None] * bi + coef_im[..., None] * br
    eye = jnp.eye(gps, dtype=F32)

    def blockdiag_in(bb):
        t = bb.reshape(n_slab, gps, n_state, S5_GROUP)
        return jnp.einsum('sgnp,gh->sgphn', t, eye).reshape(n_slab, gps * S5_GROUP, gps * n_state)

    wb = jnp.concatenate([blockdiag_in(bb_re), blockdiag_in(bb_im)], axis=-1).astype(BF16)

    def blockdiag_out(cc):
        t = cc.reshape(n_slab, gps, S5_GROUP, n_state)
        return jnp.einsum('sgpn,gh->sgnhp', t, eye).reshape(n_slab, gps * n_state, gps * S5_GROUP)

    wc = jnp.concatenate([blockdiag_out(c_re.astype(F32)), -blockdiag_out(c_im.astype(F32))],
                         axis=1).astype(BF16)
    r8 = jnp.arange(SUBLANES)
    expo = jnp.concatenate([r8 + 1.0] + [jnp.full((SUBLANES,), float(k)) for k in (1, 2, 4)]).astype(F32)
    keep = jnp.concatenate([jnp.ones((SUBLANES,), F32)] + [(r8 >= k).astype(F32) for k in (1, 2, 4)])
    expo = expo[:, None, None]
    p_mag = jnp.exp(expo * (lam_re * dt)[None]) * keep[:, None, None]
    p_re = (p_mag * jnp.cos(expo * ang[None])).reshape(S5_PW_ROWS, n_slab, half)
    p_im = (p_mag * jnp.sin(expo * ang[None])).reshape(S5_PW_ROWS, n_slab, half)
    pw = jnp.moveaxis(jnp.concatenate([p_re, p_im], axis=-1), 1, 0)

    return _mixer_call(
        _s5_kernel, (s // t_len) * n_slab, lambda i, j: i * n_slab + j,
        grid=(s // t_len, n_slab),
        in_specs=[
            pl.BlockSpec((t_len, S5_SLAB), lambda i, j: (i, cb + j)),
            pl.BlockSpec((1, S5_SLAB, 2 * half), lambda i, j: (j, 0, 0)),
            pl.BlockSpec((1, S5_PW_ROWS, 2 * half), lambda i, j: (j, 0, 0)),
            pl.BlockSpec((1, 2 * half, S5_SLAB), lambda i, j: (j, 0, 0)),
            pl.BlockSpec((1, 1, S5_SLAB), lambda i, j: (j, 0, 0)),
            pl.BlockSpec((W_MIX, W_MIX), lambda i, j: (0, 0)),
            pl.BlockSpec((1, W_MIX), lambda i, j: (0, 0)),
        ],
        out_spec=pl.BlockSpec((t_len, W_MIX), lambda i, j: (i, 0)),
        out_shape=jax.ShapeDtypeStruct((s, W_MIX), BF16),
        scratch_shapes=[pltpu.VMEM((n_slab, SUBLANES, 2 * half), F32),
                        pltpu.VMEM((t_len, 2 * half), F32),
                        pltpu.VMEM((n_slab, t_len, S5_SLAB), F32)],
        sem=("arbitrary", "arbitrary"), name="s5",
        args=(z, wb, pw, wc, d_skip.astype(F32).reshape(n_slab, 1, S5_SLAB), glu_w.astype(BF16),
              glu_b.astype(F32).reshape(1, W_MIX)), casts=casts)


def _retention_kernel(zq_ref, zk_ref, zv_ref, zg_ref, cos_ref, sin_ref, dmat_ref, qdec_ref, kdec_ref,
                      cdec_ref, ng_ref, o_ref, st_ref):
    @pl.when(pl.program_id(0) == 0)
    def _():
        st_ref[...] = jnp.zeros_like(st_ref)

    heads, hd, _ = st_ref.shape
    half = hd // 2
    cos = cos_ref[...]
    sin = sin_ref[...]

    def rope(t):
        t1, t2 = t[:, :half], t[:, half:]
        return jnp.concatenate([t1 * cos - t2 * sin, t1 * sin + t2 * cos], axis=1)

    for h in range(heads):
        sl = slice(h * hd, (h + 1) * hd)
        q = rope(zq_ref[:, sl])
        k = rope(zk_ref[:, sl]) * hd ** -0.5
        v = zv_ref[:, sl].astype(BF16)
        qb = q.astype(BF16)
        scores = lax.dot_general(qb, k.astype(BF16), (((1,), (1,)), ((), ())),
                                 preferred_element_type=F32) * dmat_ref[h]
        st = st_ref[h]
        o = (jnp.dot(scores.astype(BF16), v, preferred_element_type=F32)
             + jnp.dot(qb, st.astype(BF16), preferred_element_type=F32) * qdec_ref[h])
        kd = (k * kdec_ref[h]).astype(BF16)
        st_ref[h] = cdec_ref[h] * st + lax.dot_general(kd, v, (((0,), (0,)), ((), ())),
                                                       preferred_element_type=F32)
        mu = jnp.mean(o, axis=-1, keepdims=True)
        oc = o - mu
        var = jnp.mean(oc * oc, axis=-1, keepdims=True)
        o = oc * lax.rsqrt(var + EPS)
        o_ref[:, sl] = (o * ng_ref[:, sl] * _silu(zg_ref[:, sl])).astype(BF16)


def retention(z, col0, norm_g, casts=()):
    s = z.shape[0]
    t_len = min(RET_T, s)
    hd = W_MIX // C_HEADS
    pos = jnp.arange(s, dtype=F32)
    inv_freq = ROPE_THETA ** (-jnp.arange(0, hd, 2, dtype=F32) / hd)
    ang = pos[:, None] * inv_freq[None, :]
    cos = jnp.cos(ang)
    sin = jnp.sin(ang)
    log_gamma = jnp.log(1.0 - 2.0 ** (-5.0 - jnp.arange(C_HEADS, dtype=F32)))
    idx = jnp.arange(t_len, dtype=F32)
    rel = idx[:, None] - idx[None, :]
    dmat = jnp.where(rel[None] >= 0, jnp.exp(jnp.maximum(rel, 0.0)[None] * log_gamma[:, None, None]), 0.0)
    qdec = jnp.exp((idx + 1.0)[None, :] * log_gamma[:, None])[..., None]
    kdec = jnp.exp((t_len - 1.0 - idx)[None, :] * log_gamma[:, None])[..., None]
    cdec = jnp.broadcast_to(jnp.exp(t_len * log_gamma)[:, None, None], (C_HEADS, 1, hd))

    cb = col0 // W_MIX

    def zspec(k):
        return pl.BlockSpec((t_len, W_MIX), lambda i: (i, cb + k))

    tab = pl.BlockSpec((t_len, hd // 2), lambda i: (i, 0))
    full = lambda shape: pl.BlockSpec(shape, lambda i: (0,) * len(shape))
    return _mixer_call(
        _retention_kernel, s // t_len, lambda i: i,
        grid=(s // t_len,),
        in_specs=[zspec(0), zspec(1), zspec(2), zspec(3), tab, tab,
                  full((C_HEADS, t_len, t_len)), full((C_HEADS, t_len, 1)), full((C_HEADS, t_len, 1)),
                  full((C_HEADS, 1, hd)), full((1, W_MIX))],
        out_spec=pl.BlockSpec((t_len, W_MIX), lambda i: (i, 0)),
        out_shape=jax.ShapeDtypeStruct((s, W_MIX), BF16),
        scratch_shapes=[pltpu.VMEM((C_HEADS, hd, hd), F32)],
        sem=("arbitrary",), name="retention",
        args=(z, z, z, z, cos, sin, dmat, qdec, kdec, cdec, norm_g.reshape(1, W_MIX)), casts=casts)


def _rglru_kernel(zg_ref, zx_ref, cw_ref, cb_ref, wa_ref, ba_ref, wx_ref, bx_ref, sp_ref, o_ref,
                  xbuf, h_ref, a_s, u_s):
    t_len = zx_ref.shape[0]

    @pl.when(pl.program_id(0) == 0)
    def _():
        xbuf[0:SUBLANES, :] = jnp.zeros((SUBLANES, W_MIX), F32)
        h_ref[...] = jnp.zeros_like(h_ref)

    xbuf[SUBLANES:, :] = zx_ref[...]
    xc = cb_ref[...]
    for tap in range(CONV_WIDTH):
        off = SUBLANES - (CONV_WIDTH - 1) + tap
        xc = xc + xbuf[off:off + t_len, :] * cw_ref[tap:tap + 1, :]
    xbuf[0:SUBLANES, :] = xbuf[t_len:t_len + SUBLANES, :]
    xcb = xc.astype(BF16)
    n_blk = W_MIX // D_BLOCK
    pre_r = jnp.concatenate(
        [jnp.dot(xcb[:, b * D_BLOCK:(b + 1) * D_BLOCK], wa_ref[b], preferred_element_type=F32)
         for b in range(n_blk)], axis=1)
    pre_i = jnp.concatenate(
        [jnp.dot(xcb[:, b * D_BLOCK:(b + 1) * D_BLOCK], wx_ref[b], preferred_element_type=F32)
         for b in range(n_blk)], axis=1)
    r = _sigmoid(pre_r + ba_ref[...])
    gi = _sigmoid(pre_i + bx_ref[...])
    log_a = -RG_C * r * sp_ref[...]
    a = jnp.exp(log_a)
    a_s[...] = a
    u_s[...] = jnp.sqrt(1.0 - a * a) * (gi * xc)
    row = lax.broadcasted_iota(jnp.int32, (SUBLANES, W_MIX), 0)

    def body(b, h):
        r0 = pl.multiple_of(b * SUBLANES, SUBLANES)
        aa = a_s[pl.ds(r0, SUBLANES), :]
        uu = u_s[pl.ds(r0, SUBLANES), :]
        for k in (1, 2, 4):
            us = jnp.where(row >= k, pltpu.roll(uu, k, 0), 0.0)
            as_ = jnp.where(row >= k, pltpu.roll(aa, k, 0), 1.0)
            uu = uu + aa * us
            aa = aa * as_
        hh = uu + aa * h
        u_s[pl.ds(r0, SUBLANES), :] = hh
        return jnp.broadcast_to(hh[SUBLANES - 1:SUBLANES, :], hh.shape)

    h_ref[...] = lax.fori_loop(0, t_len // SUBLANES, body, h_ref[...])
    o_ref[...] = (_gelu_tanh(zg_ref[...]) * u_s[...]).astype(BF16)


def rglru(z, col0, conv_w, conv_b, w_a, b_a, w_x, b_x, lam, casts=()):
    s = z.shape[0]
    t_len = min(RG_T, s)
    cb = col0 // W_MIX
    sp = jax.nn.softplus(-lam.astype(F32)).reshape(1, W_MIX)
    row = lambda a: a.astype(F32).reshape(1, W_MIX)
    full = lambda shape: pl.BlockSpec(shape, lambda i: (0,) * len(shape))
    n_blk = W_MIX // D_BLOCK
    return _mixer_call(
        _rglru_kernel, s // t_len, lambda i: i,
        grid=(s // t_len,),
        in_specs=[pl.BlockSpec((t_len, W_MIX), lambda i: (i, cb)),
                  pl.BlockSpec((t_len, W_MIX), lambda i: (i, cb + 1)),
                  full((CONV_WIDTH, W_MIX)), full((1, W_MIX)),
                  full((n_blk, D_BLOCK, D_BLOCK)), full((1, W_MIX)),
                  full((n_blk, D_BLOCK, D_BLOCK)), full((1, W_MIX)), full((1, W_MIX))],
        out_spec=pl.BlockSpec((t_len, W_MIX), lambda i: (i, 0)),
        out_shape=jax.ShapeDtypeStruct((s, W_MIX), BF16),
        scratch_shapes=[pltpu.VMEM((t_len + SUBLANES, W_MIX), F32),
                        pltpu.VMEM((SUBLANES, W_MIX), F32),
                        pltpu.VMEM((t_len, W_MIX), F32),
                        pltpu.VMEM((t_len, W_MIX), F32)],
        sem=("arbitrary",), name="rglru",
        args=(z, z, conv_w.astype(F32), row(conv_b), w_a.astype(BF16), row(b_a), w_x.astype(BF16), row(b_x), sp),
        casts=casts)


def kernel(x, norm_mix_g, norm_ffn_g, final_norm_g, w_in, w_out, hgrn_lb_logits, hgrn_norm_g, s5_lambda_re, s5_lambda_im, s5_log_dt, s5_b_re, s5_b_im, s5_c_re, s5_c_im, s5_d, s5_glu_w, s5_glu_b, ret_norm_g, rg_conv_w, rg_conv_b, rg_w_a, rg_b_a, rg_w_x, rg_b_x, rg_lambda, ffn_w1, ffn_w3, ffn_w2, router_w, moe_w1, moe_w3, moe_w2):
    b_, s_, d_ = x.shape
    depth = w_in.shape[0]
    xs = x.reshape(b_ * s_, d_).astype(F32)
    lb_p = jax.nn.softmax(hgrn_lb_logits.astype(F32), axis=0)
    lb_all = jnp.cumsum(lb_p, axis=0) - lb_p[0]
    col_a, col_b, col_c, col_d = 0, 4 * W_MIX, 5 * W_MIX, 9 * W_MIX
    w_in_b = w_in[0].astype(BF16)
    for layer in range(depth):
        m = layer // 2
        dense = layer % 2 == 0
        ch = (ffn_w1, ffn_w3, ffn_w2) if dense else (moe_w1, moe_w3, moe_w2)
        nxt = ((w_in, layer + 1),) if layer + 1 < depth else ()
        z = norm_matmul(xs, norm_mix_g[layer], w_in_b)
        o_a, (w1_b,) = hgrn2(z, col_a, lb_all[layer], hgrn_norm_g[layer], casts=((ch[0], m),))
        o_b, (w3_b,) = s5(z, col_b, s5_lambda_re[layer], s5_lambda_im[layer], s5_log_dt[layer],
                          s5_b_re[layer], s5_b_im[layer], s5_c_re[layer], s5_c_im[layer],
                          s5_d[layer], s5_glu_w[layer], s5_glu_b[layer], casts=((ch[1], m),))
        o_c, _ = retention(z, col_c, ret_norm_g[layer])
        o_d, rest = rglru(z, col_d, rg_conv_w[layer], rg_conv_b[layer], rg_w_a[layer], rg_b_a[layer],
                          rg_w_x[layer], rg_b_x[layer], rg_lambda[layer], casts=((w_out, layer),) + nxt)
        xs, (w2_b,) = out_proj((o_a, o_b, o_c, o_d), rest[0], xs, casts=((ch[2], m),))
        if nxt:
            w_in_b = rest[1]
        if dense:
            xs = ffn(xs, norm_ffn_g[layer], w1_b, w3_b, w2_b)
        else:
            last = layer == depth - 1
            xs = moe(xs, norm_ffn_g[layer], router_w[m], w1_b, w3_b, w2_b, final_norm_g if last else None)
            if last:
                return xs.reshape(b_, s_, d_)
    return final_norm(xs, final_norm_g).reshape(b_, s_, d_)
```

```python
import functools
import math

import jax
import jax.numpy as jnp
from jax import lax
from jax.experimental import pallas as pl
from jax.experimental.pallas import tpu as pltpu

F32 = jnp.float32
BF16 = jnp.bfloat16
EPS = 1e-6

V7X_VMEM_BYTES = 64 * 1024 * 1024
VMEM_LIMIT_BYTES = V7X_VMEM_BYTES - 4 * 1024 * 1024
SUBLANES = 8
LANES = 128

A_HEAD_DIM = 128
S5_GROUP = 16
S5_STATE = 64
C_HEADS = 4
ROPE_THETA = 10000.0
D_BLOCK = 128
CONV_WIDTH = 4
RG_C = 8.0
N_EXPERTS = 8
TOP_K = 2
W_MIX = 1024

ROW_TILE = 512
IN_PROJ_TN = 1024
OUT_PROJ_TN = 1024
FFN_TF = 512
MOE_TM = 512
COMBINE_ROWS = 256
HGRN_T = 256
HGRN_C = 16
RET_T = 256
S5_T = 512
S5_SLAB = 128
S5_PW_ROWS = 4 * SUBLANES
RG_T = 512
CAST_ROW_ALIGN = 16
CAST_INLINE_ELEMS = 256 * 1024


def _cparams(*sem):
    return pltpu.CompilerParams(dimension_semantics=sem, vmem_limit_bytes=VMEM_LIMIT_BYTES)


def _rms(xf, g):
    return xf * lax.rsqrt(jnp.mean(xf * xf, axis=-1, keepdims=True) + EPS) * g


def _sigmoid(x):
    return 0.5 + 0.5 * jnp.tanh(0.5 * x)


def _silu(x):
    h = 0.5 * x
    return h + h * jnp.tanh(h)


def _gelu_tanh(x):
    c = math.sqrt(2.0 / math.pi)
    return 0.5 * x * (1.0 + jnp.tanh(c * (x + 0.044715 * (x * x * x))))


def _cast_specs(casts, n_steps, step_index):
    in_specs, out_specs, shapes = [], [], []
    for w, lead in casts:
        *mid, rows, cols = w.shape[1:]
        per = n_steps // math.prod(mid)
        assert per * math.prod(mid) == n_steps, (w.shape, n_steps)
        nr = next(n for n in range(per, 0, -1)
                  if per % n == 0 and rows % (n * CAST_ROW_ALIGN) == 0 and cols % (per // n * LANES) == 0)
        nc = per // nr
        blk = (1,) * len(mid) + (rows // nr, cols // nc)

        def pos(*g, mid=tuple(mid), nr=nr, nc=nc):
            t = step_index(*g)
            idx = [(t // nc) % nr, t % nc]
            t = t // (nr * nc)
            for size in reversed(mid):
                idx.insert(0, t % size)
                t = t // size
            return tuple(idx)

        in_specs.append(pl.BlockSpec((1,) + blk, lambda *g, pos=pos, lead=lead: (lead,) + pos(*g)))
        out_specs.append(pl.BlockSpec(blk, pos))
        shapes.append(jax.ShapeDtypeStruct(w.shape[1:], BF16))
    return in_specs, out_specs, shapes


def _cast_blocks(src_refs, dst_refs):
    for src, dst in zip(src_refs, dst_refs):
        rows, cols = dst.shape[-2:]
        lead = (0,) * (len(dst.shape) - 2)
        if rows * cols <= CAST_INLINE_ELEMS:
            dst[...] = src[0].astype(BF16)
        else:
            def body(c, carry, src=src, dst=dst, lead=lead):
                sl = pl.ds(pl.multiple_of(c * CAST_ROW_ALIGN, CAST_ROW_ALIGN), CAST_ROW_ALIGN)
                dst[lead + (sl, slice(None))] = src[(0,) + lead + (sl, slice(None))].astype(BF16)
                return carry

            lax.fori_loop(0, rows // CAST_ROW_ALIGN, body, 0)


def _with_casts(body, n_in, n_cast, cast_when):
    def kern(*refs):
        ins = refs[:n_in]
        cast_in = refs[n_in:n_in + n_cast]
        out = refs[n_in + n_cast]
        cast_out = refs[n_in + n_cast + 1:n_in + 2 * n_cast + 1]
        scratch = refs[n_in + 2 * n_cast + 1:]
        body(*ins, out, *scratch)
        if cast_when is None or not n_cast:
            _cast_blocks(cast_in, cast_out)
        else:
            pl.when(cast_when())(lambda: _cast_blocks(cast_in, cast_out))

    return kern


def _mixer_call(body, n_steps, step_index, grid, in_specs, out_spec, out_shape, scratch_shapes, sem, name,
                args, casts, cast_when=None):
    cin, cout, cshapes = _cast_specs(casts, n_steps, step_index)
    outs = pl.pallas_call(
        _with_casts(body, len(in_specs), len(casts), cast_when),
        out_shape=[out_shape] + cshapes,
        grid=grid,
        in_specs=list(in_specs) + cin,
        out_specs=[out_spec] + cout,
        scratch_shapes=scratch_shapes,
        compiler_params=_cparams(*sem),
        name=name,
    )(*args, *[w for w, _ in casts])
    return outs[0], tuple(outs[1:])


def _norm_matmul_kernel(x_ref, g_ref, w_ref, o_ref, h_ref):
    @pl.when(pl.program_id(1) == 0)
    def _():
        h_ref[...] = _rms(x_ref[...], g_ref[...]).astype(BF16)

    o_ref[...] = jnp.dot(h_ref[...], w_ref[...], preferred_element_type=F32)


def norm_matmul(x, g, w, casts=()):
    s, d = x.shape
    n = w.shape[1]
    tm, tn = min(ROW_TILE, s), IN_PROJ_TN
    n_col = n // tn
    cast_cols = 1 << (n_col.bit_length() - 1)
    return _mixer_call(
        _norm_matmul_kernel, (s // tm) * cast_cols, lambda i, j: i * cast_cols + jnp.minimum(j, cast_cols - 1),
        grid=(s // tm, n_col),
        in_specs=[
            pl.BlockSpec((tm, d), lambda i, j: (i, 0)),
            pl.BlockSpec((1, d), lambda i, j: (0, 0)),
            pl.BlockSpec((d, tn), lambda i, j: (0, j)),
        ],
        out_spec=pl.BlockSpec((tm, tn), lambda i, j: (i, j)),
        out_shape=jax.ShapeDtypeStruct((s, n), F32),
        scratch_shapes=[pltpu.VMEM((tm, d), BF16)],
        sem=("arbitrary", "arbitrary"), name="norm_in_proj",
        args=(x, g.reshape(1, d), w), casts=casts, cast_when=lambda: pl.program_id(1) < cast_cols)


def _out_proj_kernel(oa_ref, ob_ref, oc_ref, od_ref, w_ref, x_ref, o_ref):
    acc = x_ref[...]
    for idx, r in enumerate((oa_ref, ob_ref, oc_ref, od_ref)):
        acc = acc + jnp.dot(r[...], w_ref[idx * W_MIX:(idx + 1) * W_MIX, :], preferred_element_type=F32)
    o_ref[...] = acc


def out_proj(parts, w, x, casts=()):
    s, d = x.shape
    tm, tn = min(ROW_TILE, s), OUT_PROJ_TN
    n_col = d // tn
    part_spec = pl.BlockSpec((tm, W_MIX), lambda i, j: (i, 0))
    return _mixer_call(
        _out_proj_kernel, (s // tm) * n_col, lambda i, j: i * n_col + j,
        grid=(s // tm, n_col),
        in_specs=[part_spec, part_spec, part_spec, part_spec,
                  pl.BlockSpec((4 * W_MIX, tn), lambda i, j: (0, j)),
                  pl.BlockSpec((tm, tn), lambda i, j: (i, j))],
        out_spec=pl.BlockSpec((tm, tn), lambda i, j: (i, j)),
        out_shape=jax.ShapeDtypeStruct((s, d), F32),
        scratch_shapes=[], sem=("arbitrary", "arbitrary"), name="out_proj",
        args=(*parts, w, x), casts=casts)


def _swiglu_step(h, w1, w3, w2):
    a = jnp.dot(h, w1, preferred_element_type=F32)
    b = jnp.dot(h, w3, preferred_element_type=F32)
    act = (_silu(a) * b).astype(BF16)
    return jnp.dot(act, w2, preferred_element_type=F32)


def _ffn_kernel(x_ref, g_ref, w1_ref, w3_ref, w2_ref, o_ref, h_ref):
    @pl.when(pl.program_id(1) == 0)
    def _():
        xf = x_ref[...]
        h_ref[...] = _rms(xf, g_ref[...]).astype(BF16)
        o_ref[...] = xf

    o_ref[...] += _swiglu_step(h_ref[...], w1_ref[...], w3_ref[...], w2_ref[...])


def ffn(x, g, w1, w3, w2):
    s, d = x.shape
    dff = w1.shape[1]
    tm, tf = min(ROW_TILE, s), FFN_TF
    return pl.pallas_call(
        _ffn_kernel,
        out_shape=jax.ShapeDtypeStruct((s, d), F32),
        grid=(s // tm, dff // tf),
        in_specs=[
            pl.BlockSpec((tm, d), lambda i, f: (i, 0), pipeline_mode=pl.Buffered(1)),
            pl.BlockSpec((1, d), lambda i, f: (0, 0)),
            pl.BlockSpec((d, tf), lambda i, f: (0, f)),
            pl.BlockSpec((d, tf), lambda i, f: (0, f)),
            pl.BlockSpec((tf, d), lambda i, f: (f, 0)),
        ],
        out_specs=pl.BlockSpec((tm, d), lambda i, f: (i, 0), pipeline_mode=pl.Buffered(1)),
        scratch_shapes=[pltpu.VMEM((tm, d), BF16)],
        compiler_params=_cparams("parallel", "arbitrary"),
        name="ffn_swiglu",
    )(x, g.reshape(1, d), w1, w3, w2)


def _router_kernel(x_ref, g_ref, whi_ref, wlo_ref, comb_ref):
    h = _rms(x_ref[...], g_ref[...])
    h_hi = h.astype(BF16)
    h_lo = (h - h_hi.astype(F32)).astype(BF16)
    logits = (jnp.dot(h_hi, whi_ref[...], preferred_element_type=F32)
              + (jnp.dot(h_lo, whi_ref[...], preferred_element_type=F32)
                 + jnp.dot(h_hi, wlo_ref[...], preferred_element_type=F32)))
    lane = lax.broadcasted_iota(jnp.int32, logits.shape, 1)
    neg = jnp.float32(-jnp.inf)
    logits = jnp.where(lane < N_EXPERTS, logits, neg)
    v1 = jnp.max(logits, axis=-1, keepdims=True)
    i1 = jnp.min(jnp.where(logits == v1, lane, LANES), axis=-1, keepdims=True)
    rest = jnp.where(lane == i1, neg, logits)
    v2 = jnp.max(rest, axis=-1, keepdims=True)
    i2 = jnp.min(jnp.where(rest == v2, lane, LANES), axis=-1, keepdims=True)
    e2 = jnp.exp(v2 - v1)
    g1 = 1.0 / (1.0 + e2)
    g2 = e2 / (1.0 + e2)
    comb_ref[...] = (jnp.where(lane == 0, i1.astype(F32), 0.0) + jnp.where(lane == 1, i2.astype(F32), 0.0)
                     + jnp.where(lane == 2, g1, 0.0) + jnp.where(lane == 3, g2, 0.0))


def router(x, g, router_w):
    s, d = x.shape
    tm = min(ROW_TILE, s)
    wr = jnp.zeros((d, LANES), F32).at[:, :N_EXPERTS].set(router_w.astype(F32))
    w_hi = wr.astype(BF16)
    w_lo = (wr - w_hi.astype(F32)).astype(BF16)
    return pl.pallas_call(
        _router_kernel,
        out_shape=jax.ShapeDtypeStruct((s, LANES), F32),
        grid=(s // tm,),
        in_specs=[pl.BlockSpec((tm, d), lambda i: (i, 0)),
                  pl.BlockSpec((1, d), lambda i: (0, 0)),
                  pl.BlockSpec((d, LANES), lambda i: (0, 0)),
                  pl.BlockSpec((d, LANES), lambda i: (0, 0))],
        out_specs=pl.BlockSpec((tm, LANES), lambda i: (i, 0)),
        compiler_params=_cparams("parallel"),
        name="moe_router",
    )(x, g.reshape(1, d), w_hi, w_lo)


def _row_copy(src_hbm, src_row, dst_ref, dst_row, sem):
    return pltpu.make_async_copy(src_hbm.at[pl.ds(src_row, 1)], dst_ref.at[pl.ds(dst_row, 1)], sem)


def _moe_grouped_kernel(te_ref, nu_ref, src_ref, x_hbm, g_ref, w1_ref, w3_ref, w2_ref, o_ref,
                        xbuf, h_ref, sem):
    i = pl.program_id(0)
    f = pl.program_id(1)
    n_used = nu_ref[0]
    used = i < n_used
    tm = xbuf.shape[0]

    def start_gather(tile):
        def issue(r, c):
            _row_copy(x_hbm, src_ref[tile * tm + r], xbuf, r, sem).start()
            return c

        lax.fori_loop(0, tm, issue, 0, unroll=8)

    @pl.when((i == 0) & (f == 0))
    def _():
        start_gather(0)

    @pl.when(jnp.logical_not(used) & (f == 0))
    def _():
        o_ref[...] = jnp.zeros_like(o_ref)

    @pl.when(used & (f == 0))
    def _():
        def wait(r, c):
            _row_copy(x_hbm, 0, xbuf, r, sem).wait()
            return c

        lax.fori_loop(0, tm, wait, 0, unroll=8)
        h_ref[...] = _rms(xbuf[...], g_ref[...]).astype(BF16)
        o_ref[...] = _swiglu_step(h_ref[...], w1_ref[0], w3_ref[0], w2_ref[0])

    @pl.when((f == 1) & (i + 1 < n_used))
    def _():
        start_gather(i + 1)

    @pl.when(used & (f > 0))
    def _():
        o_ref[...] += _swiglu_step(h_ref[...], w1_ref[0], w3_ref[0], w2_ref[0])


def moe_grouped(x, src, g, tile_expert, n_used, w1, w3, w2):
    d = x.shape[1]
    p = src.shape[0]
    dff = w1.shape[2]
    tm, tf = MOE_TM, min(FFN_TF, dff // 2)
    nf = dff // tf

    def f_eff(i, f, nu):
        return jnp.where(i < nu[0], f, nf - 1)

    grid_spec = pltpu.PrefetchScalarGridSpec(
        num_scalar_prefetch=3,
        grid=(p // tm, nf),
        in_specs=[
            pl.BlockSpec(memory_space=pl.ANY),
            pl.BlockSpec((1, d), lambda i, f, te, nu, sr: (0, 0)),
            pl.BlockSpec((1, d, tf), lambda i, f, te, nu, sr: (te[i], 0, f_eff(i, f, nu))),
            pl.BlockSpec((1, d, tf), lambda i, f, te, nu, sr: (te[i], 0, f_eff(i, f, nu))),
            pl.BlockSpec((1, tf, d), lambda i, f, te, nu, sr: (te[i], f_eff(i, f, nu), 0)),
        ],
        out_specs=pl.BlockSpec((tm, d), lambda i, f, te, nu, sr: (i, 0), pipeline_mode=pl.Buffered(1)),
        scratch_shapes=[pltpu.VMEM((tm, d), F32), pltpu.VMEM((tm, d), BF16), pltpu.SemaphoreType.DMA(())],
    )
    return pl.pallas_call(
        _moe_grouped_kernel,
        out_shape=jax.ShapeDtypeStruct((p, d), F32),
        grid_spec=grid_spec,
        compiler_params=_cparams("arbitrary", "arbitrary"),
        name="moe_grouped",
    )(tile_expert, n_used, src, x, g.reshape(1, d), w1, w3, w2)


def _combine_kernel(dest_ref, x_ref, info_ref, y_hbm, *rest, final_norm):
    if final_norm:
        fg_ref, o_ref, buf, sem = rest
    else:
        o_ref, buf, sem = rest
    i = pl.program_id(0)
    n_rows = o_ref.shape[0]
    slot = i % 2

    def start_gather(step, sl):
        def issue(r, c):
            for k in range(TOP_K):
                _row_copy(y_hbm, dest_ref[TOP_K * (step * n_rows + r) + k], buf.at[sl, k], r, sem.at[sl]).start()
            return c

        lax.fori_loop(0, n_rows, issue, 0, unroll=4)

    @pl.when(i == 0)
    def _():
        start_gather(0, 0)

    @pl.when(i + 1 < pl.num_programs(0))
    def _():
        start_gather(i + 1, 1 - slot)

    def wait(r, c):
        for k in range(TOP_K):
            _row_copy(y_hbm, 0, buf.at[slot, k], r, sem.at[slot]).wait()
        return c

    lax.fori_loop(0, n_rows, wait, 0, unroll=4)
    acc = x_ref[...]
    info = info_ref[...]
    for k in range(TOP_K):
        acc = acc + info[:, TOP_K + k:TOP_K + k + 1] * buf[slot, k]
    o_ref[...] = _rms(acc, fg_ref[...]) if final_norm else acc


def combine(x, y, dest, info, final_g=None):
    s, d = x.shape
    rt = COMBINE_ROWS
    final_norm = final_g is not None
    in_specs = [pl.BlockSpec((rt, d), lambda i, dr: (i, 0)), pl.BlockSpec((rt, LANES), lambda i, dr: (i, 0)),
                pl.BlockSpec(memory_space=pl.ANY)]
    args = [x, info, y]
    if final_norm:
        in_specs.append(pl.BlockSpec((1, d), lambda i, dr: (0, 0)))
        args.append(final_g.astype(F32).reshape(1, d))
    grid_spec = pltpu.PrefetchScalarGridSpec(
        num_scalar_prefetch=1,
        grid=(s // rt,),
        in_specs=in_specs,
        out_specs=pl.BlockSpec((rt, d), lambda i, dr: (i, 0)),
        scratch_shapes=[pltpu.VMEM((2, TOP_K, rt, d), F32), pltpu.SemaphoreType.DMA((2,))],
    )
    return pl.pallas_call(
        functools.partial(_combine_kernel, final_norm=final_norm),
        out_shape=jax.ShapeDtypeStruct((s, d), F32),
        grid_spec=grid_spec,
        compiler_params=_cparams("arbitrary"),
        name="moe_combine",
    )(dest, *args)


def _routing_tables(experts, tm, n_tiles):
    n_assign = experts.size
    e_flat = experts.reshape(n_assign)
    onehot = (e_flat[:, None] == jnp.arange(N_EXPERTS, dtype=jnp.int32)[None, :]).astype(jnp.int32)
    csum = jnp.cumsum(onehot, axis=0)
    rank = jnp.sum(csum * onehot, axis=1) - 1
    counts = csum[-1]
    padded = ((counts + tm - 1) // tm) * tm
    seg_end = jnp.cumsum(padded)
    seg_start = seg_end - padded
    dest = (seg_start[e_flat] + rank).astype(jnp.int32)
    p = n_tiles * tm
    src = jnp.zeros((p,), jnp.int32).at[dest].set(jnp.arange(n_assign, dtype=jnp.int32) // TOP_K)
    n_used = (seg_end[-1] // tm).astype(jnp.int32)
    tile_start = jnp.arange(n_tiles, dtype=jnp.int32) * tm
    tile_e = jnp.sum((tile_start[:, None] >= seg_end[None, :]).astype(jnp.int32), axis=1)
    tile_e = jnp.minimum(tile_e, N_EXPERTS - 1)
    last_e = tile_e[n_used - 1]
    tile_e = jnp.where(jnp.arange(n_tiles) < n_used, tile_e, last_e).astype(jnp.int32)
    return src, dest, tile_e, n_used.reshape(1)


def moe(x, g, router_w, w1, w3, w2, final_g=None):
    s, _ = x.shape
    info = router(x, g, router_w)
    experts = info[:, 0:TOP_K].astype(jnp.int32)
    tm = MOE_TM
    n_tiles = (TOP_K * s) // tm + N_EXPERTS
    src, dest, tile_e, n_used = _routing_tables(experts, tm, n_tiles)
    y = moe_grouped(x, src, g, tile_e, n_used, w1.astype(BF16), w3.astype(BF16), w2.astype(BF16))
    return combine(x, y, dest, info, final_g)


def _final_norm_kernel(x_ref, g_ref, o_ref):
    o_ref[...] = _rms(x_ref[...], g_ref[...])


def final_norm(x, g):
    s, d = x.shape
    tm = min(ROW_TILE, s)
    return pl.pallas_call(
        _final_norm_kernel,
        out_shape=jax.ShapeDtypeStruct((s, d), F32),
        grid=(s // tm,),
        in_specs=[pl.BlockSpec((tm, d), lambda i: (i, 0)), pl.BlockSpec((1, d), lambda i: (0, 0))],
        out_specs=pl.BlockSpec((tm, d), lambda i: (i, 0)),
        compiler_params=_cparams("parallel"),
        name="final_norm",
    )(x, g.reshape(1, d))


def _hgrn2_kernel(zq_ref, zf_ref, zi_ref, zg_ref, lb_ref, ng_ref, tri_ref, o_ref,
                  st_ref, q_s, k_s, v_s, cum_s, o_s, dec_s, qd_s, kd_s):
    c_len = HGRN_C
    heads, t_len, hd = q_s.shape

    @pl.when(pl.program_id(0) == 0)
    def _():
        st_ref[...] = jnp.zeros_like(st_ref)

    lb = lb_ref[...]
    f = lb + (1.0 - lb) * _sigmoid(zf_ref[...])
    lf = jnp.log(f)
    lf_hi = lf.astype(BF16)
    lf_lo = (lf - lf_hi.astype(F32)).astype(BF16)
    tri = tri_ref[...]
    cum = (jnp.dot(tri, lf_hi, preferred_element_type=F32) + jnp.dot(tri, lf_lo, preferred_element_type=F32))
    cum3 = cum.reshape(t_len // c_len, c_len, heads * hd)
    last3 = cum3[:, c_len - 1:c_len, :]
    last = jnp.broadcast_to(last3, cum3.shape).reshape(t_len, heads * hd)
    dec = jnp.exp(last3.reshape(t_len // c_len, heads * hd))
    q = _silu(zq_ref[...])
    k = 1.0 - f
    qd = (q * jnp.exp(cum)).astype(BF16)
    kd = (k * jnp.exp(last - cum)).astype(BF16)
    v = zi_ref[...]
    cum2 = cum * math.log2(math.e)
    for h in range(heads):
        sl = slice(h * hd, (h + 1) * hd)
        q_s[h] = q[:, sl]
        k_s[h] = k[:, sl]
        v_s[h] = v[:, sl]
        cum_s[h] = cum2[:, sl]
        dec_s[h] = dec[:, sl]
        qd_s[h] = qd[:, sl]
        kd_s[h] = kd[:, sl]

    half = c_len // 2
    row = lax.broadcasted_iota(jnp.int32, (half, hd), 0)
    lane = lax.broadcasted_iota(jnp.int32, (half, hd), 1)

    def body(c, carry):
        r0 = pl.multiple_of(c * c_len, c_len)
        rows = pl.ds(r0, c_len)
        for h in range(heads):
            q_c = q_s[h, rows, :]
            k_c = k_s[h, rows, :]
            cm = cum_s[h, rows, :]
            m_lo = jnp.zeros((half, hd), F32)
            m_hi = jnp.zeros((half, hd), F32)
            for s in range(c_len):
                ks = k_c[s:s + 1, :]
                cs = cm[s:s + 1, :]
                if s < half:
                    w = q_c[:half] * (ks * jnp.exp2(jnp.minimum(cm[:half] - cs, 0.0)))
                    m_lo = jnp.where(lane == s, jnp.sum(w, axis=1, keepdims=True), m_lo)
                w = q_c[half:] * (ks * jnp.exp2(jnp.minimum(cm[half:] - cs, 0.0)))
                m_hi = jnp.where(lane == s, jnp.sum(w, axis=1, keepdims=True), m_hi)
            m_lo = jnp.where(row >= lane, m_lo, 0.0)
            m_hi = jnp.where(row + half >= lane, m_hi, 0.0)
            scores = jnp.concatenate([m_lo, m_hi], axis=0)[:, :c_len].astype(BF16)
            v_c = v_s[h, rows, :].astype(BF16)
            st = st_ref[h]
            o = (jnp.dot(scores, v_c, preferred_element_type=F32)
                 + lax.dot_general(qd_s[h, rows, :], st.astype(BF16), (((1,), (1,)), ((), ())),
                                   preferred_element_type=F32))
            o_s[h, rows, :] = o
            upd = lax.dot_general(v_c, kd_s[h, rows, :], (((0,), (0,)), ((), ())),
                                  preferred_element_type=F32)
            st_ref[h] = st * dec_s[h, pl.ds(c, 1), :] + upd
        return carry

    lax.fori_loop(0, t_len // c_len, body, 0)
    ng = ng_ref[...]
    sg = _silu(zg_ref[...])
    for h in range(heads):
        sl = slice(h * hd, (h + 1) * hd)
        o = o_s[h]
        o = o * lax.rsqrt(jnp.mean(o * o, axis=-1, keepdims=True) + EPS)
        o_ref[:, sl] = (o * ng[:, sl] * sg[:, sl]).astype(BF16)


def hgrn2(z, col0, lb, norm_g, casts=()):
    s = z.shape[0]
    t_len = min(HGRN_T, s)
    heads = W_MIX // A_HEAD_DIM
    cb = col0 // W_MIX
    r = jnp.arange(t_len)
    same = r[:, None] // HGRN_C == r[None, :] // HGRN_C
    tri = (same & (r[None, :] <= r[:, None])).astype(BF16)

    def zspec(k):
        return pl.BlockSpec((t_len, W_MIX), lambda i: (i, cb + k))

    vec = pl.BlockSpec((1, W_MIX), lambda i: (0, 0))
    sq = pl.BlockSpec((t_len, t_len), lambda i: (0, 0))
    per_head = lambda dt: pltpu.VMEM((heads, t_len, A_HEAD_DIM), dt)
    return _mixer_call(
        _hgrn2_kernel, s // t_len, lambda i: i,
        grid=(s // t_len,),
        in_specs=[zspec(0), zspec(1), zspec(2), zspec(3), vec, vec, sq],
        out_spec=pl.BlockSpec((t_len, W_MIX), lambda i: (i, 0)),
        out_shape=jax.ShapeDtypeStruct((s, W_MIX), BF16),
        scratch_shapes=[pltpu.VMEM((heads, A_HEAD_DIM, A_HEAD_DIM), F32)]
        + [per_head(F32)] * 5 + [pltpu.VMEM((heads, t_len // HGRN_C, A_HEAD_DIM), F32)] + [per_head(BF16)] * 2,
        sem=("arbitrary",), name="hgrn2",
        args=(z, z, z, z, lb.reshape(1, W_MIX), norm_g.reshape(1, W_MIX), tri), casts=casts)


def _s5_kernel(u_ref, wb_ref, pw_ref, wc_ref, d_ref, gw_ref, gb_ref, o_ref,
               carry_ref, bu_s, y_s):
    i = pl.program_id(0)
    j = pl.program_id(1)
    n_slab = pl.num_programs(1)
    t_len = bu_s.shape[0]
    half = bu_s.shape[1] // 2

    @pl.when(i == 0)
    def _():
        carry_ref[j] = jnp.zeros((SUBLANES, 2 * half), F32)

    u = u_ref[...]
    bu_s[...] = jnp.dot(u.astype(BF16), wb_ref[0], preferred_element_type=F32)
    p8_re, p8_im = pw_ref[0, 0:8, :half], pw_ref[0, 0:8, half:]

    def body(b, carry):
        c_re, c_im = carry
        r0 = pl.multiple_of(b * SUBLANES, SUBLANES)
        blk = bu_s[pl.ds(r0, SUBLANES), :]
        x_re, x_im = blk[:, :half], blk[:, half:]
        for step, k in enumerate((1, 2, 4)):
            a_re = pw_ref[0, 8 * (step + 1):8 * (step + 2), :half]
            a_im = pw_ref[0, 8 * (step + 1):8 * (step + 2), half:]
            s_re = pltpu.roll(x_re, k, 0)
            s_im = pltpu.roll(x_im, k, 0)
            x_re, x_im = (x_re + a_re * s_re - a_im * s_im,
                          x_im + a_re * s_im + a_im * s_re)
        x_re, x_im = (x_re + p8_re * c_re - p8_im * c_im,
                      x_im + p8_re * c_im + p8_im * c_re)
        bu_s[pl.ds(r0, SUBLANES), :] = jnp.concatenate([x_re, x_im], axis=1)
        n_re = jnp.broadcast_to(x_re[SUBLANES - 1:SUBLANES, :], x_re.shape)
        n_im = jnp.broadcast_to(x_im[SUBLANES - 1:SUBLANES, :], x_im.shape)
        return n_re, n_im

    c0 = carry_ref[j]
    c_re, c_im = lax.fori_loop(0, t_len // SUBLANES, body, (c0[:, :half], c0[:, half:]))
    carry_ref[j] = jnp.concatenate([c_re, c_im], axis=1)

    y = jnp.dot(bu_s[...].astype(BF16), wc_ref[0], preferred_element_type=F32)
    y = _gelu_tanh(y + d_ref[0] * u)
    y_s[j] = y

    @pl.when(j == n_slab - 1)
    def _():
        yf = jnp.concatenate([y_s[s] for s in range(y_s.shape[0])], axis=1)
        gate = jnp.dot(yf.astype(BF16), gw_ref[...], preferred_element_type=F32) + gb_ref[...]
        o_ref[...] = (yf * _sigmoid(gate)).astype(BF16)


def s5(z, col0, lam_re, lam_im, log_dt, b_re, b_im, c_re, c_im, d_skip, glu_w, glu_b, casts=()):
    s = z.shape[0]
    t_len = min(S5_T, s)
    groups, n_state = lam_re.shape
    gps = S5_SLAB // S5_GROUP
    n_slab = groups // gps
    half = gps * n_state
    cb = col0 // S5_SLAB
    lam_re = lam_re.astype(F32)
    lam_im = lam_im.astype(F32)
    dt = jnp.exp(log_dt.astype(F32))[:, None]
    mag = jnp.exp(lam_re * dt)
    ang = lam_im * dt
    ab_re = mag * jnp.cos(ang)
    ab_im = mag * jnp.sin(ang)
    den = lam_re * lam_re + lam_im * lam_im
    num_re = ab_re - 1.0
    coef_re = (num_re * lam_re + ab_im * lam_im) / den
    coef_im = (ab_im * lam_re - num_re * lam_im) / den
    br = b_re.astype(F32)
    bi = b_im.astype(F32)
    bb_re = coef_re[..., None] * br - coef_im[..., None] * bi
    bb_im = coef_re[..., None] * bi + coef_im[..., None] * br
    eye = jnp.eye(gps, dtype=F32)

    def blockdiag_in(bb):
        t = bb.reshape(n_slab, gps, n_state, S5_GROUP)
        return jnp.einsum('sgnp,gh->sgphn', t, eye).reshape(n_slab, gps * S5_GROUP, gps * n_state)

    wb = jnp.concatenate([blockdiag_in(bb_re), blockdiag_in(bb_im)], axis=-1).astype(BF16)

    def blockdiag_out(cc):
        t = cc.reshape(n_slab, gps, S5_GROUP, n_state)
        return jnp.einsum('sgpn,gh->sgnhp', t, eye).reshape(n_slab, gps * n_state, gps * S5_GROUP)

    wc = jnp.concatenate([blockdiag_out(c_re.astype(F32)), -blockdiag_out(c_im.astype(F32))],
                         axis=1).astype(BF16)
    r8 = jnp.arange(SUBLANES)
    expo = jnp.concatenate([r8 + 1.0] + [jnp.full((SUBLANES,), float(k)) for k in (1, 2, 4)]).astype(F32)
    keep = jnp.concatenate([jnp.ones((SUBLANES,), F32)] + [(r8 >= k).astype(F32) for k in (1, 2, 4)])
    expo = expo[:, None, None]
    p_mag = jnp.exp(expo * (lam_re * dt)[None]) * keep[:, None, None]
    p_re = (p_mag * jnp.cos(expo * ang[None])).reshape(S5_PW_ROWS, n_slab, half)
    p_im = (p_mag * jnp.sin(expo * ang[None])).reshape(S5_PW_ROWS, n_slab, half)
    pw = jnp.moveaxis(jnp.concatenate([p_re, p_im], axis=-1), 1, 0)

    return _mixer_call(
        _s5_kernel, (s // t_len) * n_slab, lambda i, j: i * n_slab + j,
        grid=(s // t_len, n_slab),
        in_specs=[
            pl.BlockSpec((t_len, S5_SLAB), lambda i, j: (i, cb + j)),
            pl.BlockSpec((1, S5_SLAB, 2 * half), lambda i, j: (j, 0, 0)),
            pl.BlockSpec((1, S5_PW_ROWS, 2 * half), lambda i, j: (j, 0, 0)),
            pl.BlockSpec((1, 2 * half, S5_SLAB), lambda i, j: (j, 0, 0)),
            pl.BlockSpec((1, 1, S5_SLAB), lambda i, j: (j, 0, 0)),
            pl.BlockSpec((W_MIX, W_MIX), lambda i, j: (0, 0)),
            pl.BlockSpec((1, W_MIX), lambda i, j: (0, 0)),
        ],
        out_spec=pl.BlockSpec((t_len, W_MIX), lambda i, j: (i, 0)),
        out_shape=jax.ShapeDtypeStruct((s, W_MIX), BF16),
        scratch_shapes=[pltpu.VMEM((n_slab, SUBLANES, 2 * half), F32),
                        pltpu.VMEM((t_len, 2 * half), F32),
                        pltpu.VMEM((n_slab, t_len, S5_SLAB), F32)],
        sem=("arbitrary", "arbitrary"), name="s5",
        args=(z, wb, pw, wc, d_skip.astype(F32).reshape(n_slab, 1, S5_SLAB), glu_w.astype(BF16),
              glu_b.astype(F32).reshape(1, W_MIX)), casts=casts)


def _retention_kernel(zq_ref, zk_ref, zv_ref, zg_ref, cos_ref, sin_ref, dmat_ref, qdec_ref, kdec_ref,
                      cdec_ref, ng_ref, o_ref, st_ref):
    @pl.when(pl.program_id(0) == 0)
    def _():
        st_ref[...] = jnp.zeros_like(st_ref)

    heads, hd, _ = st_ref.shape
    half = hd // 2
    cos = cos_ref[...]
    sin = sin_ref[...]

    def rope(t):
        t1, t2 = t[:, :half], t[:, half:]
        return jnp.concatenate([t1 * cos - t2 * sin, t1 * sin + t2 * cos], axis=1)

    for h in range(heads):
        sl = slice(h * hd, (h + 1) * hd)
        q = rope(zq_ref[:, sl])
        k = rope(zk_ref[:, sl]) * hd ** -0.5
        v = zv_ref[:, sl].astype(BF16)
        qb = q.astype(BF16)
        scores = lax.dot_general(qb, k.astype(BF16), (((1,), (1,)), ((), ())),
                                 preferred_element_type=F32) * dmat_ref[h]
        st = st_ref[h]
        o = (jnp.dot(scores.astype(BF16), v, preferred_element_type=F32)
             + jnp.dot(qb, st.astype(BF16), preferred_element_type=F32) * qdec_ref[h])
        kd = (k * kdec_ref[h]).astype(BF16)
        st_ref[h] = cdec_ref[h] * st + lax.dot_general(kd, v, (((0,), (0,)), ((), ())),
                                                       preferred_element_type=F32)
        mu = jnp.mean(o, axis=-1, keepdims=True)
        oc = o - mu
        var = jnp.mean(oc * oc, axis=-1, keepdims=True)
        o = oc * lax.rsqrt(var + EPS)
        o_ref[:, sl] = (o * ng_ref[:, sl] * _silu(zg_ref[:, sl])).astype(BF16)


def retention(z, col0, norm_g, casts=()):
    s = z.shape[0]
    t_len = min(RET_T, s)
    hd = W_MIX // C_HEADS
    pos = jnp.arange(s, dtype=F32)
    inv_freq = ROPE_THETA ** (-jnp.arange(0, hd, 2, dtype=F32) / hd)
    ang = pos[:, None] * inv_freq[None, :]
    cos = jnp.cos(ang)
    sin = jnp.sin(ang)
    log_gamma = jnp.log(1.0 - 2.0 ** (-5.0 - jnp.arange(C_HEADS, dtype=F32)))
    idx = jnp.arange(t_len, dtype=F32)
    rel = idx[:, None] - idx[None, :]
    dmat = jnp.where(rel[None] >= 0, jnp.exp(jnp.maximum(rel, 0.0)[None] * log_gamma[:, None, None]), 0.0)
    qdec = jnp.exp((idx + 1.0)[None, :] * log_gamma[:, None])[..., None]
    kdec = jnp.exp((t_len - 1.0 - idx)[None, :] * log_gamma[:, None])[..., None]
    cdec = jnp.broadcast_to(jnp.exp(t_len * log_gamma)[:, None, None], (C_HEADS, 1, hd))

    cb = col0 // W_MIX

    def zspec(k):
        return pl.BlockSpec((t_len, W_MIX), lambda i: (i, cb + k))

    tab = pl.BlockSpec((t_len, hd // 2), lambda i: (i, 0))
    full = lambda shape: pl.BlockSpec(shape, lambda i: (0,) * len(shape))
    return _mixer_call(
        _retention_kernel, s // t_len, lambda i: i,
        grid=(s // t_len,),
        in_specs=[zspec(0), zspec(1), zspec(2), zspec(3), tab, tab,
                  full((C_HEADS, t_len, t_len)), full((C_HEADS, t_len, 1)), full((C_HEADS, t_len, 1)),
                  full((C_HEADS, 1, hd)), full((1, W_MIX))],
        out_spec=pl.BlockSpec((t_len, W_MIX), lambda i: (i, 0)),
        out_shape=jax.ShapeDtypeStruct((s, W_MIX), BF16),
        scratch_shapes=[pltpu.VMEM((C_HEADS, hd, hd), F32)],
        sem=("arbitrary",), name="retention",
        args=(z, z, z, z, cos, sin, dmat, qdec, kdec, cdec, norm_g.reshape(1, W_MIX)), casts=casts)


def _rglru_kernel(zg_ref, zx_ref, cw_ref, cb_ref, wa_ref, ba_ref, wx_ref, bx_ref, sp_ref, o_ref,
                  xbuf, h_ref, a_s, u_s):
    t_len = zx_ref.shape[0]

    @pl.when(pl.program_id(0) == 0)
    def _():
        xbuf[0:SUBLANES, :] = jnp.zeros((SUBLANES, W_MIX), F32)
        h_ref[...] = jnp.zeros_like(h_ref)

    xbuf[SUBLANES:, :] = zx_ref[...]
    xc = cb_ref[...]
    for tap in range(CONV_WIDTH):
        off = SUBLANES - (CONV_WIDTH - 1) + tap
        xc = xc + xbuf[off:off + t_len, :] * cw_ref[tap:tap + 1, :]
    xbuf[0:SUBLANES, :] = xbuf[t_len:t_len + SUBLANES, :]
    xcb = xc.astype(BF16)
    n_blk = W_MIX // D_BLOCK
    pre_r = jnp.concatenate(
        [jnp.dot(xcb[:, b * D_BLOCK:(b + 1) * D_BLOCK], wa_ref[b], preferred_element_type=F32)
         for b in range(n_blk)], axis=1)
    pre_i = jnp.concatenate(
        [jnp.dot(xcb[:, b * D_BLOCK:(b + 1) * D_BLOCK], wx_ref[b], preferred_element_type=F32)
         for b in range(n_blk)], axis=1)
    r = _sigmoid(pre_r + ba_ref[...])
    gi = _sigmoid(pre_i + bx_ref[...])
    log_a = -RG_C * r * sp_ref[...]
    a = jnp.exp(log_a)
    a_s[...] = a
    u_s[...] = jnp.sqrt(1.0 - a * a) * (gi * xc)
    row = lax.broadcasted_iota(jnp.int32, (SUBLANES, W_MIX), 0)

    def body(b, h):
        r0 = pl.multiple_of(b * SUBLANES, SUBLANES)
        aa = a_s[pl.ds(r0, SUBLANES), :]
        uu = u_s[pl.ds(r0, SUBLANES), :]
        for k in (1, 2, 4):
            us = jnp.where(row >= k, pltpu.roll(uu, k, 0), 0.0)
            as_ = jnp.where(row >= k, pltpu.roll(aa, k, 0), 1.0)
            uu = uu + aa * us
            aa = aa * as_
        hh = uu + aa * h
        u_s[pl.ds(r0, SUBLANES), :] = hh
        return jnp.broadcast_to(hh[SUBLANES - 1:SUBLANES, :], hh.shape)

    h_ref[...] = lax.fori_loop(0, t_len // SUBLANES, body, h_ref[...])
    o_ref[...] = (_gelu_tanh(zg_ref[...]) * u_s[...]).astype(BF16)


def rglru(z, col0, conv_w, conv_b, w_a, b_a, w_x, b_x, lam, casts=()):
    s = z.shape[0]
    t_len = min(RG_T, s)
    cb = col0 // W_MIX
    sp = jax.nn.softplus(-lam.astype(F32)).reshape(1, W_MIX)
    row = lambda a: a.astype(F32).reshape(1, W_MIX)
    full = lambda shape: pl.BlockSpec(shape, lambda i: (0,) * len(shape))
    n_blk = W_MIX // D_BLOCK
    return _mixer_call(
        _rglru_kernel, s // t_len, lambda i: i,
        grid=(s // t_len,),
        in_specs=[pl.BlockSpec((t_len, W_MIX), lambda i: (i, cb)),
                  pl.BlockSpec((t_len, W_MIX), lambda i: (i, cb + 1)),
                  full((CONV_WIDTH, W_MIX)), full((1, W_MIX)),
                  full((n_blk, D_BLOCK, D_BLOCK)), full((1, W_MIX)),
                  full((n_blk, D_BLOCK, D_BLOCK)), full((1, W_MIX)), full((1, W_MIX))],
        out_spec=pl.BlockSpec((t_len, W_MIX), lambda i: (i, 0)),
        out_shape=jax.ShapeDtypeStruct((s, W_MIX), BF16),
        scratch_shapes=[pltpu.VMEM((t_len + SUBLANES, W_MIX), F32),
                        pltpu.VMEM((SUBLANES, W_MIX), F32),
                        pltpu.VMEM((t_len, W_MIX), F32),
                        pltpu.VMEM((t_len, W_MIX), F32)],
        sem=("arbitrary",), name="rglru",
        args=(z, z, conv_w.astype(F32), row(conv_b), w_a.astype(BF16), row(b_a), w_x.astype(BF16), row(b_x), sp),
        casts=casts)


def kernel(x, norm_mix_g, norm_ffn_g, final_norm_g, w_in, w_out, hgrn_lb_logits, hgrn_norm_g, s5_lambda_re, s5_lambda_im, s5_log_dt, s5_b_re, s5_b_im, s5_c_re, s5_c_im, s5_d, s5_glu_w, s5_glu_b, ret_norm_g, rg_conv_w, rg_conv_b, rg_w_a, rg_b_a, rg_w_x, rg_b_x, rg_lambda, ffn_w1, ffn_w3, ffn_w2, router_w, moe_w1, moe_w3, moe_w2):
    b_, s_, d_ = x.shape
    depth = w_in.shape[0]
    xs = x.reshape(b_ * s_, d_).astype(F32)
    lb_p = jax.nn.softmax(hgrn_lb_logits.astype(F32), axis=0)
    lb_all = jnp.cumsum(lb_p, axis=0) - lb_p[0]
    col_a, col_b, col_c, col_d = 0, 4 * W_MIX, 5 * W_MIX, 9 * W_MIX
    w_in_b = w_in[0].astype(BF16)
    for layer in range(depth):
        m = layer // 2
        dense = layer % 2 == 0
        ch = (ffn_w1, ffn_w3, ffn_w2) if dense else (moe_w1, moe_w3, moe_w2)
        nxt = ((w_in, layer + 1),) if layer + 1 < depth else ()
        z, (w2_b,) = norm_matmul(xs, norm_mix_g[layer], w_in_b, casts=((ch[2], m),))
        o_a, (w1_b,) = hgrn2(z, col_a, lb_all[layer], hgrn_norm_g[layer], casts=((ch[0], m),))
        o_b, (w3_b,) = s5(z, col_b, s5_lambda_re[layer], s5_lambda_im[layer], s5_log_dt[layer],
                          s5_b_re[layer], s5_b_im[layer], s5_c_re[layer], s5_c_im[layer],
                          s5_d[layer], s5_glu_w[layer], s5_glu_b[layer], casts=((ch[1], m),))
        o_c, _ = retention(z, col_c, ret_norm_g[layer])
        o_d, rest = rglru(z, col_d, rg_conv_w[layer], rg_conv_b[layer], rg_w_a[layer], rg_b_a[layer],
                          rg_w_x[layer], rg_b_x[layer], rg_lambda[layer], casts=((w_out, layer),) + nxt)
        xs, _ = out_proj((o_a, o_b, o_c, o_d), rest[0], xs)
        if nxt:
            w_in_b = rest[1]
        if dense:
            xs = ffn(xs, norm_ffn_g[layer], w1_b, w3_b, w2_b)
        else:
            last = layer == depth - 1
            xs = moe(xs, norm_ffn_g[layer], router_w[m], w1_b, w3_b, w2_b, final_norm_g if last else None)
            if last:
                return xs.reshape(b_, s_, d_)
    return final_norm(xs, final_norm_g).reshape(b_, s_, d_)
```

```python
import functools
import math

import jax
import jax.numpy as jnp
from jax import lax
from jax.experimental import pallas as pl
from jax.experimental.pallas import tpu as pltpu

F32 = jnp.float32
BF16 = jnp.bfloat16
EPS = 1e-6

V7X_VMEM_BYTES = 64 * 1024 * 1024
VMEM_LIMIT_BYTES = V7X_VMEM_BYTES - 4 * 1024 * 1024
SUBLANES = 8
LANES = 128

A_HEAD_DIM = 128
S5_GROUP = 16
S5_STATE = 64
C_HEADS = 4
ROPE_THETA = 10000.0
D_BLOCK = 128
CONV_WIDTH = 4
RG_C = 8.0
N_EXPERTS = 8
TOP_K = 2
W_MIX = 1024

ROW_TILE = 512
IN_PROJ_TN = 1024
OUT_PROJ_TN = 1024
FFN_TF = 512
MOE_TM = 512
COMBINE_ROWS = 256
HGRN_T = 256
HGRN_C = 16
RET_T = 256
S5_T = 512
S5_SLAB = 128
S5_PW_ROWS = 4 * SUBLANES
RG_T = 512
CAST_ROW_ALIGN = 16
CAST_INLINE_ELEMS = 256 * 1024


def _cparams(*sem):
    return pltpu.CompilerParams(dimension_semantics=sem, vmem_limit_bytes=VMEM_LIMIT_BYTES)


def _rms(xf, g):
    return xf * lax.rsqrt(jnp.mean(xf * xf, axis=-1, keepdims=True) + EPS) * g


def _sigmoid(x):
    return 0.5 + 0.5 * jnp.tanh(0.5 * x)


def _silu(x):
    h = 0.5 * x
    return h + h * jnp.tanh(h)


def _gelu_tanh(x):
    c = math.sqrt(2.0 / math.pi)
    return 0.5 * x * (1.0 + jnp.tanh(c * (x + 0.044715 * (x * x * x))))


def _cast_specs(casts, n_steps, step_index):
    in_specs, out_specs, shapes = [], [], []
    for w, lead in casts:
        *mid, rows, cols = w.shape[1:]
        per = n_steps // math.prod(mid)
        assert per * math.prod(mid) == n_steps, (w.shape, n_steps)
        nr = next(n for n in range(per, 0, -1)
                  if per % n == 0 and rows % (n * CAST_ROW_ALIGN) == 0 and cols % (per // n * LANES) == 0)
        nc = per // nr
        blk = (1,) * len(mid) + (rows // nr, cols // nc)

        def pos(*g, mid=tuple(mid), nr=nr, nc=nc):
            t = step_index(*g)
            idx = [(t // nc) % nr, t % nc]
            t = t // (nr * nc)
            for size in reversed(mid):
                idx.insert(0, t % size)
                t = t // size
            return tuple(idx)

        in_specs.append(pl.BlockSpec((1,) + blk, lambda *g, pos=pos, lead=lead: (lead,) + pos(*g)))
        out_specs.append(pl.BlockSpec(blk, pos))
        shapes.append(jax.ShapeDtypeStruct(w.shape[1:], BF16))
    return in_specs, out_specs, shapes


def _cast_blocks(src_refs, dst_refs):
    for src, dst in zip(src_refs, dst_refs):
        rows, cols = dst.shape[-2:]
        lead = (0,) * (len(dst.shape) - 2)
        if rows * cols <= CAST_INLINE_ELEMS:
            dst[...] = src[0].astype(BF16)
        else:
            def body(c, carry, src=src, dst=dst, lead=lead):
                sl = pl.ds(pl.multiple_of(c * CAST_ROW_ALIGN, CAST_ROW_ALIGN), CAST_ROW_ALIGN)
                dst[lead + (sl, slice(None))] = src[(0,) + lead + (sl, slice(None))].astype(BF16)
                return carry

            lax.fori_loop(0, rows // CAST_ROW_ALIGN, body, 0)


def _with_casts(body, n_in, n_cast, cast_when):
    def kern(*refs):
        ins = refs[:n_in]
        cast_in = refs[n_in:n_in + n_cast]
        out = refs[n_in + n_cast]
        cast_out = refs[n_in + n_cast + 1:n_in + 2 * n_cast + 1]
        scratch = refs[n_in + 2 * n_cast + 1:]
        body(*ins, out, *scratch)
        if cast_when is None or not n_cast:
            _cast_blocks(cast_in, cast_out)
        else:
            pl.when(cast_when())(lambda: _cast_blocks(cast_in, cast_out))

    return kern


def _mixer_call(body, n_steps, step_index, grid, in_specs, out_spec, out_shape, scratch_shapes, sem, name,
                args, casts, cast_when=None):
    cin, cout, cshapes = _cast_specs(casts, n_steps, step_index)
    outs = pl.pallas_call(
        _with_casts(body, len(in_specs), len(casts), cast_when),
        out_shape=[out_shape] + cshapes,
        grid=grid,
        in_specs=list(in_specs) + cin,
        out_specs=[out_spec] + cout,
        scratch_shapes=scratch_shapes,
        compiler_params=_cparams(*sem),
        name=name,
    )(*args, *[w for w, _ in casts])
    return outs[0], tuple(outs[1:])


def _norm_matmul_kernel(x_ref, g_ref, w_ref, o_ref, h_ref):
    @pl.when(pl.program_id(1) == 0)
    def _():
        h_ref[...] = _rms(x_ref[...], g_ref[...]).astype(BF16)

    o_ref[...] = jnp.dot(h_ref[...], w_ref[...], preferred_element_type=F32)


def norm_matmul(x, g, w, casts=()):
    s, d = x.shape
    n = w.shape[1]
    tm, tn = min(ROW_TILE, s), IN_PROJ_TN
    n_col = n // tn
    cast_cols = 1 << (n_col.bit_length() - 1)
    return _mixer_call(
        _norm_matmul_kernel, (s // tm) * cast_cols, lambda i, j: i * cast_cols + jnp.minimum(j, cast_cols - 1),
        grid=(s // tm, n_col),
        in_specs=[
            pl.BlockSpec((tm, d), lambda i, j: (i, 0)),
            pl.BlockSpec((1, d), lambda i, j: (0, 0)),
            pl.BlockSpec((d, tn), lambda i, j: (0, j)),
        ],
        out_spec=pl.BlockSpec((tm, tn), lambda i, j: (i, j)),
        out_shape=jax.ShapeDtypeStruct((s, n), F32),
        scratch_shapes=[pltpu.VMEM((tm, d), BF16)],
        sem=("arbitrary", "arbitrary"), name="norm_in_proj",
        args=(x, g.reshape(1, d), w), casts=casts, cast_when=lambda: pl.program_id(1) < cast_cols)


def _out_proj_kernel(oa_ref, ob_ref, oc_ref, od_ref, w_ref, x_ref, o_ref):
    acc = x_ref[...]
    for idx, r in enumerate((oa_ref, ob_ref, oc_ref, od_ref)):
        acc = acc + jnp.dot(r[...], w_ref[idx * W_MIX:(idx + 1) * W_MIX, :], preferred_element_type=F32)
    o_ref[...] = acc


def out_proj(parts, w, x, casts=()):
    s, d = x.shape
    tm, tn = min(ROW_TILE, s), OUT_PROJ_TN
    n_col = d // tn
    part_spec = pl.BlockSpec((tm, W_MIX), lambda i, j: (i, 0))
    return _mixer_call(
        _out_proj_kernel, (s // tm) * n_col, lambda i, j: i * n_col + j,
        grid=(s // tm, n_col),
        in_specs=[part_spec, part_spec, part_spec, part_spec,
                  pl.BlockSpec((4 * W_MIX, tn), lambda i, j: (0, j)),
                  pl.BlockSpec((tm, tn), lambda i, j: (i, j))],
        out_spec=pl.BlockSpec((tm, tn), lambda i, j: (i, j)),
        out_shape=jax.ShapeDtypeStruct((s, d), F32),
        scratch_shapes=[], sem=("arbitrary", "arbitrary"), name="out_proj",
        args=(*parts, w, x), casts=casts)


def _swiglu_step(h, w1, w3, w2):
    a = jnp.dot(h, w1, preferred_element_type=F32)
    b = jnp.dot(h, w3, preferred_element_type=F32)
    act = (_silu(a) * b).astype(BF16)
    return jnp.dot(act, w2, preferred_element_type=F32)


def _ffn_kernel(x_ref, g_ref, w1_ref, w3_ref, w2_ref, o_ref, h_ref):
    @pl.when(pl.program_id(1) == 0)
    def _():
        xf = x_ref[...]
        h_ref[...] = _rms(xf, g_ref[...]).astype(BF16)
        o_ref[...] = xf

    o_ref[...] += _swiglu_step(h_ref[...], w1_ref[...], w3_ref[...], w2_ref[...])


def ffn(x, g, w1, w3, w2):
    s, d = x.shape
    dff = w1.shape[1]
    tm, tf = min(ROW_TILE, s), FFN_TF
    return pl.pallas_call(
        _ffn_kernel,
        out_shape=jax.ShapeDtypeStruct((s, d), F32),
        grid=(s // tm, dff // tf),
        in_specs=[
            pl.BlockSpec((tm, d), lambda i, f: (i, 0), pipeline_mode=pl.Buffered(1)),
            pl.BlockSpec((1, d), lambda i, f: (0, 0)),
            pl.BlockSpec((d, tf), lambda i, f: (0, f)),
            pl.BlockSpec((d, tf), lambda i, f: (0, f)),
            pl.BlockSpec((tf, d), lambda i, f: (f, 0)),
        ],
        out_specs=pl.BlockSpec((tm, d), lambda i, f: (i, 0), pipeline_mode=pl.Buffered(1)),
        scratch_shapes=[pltpu.VMEM((tm, d), BF16)],
        compiler_params=_cparams("parallel", "arbitrary"),
        name="ffn_swiglu",
    )(x, g.reshape(1, d), w1, w3, w2)


def _router_kernel(x_ref, g_ref, whi_ref, wlo_ref, comb_ref):
    h = _rms(x_ref[...], g_ref[...])
    h_hi = h.astype(BF16)
    h_lo = (h - h_hi.astype(F32)).astype(BF16)
    logits = (jnp.dot(h_hi, whi_ref[...], preferred_element_type=F32)
              + (jnp.dot(h_lo, whi_ref[...], preferred_element_type=F32)
                 + jnp.dot(h_hi, wlo_ref[...], preferred_element_type=F32)))
    lane = lax.broadcasted_iota(jnp.int32, logits.shape, 1)
    neg = jnp.float32(-jnp.inf)
    logits = jnp.where(lane < N_EXPERTS, logits, neg)
    v1 = jnp.max(logits, axis=-1, keepdims=True)
    i1 = jnp.min(jnp.where(logits == v1, lane, LANES), axis=-1, keepdims=True)
    rest = jnp.where(lane == i1, neg, logits)
    v2 = jnp.max(rest, axis=-1, keepdims=True)
    i2 = jnp.min(jnp.where(rest == v2, lane, LANES), axis=-1, keepdims=True)
    e2 = jnp.exp(v2 - v1)
    g1 = 1.0 / (1.0 + e2)
    g2 = e2 / (1.0 + e2)
    comb_ref[...] = (jnp.where(lane == 0, i1.astype(F32), 0.0) + jnp.where(lane == 1, i2.astype(F32), 0.0)
                     + jnp.where(lane == 2, g1, 0.0) + jnp.where(lane == 3, g2, 0.0))


def router(x, g, router_w):
    s, d = x.shape
    tm = min(ROW_TILE, s)
    wr = jnp.zeros((d, LANES), F32).at[:, :N_EXPERTS].set(router_w.astype(F32))
    w_hi = wr.astype(BF16)
    w_lo = (wr - w_hi.astype(F32)).astype(BF16)
    return pl.pallas_call(
        _router_kernel,
        out_shape=jax.ShapeDtypeStruct((s, LANES), F32),
        grid=(s // tm,),
        in_specs=[pl.BlockSpec((tm, d), lambda i: (i, 0)),
                  pl.BlockSpec((1, d), lambda i: (0, 0)),
                  pl.BlockSpec((d, LANES), lambda i: (0, 0)),
                  pl.BlockSpec((d, LANES), lambda i: (0, 0))],
        out_specs=pl.BlockSpec((tm, LANES), lambda i: (i, 0)),
        compiler_params=_cparams("parallel"),
        name="moe_router",
    )(x, g.reshape(1, d), w_hi, w_lo)


def _row_copy(src_hbm, src_row, dst_ref, dst_row, sem):
    return pltpu.make_async_copy(src_hbm.at[pl.ds(src_row, 1)], dst_ref.at[pl.ds(dst_row, 1)], sem)


def _moe_grouped_kernel(te_ref, nu_ref, src_ref, x_hbm, g_ref, w1_ref, w3_ref, w2_ref, o_ref,
                        xbuf, h_ref, sem):
    i = pl.program_id(0)
    f = pl.program_id(1)
    n_used = nu_ref[0]
    used = i < n_used
    tm = xbuf.shape[0]

    def start_gather(tile):
        def issue(r, c):
            _row_copy(x_hbm, src_ref[tile * tm + r], xbuf, r, sem).start()
            return c

        lax.fori_loop(0, tm, issue, 0, unroll=8)

    @pl.when((i == 0) & (f == 0))
    def _():
        start_gather(0)

    @pl.when(jnp.logical_not(used) & (f == 0))
    def _():
        o_ref[...] = jnp.zeros_like(o_ref)

    @pl.when(used & (f == 0))
    def _():
        def wait(r, c):
            _row_copy(x_hbm, 0, xbuf, r, sem).wait()
            return c

        lax.fori_loop(0, tm, wait, 0, unroll=8)
        h_ref[...] = _rms(xbuf[...], g_ref[...]).astype(BF16)
        o_ref[...] = _swiglu_step(h_ref[...], w1_ref[0], w3_ref[0], w2_ref[0])

    @pl.when((f == 1) & (i + 1 < n_used))
    def _():
        start_gather(i + 1)

    @pl.when(used & (f > 0))
    def _():
        o_ref[...] += _swiglu_step(h_ref[...], w1_ref[0], w3_ref[0], w2_ref[0])


def moe_grouped(x, src, g, tile_expert, n_used, w1, w3, w2):
    d = x.shape[1]
    p = src.shape[0]
    dff = w1.shape[2]
    tm, tf = MOE_TM, min(FFN_TF, dff // 2)
    nf = dff // tf

    def f_eff(i, f, nu):
        return jnp.where(i < nu[0], f, nf - 1)

    grid_spec = pltpu.PrefetchScalarGridSpec(
        num_scalar_prefetch=3,
        grid=(p // tm, nf),
        in_specs=[
            pl.BlockSpec(memory_space=pl.ANY),
            pl.BlockSpec((1, d), lambda i, f, te, nu, sr: (0, 0)),
            pl.BlockSpec((1, d, tf), lambda i, f, te, nu, sr: (te[i], 0, f_eff(i, f, nu))),
            pl.BlockSpec((1, d, tf), lambda i, f, te, nu, sr: (te[i], 0, f_eff(i, f, nu))),
            pl.BlockSpec((1, tf, d), lambda i, f, te, nu, sr: (te[i], f_eff(i, f, nu), 0)),
        ],
        out_specs=pl.BlockSpec((tm, d), lambda i, f, te, nu, sr: (i, 0), pipeline_mode=pl.Buffered(1)),
        scratch_shapes=[pltpu.VMEM((tm, d), F32), pltpu.VMEM((tm, d), BF16), pltpu.SemaphoreType.DMA(())],
    )
    return pl.pallas_call(
        _moe_grouped_kernel,
        out_shape=jax.ShapeDtypeStruct((p, d), F32),
        grid_spec=grid_spec,
        compiler_params=_cparams("arbitrary", "arbitrary"),
        name="moe_grouped",
    )(tile_expert, n_used, src, x, g.reshape(1, d), w1, w3, w2)


def _combine_kernel(dest_ref, x_ref, info_ref, y_hbm, *rest, final_norm):
    if final_norm:
        fg_ref, o_ref, buf, sem = rest
    else:
        o_ref, buf, sem = rest
    i = pl.program_id(0)
    n_rows = o_ref.shape[0]
    slot = i % 2

    def start_gather(step, sl):
        def issue(r, c):
            for k in range(TOP_K):
                _row_copy(y_hbm, dest_ref[TOP_K * (step * n_rows + r) + k], buf.at[sl, k], r, sem.at[sl]).start()
            return c

        lax.fori_loop(0, n_rows, issue, 0, unroll=4)

    @pl.when(i == 0)
    def _():
        start_gather(0, 0)

    @pl.when(i + 1 < pl.num_programs(0))
    def _():
        start_gather(i + 1, 1 - slot)

    def wait(r, c):
        for k in range(TOP_K):
            _row_copy(y_hbm, 0, buf.at[slot, k], r, sem.at[slot]).wait()
        return c

    lax.fori_loop(0, n_rows, wait, 0, unroll=4)
    acc = x_ref[...]
    info = info_ref[...]
    for k in range(TOP_K):
        acc = acc + info[:, TOP_K + k:TOP_K + k + 1] * buf[slot, k]
    o_ref[...] = _rms(acc, fg_ref[...]) if final_norm else acc


def combine(x, y, dest, info, final_g=None):
    s, d = x.shape
    rt = COMBINE_ROWS
    final_norm = final_g is not None
    in_specs = [pl.BlockSpec((rt, d), lambda i, dr: (i, 0)), pl.BlockSpec((rt, LANES), lambda i, dr: (i, 0)),
                pl.BlockSpec(memory_space=pl.ANY)]
    args = [x, info, y]
    if final_norm:
        in_specs.append(pl.BlockSpec((1, d), lambda i, dr: (0, 0)))
        args.append(final_g.astype(F32).reshape(1, d))
    grid_spec = pltpu.PrefetchScalarGridSpec(
        num_scalar_prefetch=1,
        grid=(s // rt,),
        in_specs=in_specs,
        out_specs=pl.BlockSpec((rt, d), lambda i, dr: (i, 0)),
        scratch_shapes=[pltpu.VMEM((2, TOP_K, rt, d), F32), pltpu.SemaphoreType.DMA((2,))],
    )
    return pl.pallas_call(
        functools.partial(_combine_kernel, final_norm=final_norm),
        out_shape=jax.ShapeDtypeStruct((s, d), F32),
        grid_spec=grid_spec,
        compiler_params=_cparams("arbitrary"),
        name="moe_combine",
    )(dest, *args)


def _routing_tables(experts, tm, n_tiles):
    n_assign = experts.size
    e_flat = experts.reshape(n_assign)
    onehot = (e_flat[:, None] == jnp.arange(N_EXPERTS, dtype=jnp.int32)[None, :]).astype(jnp.int32)
    csum = jnp.cumsum(onehot, axis=0)
    rank = jnp.sum(csum * onehot, axis=1) - 1
    counts = csum[-1]
    padded = ((counts + tm - 1) // tm) * tm
    seg_end = jnp.cumsum(padded)
    seg_start = seg_end - padded
    dest = (seg_start[e_flat] + rank).astype(jnp.int32)
    p = n_tiles * tm
    src = jnp.zeros((p,), jnp.int32).at[dest].set(jnp.arange(n_assign, dtype=jnp.int32) // TOP_K)
    n_used = (seg_end[-1] // tm).astype(jnp.int32)
    tile_start = jnp.arange(n_tiles, dtype=jnp.int32) * tm
    tile_e = jnp.sum((tile_start[:, None] >= seg_end[None, :]).astype(jnp.int32), axis=1)
    tile_e = jnp.minimum(tile_e, N_EXPERTS - 1)
    last_e = tile_e[n_used - 1]
    tile_e = jnp.where(jnp.arange(n_tiles) < n_used, tile_e, last_e).astype(jnp.int32)
    return src, dest, tile_e, n_used.reshape(1)


def moe(x, g, router_w, w1, w3, w2, final_g=None):
    s, _ = x.shape
    info = router(x, g, router_w)
    experts = info[:, 0:TOP_K].astype(jnp.int32)
    tm = MOE_TM
    n_tiles = (TOP_K * s) // tm + N_EXPERTS
    src, dest, tile_e, n_used = _routing_tables(experts, tm, n_tiles)
    y = moe_grouped(x, src, g, tile_e, n_used, w1.astype(BF16), w3.astype(BF16), w2.astype(BF16))
    return combine(x, y, dest, info, final_g)


def _final_norm_kernel(x_ref, g_ref, o_ref):
    o_ref[...] = _rms(x_ref[...], g_ref[...])


def final_norm(x, g):
    s, d = x.shape
    tm = min(ROW_TILE, s)
    return pl.pallas_call(
        _final_norm_kernel,
        out_shape=jax.ShapeDtypeStruct((s, d), F32),
        grid=(s // tm,),
        in_specs=[pl.BlockSpec((tm, d), lambda i: (i, 0)), pl.BlockSpec((1, d), lambda i: (0, 0))],
        out_specs=pl.BlockSpec((tm, d), lambda i: (i, 0)),
        compiler_params=_cparams("parallel"),
        name="final_norm",
    )(x, g.reshape(1, d))


def _hgrn2_kernel(zq_ref, zf_ref, zi_ref, zg_ref, lb_ref, ng_ref, tri_ref, o_ref,
                  st_ref, q_s, k_s, v_s, cum_s, o_s, dec_s, qd_s, kd_s):
    c_len = HGRN_C
    heads, t_len, hd = q_s.shape

    @pl.when(pl.program_id(0) == 0)
    def _():
        st_ref[...] = jnp.zeros_like(st_ref)

    lb = lb_ref[...]
    f = lb + (1.0 - lb) * _sigmoid(zf_ref[...])
    lf = jnp.log(f)
    lf_hi = lf.astype(BF16)
    lf_lo = (lf - lf_hi.astype(F32)).astype(BF16)
    tri = tri_ref[...]
    cum = (jnp.dot(tri, lf_hi, preferred_element_type=F32) + jnp.dot(tri, lf_lo, preferred_element_type=F32))
    cum3 = cum.reshape(t_len // c_len, c_len, heads * hd)
    last3 = cum3[:, c_len - 1:c_len, :]
    last = jnp.broadcast_to(last3, cum3.shape).reshape(t_len, heads * hd)
    dec = jnp.exp(last3.reshape(t_len // c_len, heads * hd))
    q = _silu(zq_ref[...])
    k = 1.0 - f
    qd = (q * jnp.exp(cum)).astype(BF16)
    kd = (k * jnp.exp(last - cum)).astype(BF16)
    v = zi_ref[...]
    cum2 = cum * math.log2(math.e)
    for h in range(heads):
        sl = slice(h * hd, (h + 1) * hd)
        q_s[h] = q[:, sl]
        k_s[h] = k[:, sl]
        v_s[h] = v[:, sl]
        cum_s[h] = cum2[:, sl]
        dec_s[h] = dec[:, sl]
        qd_s[h] = qd[:, sl]
        kd_s[h] = kd[:, sl]

    half = c_len // 2
    row = lax.broadcasted_iota(jnp.int32, (half, hd), 0)
    lane = lax.broadcasted_iota(jnp.int32, (half, hd), 1)

    def body(c, carry):
        r0 = pl.multiple_of(c * c_len, c_len)
        rows = pl.ds(r0, c_len)
        for h in range(heads):
            q_c = q_s[h, rows, :]
            k_c = k_s[h, rows, :]
            cm = cum_s[h, rows, :]
            m_lo = jnp.zeros((half, hd), F32)
            m_hi = jnp.zeros((half, hd), F32)
            for s in range(c_len):
                ks = k_c[s:s + 1, :]
                cs = cm[s:s + 1, :]
                if s < half:
                    w = q_c[:half] * (ks * jnp.exp2(jnp.minimum(cm[:half] - cs, 0.0)))
                    m_lo = jnp.where(lane == s, jnp.sum(w, axis=1, keepdims=True), m_lo)
                w = q_c[half:] * (ks * jnp.exp2(jnp.minimum(cm[half:] - cs, 0.0)))
                m_hi = jnp.where(lane == s, jnp.sum(w, axis=1, keepdims=True), m_hi)
            m_lo = jnp.where(row >= lane, m_lo, 0.0)
            m_hi = jnp.where(row + half >= lane, m_hi, 0.0)
            scores = jnp.concatenate([m_lo, m_hi], axis=0)[:, :c_len].astype(BF16)
            v_c = v_s[h, rows, :].astype(BF16)
            st = st_ref[h]
            o = (jnp.dot(scores, v_c, preferred_element_type=F32)
                 + lax.dot_general(qd_s[h, rows, :], st.astype(BF16), (((1,), (1,)), ((), ())),
                                   preferred_element_type=F32))
            o_s[h, rows, :] = o
            upd = lax.dot_general(v_c, kd_s[h, rows, :], (((0,), (0,)), ((), ())),
                                  preferred_element_type=F32)
            st_ref[h] = st * dec_s[h, pl.ds(c, 1), :] + upd
        return carry

    lax.fori_loop(0, t_len // c_len, body, 0, unroll=8)
    ng = ng_ref[...]
    sg = _silu(zg_ref[...])
    for h in range(heads):
        sl = slice(h * hd, (h + 1) * hd)
        o = o_s[h]
        o = o * lax.rsqrt(jnp.mean(o * o, axis=-1, keepdims=True) + EPS)
        o_ref[:, sl] = (o * ng[:, sl] * sg[:, sl]).astype(BF16)


def hgrn2(z, col0, lb, norm_g, casts=()):
    s = z.shape[0]
    t_len = min(HGRN_T, s)
    heads = W_MIX // A_HEAD_DIM
    cb = col0 // W_MIX
    r = jnp.arange(t_len)
    same = r[:, None] // HGRN_C == r[None, :] // HGRN_C
    tri = (same & (r[None, :] <= r[:, None])).astype(BF16)

    def zspec(k):
        return pl.BlockSpec((t_len, W_MIX), lambda i: (i, cb + k))

    vec = pl.BlockSpec((1, W_MIX), lambda i: (0, 0))
    sq = pl.BlockSpec((t_len, t_len), lambda i: (0, 0))
    per_head = lambda dt: pltpu.VMEM((heads, t_len, A_HEAD_DIM), dt)
    return _mixer_call(
        _hgrn2_kernel, s // t_len, lambda i: i,
        grid=(s // t_len,),
        in_specs=[zspec(0), zspec(1), zspec(2), zspec(3), vec, vec, sq],
        out_spec=pl.BlockSpec((t_len, W_MIX), lambda i: (i, 0)),
        out_shape=jax.ShapeDtypeStruct((s, W_MIX), BF16),
        scratch_shapes=[pltpu.VMEM((heads, A_HEAD_DIM, A_HEAD_DIM), F32)]
        + [per_head(F32)] * 5 + [pltpu.VMEM((heads, t_len // HGRN_C, A_HEAD_DIM), F32)] + [per_head(BF16)] * 2,
        sem=("arbitrary",), name="hgrn2",
        args=(z, z, z, z, lb.reshape(1, W_MIX), norm_g.reshape(1, W_MIX), tri), casts=casts)


def _s5_kernel(u_ref, wb_ref, pw_ref, wc_ref, d_ref, gw_ref, gb_ref, o_ref,
               carry_ref, bu_s, y_s):
    i = pl.program_id(0)
    j = pl.program_id(1)
    n_slab = pl.num_programs(1)
    t_len = bu_s.shape[0]
    half = bu_s.shape[1] // 2

    @pl.when(i == 0)
    def _():
        carry_ref[j] = jnp.zeros((SUBLANES, 2 * half), F32)

    u = u_ref[...]
    bu_s[...] = jnp.dot(u.astype(BF16), wb_ref[0], preferred_element_type=F32)
    p8_re, p8_im = pw_ref[0, 0:8, :half], pw_ref[0, 0:8, half:]

    def body(b, carry):
        c_re, c_im = carry
        r0 = pl.multiple_of(b * SUBLANES, SUBLANES)
        blk = bu_s[pl.ds(r0, SUBLANES), :]
        x_re, x_im = blk[:, :half], blk[:, half:]
        for step, k in enumerate((1, 2, 4)):
            a_re = pw_ref[0, 8 * (step + 1):8 * (step + 2), :half]
            a_im = pw_ref[0, 8 * (step + 1):8 * (step + 2), half:]
            s_re = pltpu.roll(x_re, k, 0)
            s_im = pltpu.roll(x_im, k, 0)
            x_re, x_im = (x_re + a_re * s_re - a_im * s_im,
                          x_im + a_re * s_im + a_im * s_re)
        x_re, x_im = (x_re + p8_re * c_re - p8_im * c_im,
                      x_im + p8_re * c_im + p8_im * c_re)
        bu_s[pl.ds(r0, SUBLANES), :] = jnp.concatenate([x_re, x_im], axis=1)
        n_re = jnp.broadcast_to(x_re[SUBLANES - 1:SUBLANES, :], x_re.shape)
        n_im = jnp.broadcast_to(x_im[SUBLANES - 1:SUBLANES, :], x_im.shape)
        return n_re, n_im

    c0 = carry_ref[j]
    c_re, c_im = lax.fori_loop(0, t_len // SUBLANES, body, (c0[:, :half], c0[:, half:]))
    carry_ref[j] = jnp.concatenate([c_re, c_im], axis=1)

    y = jnp.dot(bu_s[...].astype(BF16), wc_ref[0], preferred_element_type=F32)
    y = _gelu_tanh(y + d_ref[0] * u)
    y_s[j] = y

    @pl.when(j == n_slab - 1)
    def _():
        yf = jnp.concatenate([y_s[s] for s in range(y_s.shape[0])], axis=1)
        gate = jnp.dot(yf.astype(BF16), gw_ref[...], preferred_element_type=F32) + gb_ref[...]
        o_ref[...] = (yf * _sigmoid(gate)).astype(BF16)


def s5(z, col0, lam_re, lam_im, log_dt, b_re, b_im, c_re, c_im, d_skip, glu_w, glu_b, casts=()):
    s = z.shape[0]
    t_len = min(S5_T, s)
    groups, n_state = lam_re.shape
    gps = S5_SLAB // S5_GROUP
    n_slab = groups // gps
    half = gps * n_state
    cb = col0 // S5_SLAB
    lam_re = lam_re.astype(F32)
    lam_im = lam_im.astype(F32)
    dt = jnp.exp(log_dt.astype(F32))[:, None]
    mag = jnp.exp(lam_re * dt)
    ang = lam_im * dt
    ab_re = mag * jnp.cos(ang)
    ab_im = mag * jnp.sin(ang)
    den = lam_re * lam_re + lam_im * lam_im
    num_re = ab_re - 1.0
    coef_re = (num_re * lam_re + ab_im * lam_im) / den
    coef_im = (ab_im * lam_re - num_re * lam_im) / den
    br = b_re.astype(F32)
    bi = b_im.astype(F32)
    bb_re = coef_re[..., None] * br - coef_im[..., None] * bi
    bb_im = coef_re[..., None] * bi + coef_im[..., None] * br
    eye = jnp.eye(gps, dtype=F32)

    def blockdiag_in(bb):
        t = bb.reshape(n_slab, gps, n_state, S5_GROUP)
        return jnp.einsum('sgnp,gh->sgphn', t, eye).reshape(n_slab, gps * S5_GROUP, gps * n_state)

    wb = jnp.concatenate([blockdiag_in(bb_re), blockdiag_in(bb_im)], axis=-1).astype(BF16)

    def blockdiag_out(cc):
        t = cc.reshape(n_slab, gps, S5_GROUP, n_state)
        return jnp.einsum('sgpn,gh->sgnhp', t, eye).reshape(n_slab, gps * n_state, gps * S5_GROUP)

    wc = jnp.concatenate([blockdiag_out(c_re.astype(F32)), -blockdiag_out(c_im.astype(F32))],
                         axis=1).astype(BF16)
    r8 = jnp.arange(SUBLANES)
    expo = jnp.concatenate([r8 + 1.0] + [jnp.full((SUBLANES,), float(k)) for k in (1, 2, 4)]).astype(F32)
    keep = jnp.concatenate([jnp.ones((SUBLANES,), F32)] + [(r8 >= k).astype(F32) for k in (1, 2, 4)])
    expo = expo[:, None, None]
    p_mag = jnp.exp(expo * (lam_re * dt)[None]) * keep[:, None, None]
    p_re = (p_mag * jnp.cos(expo * ang[None])).reshape(S5_PW_ROWS, n_slab, half)
    p_im = (p_mag * jnp.sin(expo * ang[None])).reshape(S5_PW_ROWS, n_slab, half)
    pw = jnp.moveaxis(jnp.concatenate([p_re, p_im], axis=-1), 1, 0)

    return _mixer_call(
        _s5_kernel, (s // t_len) * n_slab, lambda i, j: i * n_slab + j,
        grid=(s // t_len, n_slab),
        in_specs=[
            pl.BlockSpec((t_len, S5_SLAB), lambda i, j: (i, cb + j)),
            pl.BlockSpec((1, S5_SLAB, 2 * half), lambda i, j: (j, 0, 0)),
            pl.BlockSpec((1, S5_PW_ROWS, 2 * half), lambda i, j: (j, 0, 0)),
            pl.BlockSpec((1, 2 * half, S5_SLAB), lambda i, j: (j, 0, 0)),
            pl.BlockSpec((1, 1, S5_SLAB), lambda i, j: (j, 0, 0)),
            pl.BlockSpec((W_MIX, W_MIX), lambda i, j: (0, 0)),
            pl.BlockSpec((1, W_MIX), lambda i, j: (0, 0)),
        ],
        out_spec=pl.BlockSpec((t_len, W_MIX), lambda i, j: (i, 0)),
        out_shape=jax.ShapeDtypeStruct((s, W_MIX), BF16),
        scratch_shapes=[pltpu.VMEM((n_slab, SUBLANES, 2 * half), F32),
                        pltpu.VMEM((t_len, 2 * half), F32),
                        pltpu.VMEM((n_slab, t_len, S5_SLAB), F32)],
        sem=("arbitrary", "arbitrary"), name="s5",
        args=(z, wb, pw, wc, d_skip.astype(F32).reshape(n_slab, 1, S5_SLAB), glu_w.astype(BF16),
              glu_b.astype(F32).reshape(1, W_MIX)), casts=casts)


def _retention_kernel(zq_ref, zk_ref, zv_ref, zg_ref, cos_ref, sin_ref, dmat_ref, qdec_ref, kdec_ref,
                      cdec_ref, ng_ref, o_ref, st_ref):
    @pl.when(pl.program_id(0) == 0)
    def _():
        st_ref[...] = jnp.zeros_like(st_ref)

    heads, hd, _ = st_ref.shape
    half = hd // 2
    cos = cos_ref[...]
    sin = sin_ref[...]

    def rope(t):
        t1, t2 = t[:, :half], t[:, half:]
        return jnp.concatenate([t1 * cos - t2 * sin, t1 * sin + t2 * cos], axis=1)

    for h in range(heads):
        sl = slice(h * hd, (h + 1) * hd)
        q = rope(zq_ref[:, sl])
        k = rope(zk_ref[:, sl]) * hd ** -0.5
        v = zv_ref[:, sl].astype(BF16)
        qb = q.astype(BF16)
        scores = lax.dot_general(qb, k.astype(BF16), (((1,), (1,)), ((), ())),
                                 preferred_element_type=F32) * dmat_ref[h]
        st = st_ref[h]
        o = (jnp.dot(scores.astype(BF16), v, preferred_element_type=F32)
             + jnp.dot(qb, st.astype(BF16), preferred_element_type=F32) * qdec_ref[h])
        kd = (k * kdec_ref[h]).astype(BF16)
        st_ref[h] = cdec_ref[h] * st + lax.dot_general(kd, v, (((0,), (0,)), ((), ())),
                                                       preferred_element_type=F32)
        mu = jnp.mean(o, axis=-1, keepdims=True)
        oc = o - mu
        var = jnp.mean(oc * oc, axis=-1, keepdims=True)
        o = oc * lax.rsqrt(var + EPS)
        o_ref[:, sl] = (o * ng_ref[:, sl] * _silu(zg_ref[:, sl])).astype(BF16)


def retention(z, col0, norm_g, casts=()):
    s = z.shape[0]
    t_len = min(RET_T, s)
    hd = W_MIX // C_HEADS
    pos = jnp.arange(s, dtype=F32)
    inv_freq = ROPE_THETA ** (-jnp.arange(0, hd, 2, dtype=F32) / hd)
    ang = pos[:, None] * inv_freq[None, :]
    cos = jnp.cos(ang)
    sin = jnp.sin(ang)
    log_gamma = jnp.log(1.0 - 2.0 ** (-5.0 - jnp.arange(C_HEADS, dtype=F32)))
    idx = jnp.arange(t_len, dtype=F32)
    rel = idx[:, None] - idx[None, :]
    dmat = jnp.where(rel[None] >= 0, jnp.exp(jnp.maximum(rel, 0.0)[None] * log_gamma[:, None, None]), 0.0)
    qdec = jnp.exp((idx + 1.0)[None, :] * log_gamma[:, None])[..., None]
    kdec = jnp.exp((t_len - 1.0 - idx)[None, :] * log_gamma[:, None])[..., None]
    cdec = jnp.broadcast_to(jnp.exp(t_len * log_gamma)[:, None, None], (C_HEADS, 1, hd))

    cb = col0 // W_MIX

    def zspec(k):
        return pl.BlockSpec((t_len, W_MIX), lambda i: (i, cb + k))

    tab = pl.BlockSpec((t_len, hd // 2), lambda i: (i, 0))
    full = lambda shape: pl.BlockSpec(shape, lambda i: (0,) * len(shape))
    return _mixer_call(
        _retention_kernel, s // t_len, lambda i: i,
        grid=(s // t_len,),
        in_specs=[zspec(0), zspec(1), zspec(2), zspec(3), tab, tab,
                  full((C_HEADS, t_len, t_len)), full((C_HEADS, t_len, 1)), full((C_HEADS, t_len, 1)),
                  full((C_HEADS, 1, hd)), full((1, W_MIX))],
        out_spec=pl.BlockSpec((t_len, W_MIX), lambda i: (i, 0)),
        out_shape=jax.ShapeDtypeStruct((s, W_MIX), BF16),
        scratch_shapes=[pltpu.VMEM((C_HEADS, hd, hd), F32)],
        sem=("arbitrary",), name="retention",
        args=(z, z, z, z, cos, sin, dmat, qdec, kdec, cdec, norm_g.reshape(1, W_MIX)), casts=casts)


def _rglru_kernel(zg_ref, zx_ref, cw_ref, cb_ref, wa_ref, ba_ref, wx_ref, bx_ref, sp_ref, o_ref,
                  xbuf, h_ref, a_s, u_s):
    t_len = zx_ref.shape[0]

    @pl.when(pl.program_id(0) == 0)
    def _():
        xbuf[0:SUBLANES, :] = jnp.zeros((SUBLANES, W_MIX), F32)
        h_ref[...] = jnp.zeros_like(h_ref)

    xbuf[SUBLANES:, :] = zx_ref[...]
    xc = cb_ref[...]
    for tap in range(CONV_WIDTH):
        off = SUBLANES - (CONV_WIDTH - 1) + tap
        xc = xc + xbuf[off:off + t_len, :] * cw_ref[tap:tap + 1, :]
    xbuf[0:SUBLANES, :] = xbuf[t_len:t_len + SUBLANES, :]
    xcb = xc.astype(BF16)
    n_blk = W_MIX // D_BLOCK
    pre_r = jnp.concatenate(
        [jnp.dot(xcb[:, b * D_BLOCK:(b + 1) * D_BLOCK], wa_ref[b], preferred_element_type=F32)
         for b in range(n_blk)], axis=1)
    pre_i = jnp.concatenate(
        [jnp.dot(xcb[:, b * D_BLOCK:(b + 1) * D_BLOCK], wx_ref[b], preferred_element_type=F32)
         for b in range(n_blk)], axis=1)
    r = _sigmoid(pre_r + ba_ref[...])
    gi = _sigmoid(pre_i + bx_ref[...])
    log_a = -RG_C * r * sp_ref[...]
    a = jnp.exp(log_a)
    a_s[...] = a
    u_s[...] = jnp.sqrt(1.0 - a * a) * (gi * xc)
    row = lax.broadcasted_iota(jnp.int32, (SUBLANES, W_MIX), 0)

    def body(b, h):
        r0 = pl.multiple_of(b * SUBLANES, SUBLANES)
        aa = a_s[pl.ds(r0, SUBLANES), :]
        uu = u_s[pl.ds(r0, SUBLANES), :]
        for k in (1, 2, 4):
            us = jnp.where(row >= k, pltpu.roll(uu, k, 0), 0.0)
            as_ = jnp.where(row >= k, pltpu.roll(aa, k, 0), 1.0)
            uu = uu + aa * us
            aa = aa * as_
        hh = uu + aa * h
        u_s[pl.ds(r0, SUBLANES), :] = hh
        return jnp.broadcast_to(hh[SUBLANES - 1:SUBLANES, :], hh.shape)

    h_ref[...] = lax.fori_loop(0, t_len // SUBLANES, body, h_ref[...])
    o_ref[...] = (_gelu_tanh(zg_ref[...]) * u_s[...]).astype(BF16)


def rglru(z, col0, conv_w, conv_b, w_a, b_a, w_x, b_x, lam, casts=()):
    s = z.shape[0]
    t_len = min(RG_T, s)
    cb = col0 // W_MIX
    sp = jax.nn.softplus(-lam.astype(F32)).reshape(1, W_MIX)
    row = lambda a: a.astype(F32).reshape(1, W_MIX)
    full = lambda shape: pl.BlockSpec(shape, lambda i: (0,) * len(shape))
    n_blk = W_MIX // D_BLOCK
    return _mixer_call(
        _rglru_kernel, s // t_len, lambda i: i,
        grid=(s // t_len,),
        in_specs=[pl.BlockSpec((t_len, W_MIX), lambda i: (i, cb)),
                  pl.BlockSpec((t_len, W_MIX), lambda i: (i, cb + 1)),
                  full((CONV_WIDTH, W_MIX)), full((1, W_MIX)),
                  full((n_blk, D_BLOCK, D_BLOCK)), full((1, W_MIX)),
                  full((n_blk, D_BLOCK, D_BLOCK)), full((1, W_MIX)), full((1, W_MIX))],
        out_spec=pl.BlockSpec((t_len, W_MIX), lambda i: (i, 0)),
        out_shape=jax.ShapeDtypeStruct((s, W_MIX), BF16),
        scratch_shapes=[pltpu.VMEM((t_len + SUBLANES, W_MIX), F32),
                        pltpu.VMEM((SUBLANES, W_MIX), F32),
                        pltpu.VMEM((t_len, W_MIX), F32),
                        pltpu.VMEM((t_len, W_MIX), F32)],
        sem=("arbitrary",), name="rglru",
        args=(z, z, conv_w.astype(F32), row(conv_b), w_a.astype(BF16), row(b_a), w_x.astype(BF16), row(b_x), sp),
        casts=casts)


def kernel(x, norm_mix_g, norm_ffn_g, final_norm_g, w_in, w_out, hgrn_lb_logits, hgrn_norm_g, s5_lambda_re, s5_lambda_im, s5_log_dt, s5_b_re, s5_b_im, s5_c_re, s5_c_im, s5_d, s5_glu_w, s5_glu_b, ret_norm_g, rg_conv_w, rg_conv_b, rg_w_a, rg_b_a, rg_w_x, rg_b_x, rg_lambda, ffn_w1, ffn_w3, ffn_w2, router_w, moe_w1, moe_w3, moe_w2):
    b_, s_, d_ = x.shape
    depth = w_in.shape[0]
    xs = x.reshape(b_ * s_, d_).astype(F32)
    lb_p = jax.nn.softmax(hgrn_lb_logits.astype(F32), axis=0)
    lb_all = jnp.cumsum(lb_p, axis=0) - lb_p[0]
    col_a, col_b, col_c, col_d = 0, 4 * W_MIX, 5 * W_MIX, 9 * W_MIX
    w_in_b = w_in[0].astype(BF16)
    for layer in range(depth):
        m = layer // 2
        dense = layer % 2 == 0
        ch = (ffn_w1, ffn_w3, ffn_w2) if dense else (moe_w1, moe_w3, moe_w2)
        nxt = ((w_in, layer + 1),) if layer + 1 < depth else ()
        z, _ = norm_matmul(xs, norm_mix_g[layer], w_in_b)
        o_a, (w1_b,) = hgrn2(z, col_a, lb_all[layer], hgrn_norm_g[layer], casts=((ch[0], m),))
        o_b, (w3_b, w2_b) = s5(z, col_b, s5_lambda_re[layer], s5_lambda_im[layer], s5_log_dt[layer],
                               s5_b_re[layer], s5_b_im[layer], s5_c_re[layer], s5_c_im[layer],
                               s5_d[layer], s5_glu_w[layer], s5_glu_b[layer], casts=((ch[1], m), (ch[2], m)))
        o_c, _ = retention(z, col_c, ret_norm_g[layer])
        o_d, rest = rglru(z, col_d, rg_conv_w[layer], rg_conv_b[layer], rg_w_a[layer], rg_b_a[layer],
                          rg_w_x[layer], rg_b_x[layer], rg_lambda[layer], casts=((w_out, layer),) + nxt)
        xs, _ = out_proj((o_a, o_b, o_c, o_d), rest[0], xs)
        if nxt:
            w_in_b = rest[1]
        if dense:
            xs = ffn(xs, norm_ffn_g[layer], w1_b, w3_b, w2_b)
        else:
            last = layer == depth - 1
            xs = moe(xs, norm_ffn_g[layer], router_w[m], w1_b, w3_b, w2_b, final_norm_g if last else None)
            if last:
                return xs.reshape(b_, s_, d_)
    return final_norm(xs, final_norm_g).reshape(b_, s_, d_)
```

```python
import functools
import math

import jax
import jax.numpy as jnp
from jax import lax
from jax.experimental import pallas as pl
from jax.experimental.pallas import tpu as pltpu

F32 = jnp.float32
BF16 = jnp.bfloat16
EPS = 1e-6

V7X_VMEM_BYTES = 64 * 1024 * 1024
VMEM_LIMIT_BYTES = V7X_VMEM_BYTES - 4 * 1024 * 1024
SUBLANES = 8
LANES = 128

A_HEAD_DIM = 128
S5_GROUP = 16
S5_STATE = 64
C_HEADS = 4
ROPE_THETA = 10000.0
D_BLOCK = 128
CONV_WIDTH = 4
RG_C = 8.0
N_EXPERTS = 8
TOP_K = 2
W_MIX = 1024

ROW_TILE = 512
IN_PROJ_TN = 1024
OUT_PROJ_TM = 1024
OUT_PROJ_TN = 1024
FFN_TF = 512
MOE_TM = 512
COMBINE_ROWS = 256
HGRN_T = 256
HGRN_C = 16
RET_T = 256
S5_T = 512
S5_SLAB = 128
S5_PW_ROWS = 4 * SUBLANES
RG_T = 512
CAST_ROW_ALIGN = 16
CAST_INLINE_ELEMS = 256 * 1024


def _cparams(*sem):
    return pltpu.CompilerParams(dimension_semantics=sem, vmem_limit_bytes=VMEM_LIMIT_BYTES)


def _rms(xf, g):
    return xf * lax.rsqrt(jnp.mean(xf * xf, axis=-1, keepdims=True) + EPS) * g


def _sigmoid(x):
    return 0.5 + 0.5 * jnp.tanh(0.5 * x)


def _silu(x):
    h = 0.5 * x
    return h + h * jnp.tanh(h)


def _gelu_tanh(x):
    c = math.sqrt(2.0 / math.pi)
    return 0.5 * x * (1.0 + jnp.tanh(c * (x + 0.044715 * (x * x * x))))


def _cast_specs(casts, n_steps, step_index):
    in_specs, out_specs, shapes = [], [], []
    for w, lead in casts:
        *mid, rows, cols = w.shape[1:]
        per = n_steps // math.prod(mid)
        assert per * math.prod(mid) == n_steps, (w.shape, n_steps)
        nr = next(n for n in range(per, 0, -1)
                  if per % n == 0 and rows % (n * CAST_ROW_ALIGN) == 0 and cols % (per // n * LANES) == 0)
        nc = per // nr
        blk = (1,) * len(mid) + (rows // nr, cols // nc)

        def pos(*g, mid=tuple(mid), nr=nr, nc=nc):
            t = step_index(*g)
            idx = [(t // nc) % nr, t % nc]
            t = t // (nr * nc)
            for size in reversed(mid):
                idx.insert(0, t % size)
                t = t // size
            return tuple(idx)

        in_specs.append(pl.BlockSpec((1,) + blk, lambda *g, pos=pos, lead=lead: (lead,) + pos(*g)))
        out_specs.append(pl.BlockSpec(blk, pos))
        shapes.append(jax.ShapeDtypeStruct(w.shape[1:], BF16))
    return in_specs, out_specs, shapes


def _cast_blocks(src_refs, dst_refs):
    for src, dst in zip(src_refs, dst_refs):
        rows, cols = dst.shape[-2:]
        lead = (0,) * (len(dst.shape) - 2)
        if rows * cols <= CAST_INLINE_ELEMS:
            dst[...] = src[0].astype(BF16)
        else:
            def body(c, carry, src=src, dst=dst, lead=lead):
                sl = pl.ds(pl.multiple_of(c * CAST_ROW_ALIGN, CAST_ROW_ALIGN), CAST_ROW_ALIGN)
                dst[lead + (sl, slice(None))] = src[(0,) + lead + (sl, slice(None))].astype(BF16)
                return carry

            lax.fori_loop(0, rows // CAST_ROW_ALIGN, body, 0)


def _with_casts(body, n_in, n_cast, cast_when):
    def kern(*refs):
        ins = refs[:n_in]
        cast_in = refs[n_in:n_in + n_cast]
        out = refs[n_in + n_cast]
        cast_out = refs[n_in + n_cast + 1:n_in + 2 * n_cast + 1]
        scratch = refs[n_in + 2 * n_cast + 1:]
        body(*ins, out, *scratch)
        if cast_when is None or not n_cast:
            _cast_blocks(cast_in, cast_out)
        else:
            pl.when(cast_when())(lambda: _cast_blocks(cast_in, cast_out))

    return kern


def _mixer_call(body, n_steps, step_index, grid, in_specs, out_spec, out_shape, scratch_shapes, sem, name,
                args, casts, cast_when=None):
    cin, cout, cshapes = _cast_specs(casts, n_steps, step_index)
    outs = pl.pallas_call(
        _with_casts(body, len(in_specs), len(casts), cast_when),
        out_shape=[out_shape] + cshapes,
        grid=grid,
        in_specs=list(in_specs) + cin,
        out_specs=[out_spec] + cout,
        scratch_shapes=scratch_shapes,
        compiler_params=_cparams(*sem),
        name=name,
    )(*args, *[w for w, _ in casts])
    return outs[0], tuple(outs[1:])


def _norm_matmul_kernel(x_ref, g_ref, w_ref, o_ref, h_ref):
    @pl.when(pl.program_id(1) == 0)
    def _():
        h_ref[...] = _rms(x_ref[...], g_ref[...]).astype(BF16)

    o_ref[...] = jnp.dot(h_ref[...], w_ref[...], preferred_element_type=F32)


def norm_matmul(x, g, w, casts=()):
    s, d = x.shape
    n = w.shape[1]
    tm, tn = min(ROW_TILE, s), IN_PROJ_TN
    n_col = n // tn
    cast_cols = 1 << (n_col.bit_length() - 1)
    return _mixer_call(
        _norm_matmul_kernel, (s // tm) * cast_cols, lambda i, j: i * cast_cols + jnp.minimum(j, cast_cols - 1),
        grid=(s // tm, n_col),
        in_specs=[
            pl.BlockSpec((tm, d), lambda i, j: (i, 0)),
            pl.BlockSpec((1, d), lambda i, j: (0, 0)),
            pl.BlockSpec((d, tn), lambda i, j: (0, j)),
        ],
        out_spec=pl.BlockSpec((tm, tn), lambda i, j: (i, j)),
        out_shape=jax.ShapeDtypeStruct((s, n), F32),
        scratch_shapes=[pltpu.VMEM((tm, d), BF16)],
        sem=("arbitrary", "arbitrary"), name="norm_in_proj",
        args=(x, g.reshape(1, d), w), casts=casts, cast_when=lambda: pl.program_id(1) < cast_cols)


def _out_proj_kernel(oa_ref, ob_ref, oc_ref, od_ref, w_ref, x_ref, o_ref):
    acc = x_ref[...]
    for idx, r in enumerate((oa_ref, ob_ref, oc_ref, od_ref)):
        acc = acc + jnp.dot(r[...], w_ref[idx * W_MIX:(idx + 1) * W_MIX, :], preferred_element_type=F32)
    o_ref[...] = acc


def out_proj(parts, w, x, casts=()):
    s, d = x.shape
    tm, tn = min(OUT_PROJ_TM, s), OUT_PROJ_TN
    n_col = d // tn
    part_spec = pl.BlockSpec((tm, W_MIX), lambda i, j: (i, 0))
    return _mixer_call(
        _out_proj_kernel, (s // tm) * n_col, lambda i, j: i * n_col + j,
        grid=(s // tm, n_col),
        in_specs=[part_spec, part_spec, part_spec, part_spec,
                  pl.BlockSpec((4 * W_MIX, tn), lambda i, j: (0, j)),
                  pl.BlockSpec((tm, tn), lambda i, j: (i, j))],
        out_spec=pl.BlockSpec((tm, tn), lambda i, j: (i, j)),
        out_shape=jax.ShapeDtypeStruct((s, d), F32),
        scratch_shapes=[], sem=("arbitrary", "arbitrary"), name="out_proj",
        args=(*parts, w, x), casts=casts)


def _swiglu_step(h, w1, w3, w2):
    a = jnp.dot(h, w1, preferred_element_type=F32)
    b = jnp.dot(h, w3, preferred_element_type=F32)
    act = (_silu(a) * b).astype(BF16)
    return jnp.dot(act, w2, preferred_element_type=F32)


def _ffn_kernel(x_ref, g_ref, w1_ref, w3_ref, w2_ref, o_ref, h_ref):
    @pl.when(pl.program_id(1) == 0)
    def _():
        xf = x_ref[...]
        h_ref[...] = _rms(xf, g_ref[...]).astype(BF16)
        o_ref[...] = xf

    o_ref[...] += _swiglu_step(h_ref[...], w1_ref[...], w3_ref[...], w2_ref[...])


def ffn(x, g, w1, w3, w2):
    s, d = x.shape
    dff = w1.shape[1]
    tm, tf = min(ROW_TILE, s), FFN_TF
    return pl.pallas_call(
        _ffn_kernel,
        out_shape=jax.ShapeDtypeStruct((s, d), F32),
        grid=(s // tm, dff // tf),
        in_specs=[
            pl.BlockSpec((tm, d), lambda i, f: (i, 0), pipeline_mode=pl.Buffered(1)),
            pl.BlockSpec((1, d), lambda i, f: (0, 0)),
            pl.BlockSpec((d, tf), lambda i, f: (0, f)),
            pl.BlockSpec((d, tf), lambda i, f: (0, f)),
            pl.BlockSpec((tf, d), lambda i, f: (f, 0)),
        ],
        out_specs=pl.BlockSpec((tm, d), lambda i, f: (i, 0), pipeline_mode=pl.Buffered(1)),
        scratch_shapes=[pltpu.VMEM((tm, d), BF16)],
        compiler_params=_cparams("parallel", "arbitrary"),
        name="ffn_swiglu",
    )(x, g.reshape(1, d), w1, w3, w2)


def _router_kernel(x_ref, g_ref, whi_ref, wlo_ref, comb_ref):
    h = _rms(x_ref[...], g_ref[...])
    h_hi = h.astype(BF16)
    h_lo = (h - h_hi.astype(F32)).astype(BF16)
    logits = (jnp.dot(h_hi, whi_ref[...], preferred_element_type=F32)
              + (jnp.dot(h_lo, whi_ref[...], preferred_element_type=F32)
                 + jnp.dot(h_hi, wlo_ref[...], preferred_element_type=F32)))
    lane = lax.broadcasted_iota(jnp.int32, logits.shape, 1)
    neg = jnp.float32(-jnp.inf)
    logits = jnp.where(lane < N_EXPERTS, logits, neg)
    v1 = jnp.max(logits, axis=-1, keepdims=True)
    i1 = jnp.min(jnp.where(logits == v1, lane, LANES), axis=-1, keepdims=True)
    rest = jnp.where(lane == i1, neg, logits)
    v2 = jnp.max(rest, axis=-1, keepdims=True)
    i2 = jnp.min(jnp.where(rest == v2, lane, LANES), axis=-1, keepdims=True)
    e2 = jnp.exp(v2 - v1)
    g1 = 1.0 / (1.0 + e2)
    g2 = e2 / (1.0 + e2)
    comb_ref[...] = (jnp.where(lane == 0, i1.astype(F32), 0.0) + jnp.where(lane == 1, i2.astype(F32), 0.0)
                     + jnp.where(lane == 2, g1, 0.0) + jnp.where(lane == 3, g2, 0.0))


def router(x, g, router_w):
    s, d = x.shape
    tm = min(ROW_TILE, s)
    wr = jnp.zeros((d, LANES), F32).at[:, :N_EXPERTS].set(router_w.astype(F32))
    w_hi = wr.astype(BF16)
    w_lo = (wr - w_hi.astype(F32)).astype(BF16)
    return pl.pallas_call(
        _router_kernel,
        out_shape=jax.ShapeDtypeStruct((s, LANES), F32),
        grid=(s // tm,),
        in_specs=[pl.BlockSpec((tm, d), lambda i: (i, 0)),
                  pl.BlockSpec((1, d), lambda i: (0, 0)),
                  pl.BlockSpec((d, LANES), lambda i: (0, 0)),
                  pl.BlockSpec((d, LANES), lambda i: (0, 0))],
        out_specs=pl.BlockSpec((tm, LANES), lambda i: (i, 0)),
        compiler_params=_cparams("parallel"),
        name="moe_router",
    )(x, g.reshape(1, d), w_hi, w_lo)


def _row_copy(src_hbm, src_row, dst_ref, dst_row, sem):
    return pltpu.make_async_copy(src_hbm.at[pl.ds(src_row, 1)], dst_ref.at[pl.ds(dst_row, 1)], sem)


def _moe_grouped_kernel(te_ref, nu_ref, src_ref, x_hbm, g_ref, w1_ref, w3_ref, w2_ref, o_ref,
                        xbuf, h_ref, sem):
    i = pl.program_id(0)
    f = pl.program_id(1)
    n_used = nu_ref[0]
    used = i < n_used
    tm = xbuf.shape[0]

    def start_gather(tile):
        def issue(r, c):
            _row_copy(x_hbm, src_ref[tile * tm + r], xbuf, r, sem).start()
            return c

        lax.fori_loop(0, tm, issue, 0, unroll=8)

    @pl.when((i == 0) & (f == 0))
    def _():
        start_gather(0)

    @pl.when(jnp.logical_not(used) & (f == 0))
    def _():
        o_ref[...] = jnp.zeros_like(o_ref)

    @pl.when(used & (f == 0))
    def _():
        def wait(r, c):
            _row_copy(x_hbm, 0, xbuf, r, sem).wait()
            return c

        lax.fori_loop(0, tm, wait, 0, unroll=8)
        h_ref[...] = _rms(xbuf[...], g_ref[...]).astype(BF16)
        o_ref[...] = _swiglu_step(h_ref[...], w1_ref[0], w3_ref[0], w2_ref[0])

    @pl.when((f == 1) & (i + 1 < n_used))
    def _():
        start_gather(i + 1)

    @pl.when(used & (f > 0))
    def _():
        o_ref[...] += _swiglu_step(h_ref[...], w1_ref[0], w3_ref[0], w2_ref[0])


def moe_grouped(x, src, g, tile_expert, n_used, w1, w3, w2):
    d = x.shape[1]
    p = src.shape[0]
    dff = w1.shape[2]
    tm, tf = MOE_TM, min(FFN_TF, dff // 2)
    nf = dff // tf

    def f_eff(i, f, nu):
        return jnp.where(i < nu[0], f, nf - 1)

    grid_spec = pltpu.PrefetchScalarGridSpec(
        num_scalar_prefetch=3,
        grid=(p // tm, nf),
        in_specs=[
            pl.BlockSpec(memory_space=pl.ANY),
            pl.BlockSpec((1, d), lambda i, f, te, nu, sr: (0, 0)),
            pl.BlockSpec((1, d, tf), lambda i, f, te, nu, sr: (te[i], 0, f_eff(i, f, nu))),
            pl.BlockSpec((1, d, tf), lambda i, f, te, nu, sr: (te[i], 0, f_eff(i, f, nu))),
            pl.BlockSpec((1, tf, d), lambda i, f, te, nu, sr: (te[i], f_eff(i, f, nu), 0)),
        ],
        out_specs=pl.BlockSpec((tm, d), lambda i, f, te, nu, sr: (i, 0), pipeline_mode=pl.Buffered(1)),
        scratch_shapes=[pltpu.VMEM((tm, d), F32), pltpu.VMEM((tm, d), BF16), pltpu.SemaphoreType.DMA(())],
    )
    return pl.pallas_call(
        _moe_grouped_kernel,
        out_shape=jax.ShapeDtypeStruct((p, d), F32),
        grid_spec=grid_spec,
        compiler_params=_cparams("arbitrary", "arbitrary"),
        name="moe_grouped",
    )(tile_expert, n_used, src, x, g.reshape(1, d), w1, w3, w2)


def _combine_kernel(dest_ref, x_ref, info_ref, y_hbm, *rest, final_norm):
    if final_norm:
        fg_ref, o_ref, buf, sem = rest
    else:
        o_ref, buf, sem = rest
    i = pl.program_id(0)
    n_rows = o_ref.shape[0]
    slot = i % 2

    def start_gather(step, sl):
        def issue(r, c):
            for k in range(TOP_K):
                _row_copy(y_hbm, dest_ref[TOP_K * (step * n_rows + r) + k], buf.at[sl, k], r, sem.at[sl]).start()
            return c

        lax.fori_loop(0, n_rows, issue, 0, unroll=4)

    @pl.when(i == 0)
    def _():
        start_gather(0, 0)

    @pl.when(i + 1 < pl.num_programs(0))
    def _():
        start_gather(i + 1, 1 - slot)

    def wait(r, c):
        for k in range(TOP_K):
            _row_copy(y_hbm, 0, buf.at[slot, k], r, sem.at[slot]).wait()
        return c

    lax.fori_loop(0, n_rows, wait, 0, unroll=4)
    acc = x_ref[...]
    info = info_ref[...]
    for k in range(TOP_K):
        acc = acc + info[:, TOP_K + k:TOP_K + k + 1] * buf[slot, k]
    o_ref[...] = _rms(acc, fg_ref[...]) if final_norm else acc


def combine(x, y, dest, info, final_g=None):
    s, d = x.shape
    rt = COMBINE_ROWS
    final_norm = final_g is not None
    in_specs = [pl.BlockSpec((rt, d), lambda i, dr: (i, 0)), pl.BlockSpec((rt, LANES), lambda i, dr: (i, 0)),
                pl.BlockSpec(memory_space=pl.ANY)]
    args = [x, info, y]
    if final_norm:
        in_specs.append(pl.BlockSpec((1, d), lambda i, dr: (0, 0)))
        args.append(final_g.astype(F32).reshape(1, d))
    grid_spec = pltpu.PrefetchScalarGridSpec(
        num_scalar_prefetch=1,
        grid=(s // rt,),
        in_specs=in_specs,
        out_specs=pl.BlockSpec((rt, d), lambda i, dr: (i, 0)),
        scratch_shapes=[pltpu.VMEM((2, TOP_K, rt, d), F32), pltpu.SemaphoreType.DMA((2,))],
    )
    return pl.pallas_call(
        functools.partial(_combine_kernel, final_norm=final_norm),
        out_shape=jax.ShapeDtypeStruct((s, d), F32),
        grid_spec=grid_spec,
        compiler_params=_cparams("arbitrary"),
        name="moe_combine",
    )(dest, *args)


def _routing_tables(experts, tm, n_tiles):
    n_assign = experts.size
    e_flat = experts.reshape(n_assign)
    onehot = (e_flat[:, None] == jnp.arange(N_EXPERTS, dtype=jnp.int32)[None, :]).astype(jnp.int32)
    csum = jnp.cumsum(onehot, axis=0)
    rank = jnp.sum(csum * onehot, axis=1) - 1
    counts = csum[-1]
    padded = ((counts + tm - 1) // tm) * tm
    seg_end = jnp.cumsum(padded)
    seg_start = seg_end - padded
    dest = (seg_start[e_flat] + rank).astype(jnp.int32)
    p = n_tiles * tm
    src = jnp.zeros((p,), jnp.int32).at[dest].set(jnp.arange(n_assign, dtype=jnp.int32) // TOP_K)
    n_used = (seg_end[-1] // tm).astype(jnp.int32)
    tile_start = jnp.arange(n_tiles, dtype=jnp.int32) * tm
    tile_e = jnp.sum((tile_start[:, None] >= seg_end[None, :]).astype(jnp.int32), axis=1)
    tile_e = jnp.minimum(tile_e, N_EXPERTS - 1)
    last_e = tile_e[n_used - 1]
    tile_e = jnp.where(jnp.arange(n_tiles) < n_used, tile_e, last_e).astype(jnp.int32)
    return src, dest, tile_e, n_used.reshape(1)


def moe(x, g, router_w, w1, w3, w2, final_g=None):
    s, _ = x.shape
    info = router(x, g, router_w)
    experts = info[:, 0:TOP_K].astype(jnp.int32)
    tm = MOE_TM
    n_tiles = (TOP_K * s) // tm + N_EXPERTS
    src, dest, tile_e, n_used = _routing_tables(experts, tm, n_tiles)
    y = moe_grouped(x, src, g, tile_e, n_used, w1.astype(BF16), w3.astype(BF16), w2.astype(BF16))
    return combine(x, y, dest, info, final_g)


def _final_norm_kernel(x_ref, g_ref, o_ref):
    o_ref[...] = _rms(x_ref[...], g_ref[...])


def final_norm(x, g):
    s, d = x.shape
    tm = min(ROW_TILE, s)
    return pl.pallas_call(
        _final_norm_kernel,
        out_shape=jax.ShapeDtypeStruct((s, d), F32),
        grid=(s // tm,),
        in_specs=[pl.BlockSpec((tm, d), lambda i: (i, 0)), pl.BlockSpec((1, d), lambda i: (0, 0))],
        out_specs=pl.BlockSpec((tm, d), lambda i: (i, 0)),
        compiler_params=_cparams("parallel"),
        name="final_norm",
    )(x, g.reshape(1, d))


def _hgrn2_kernel(zq_ref, zf_ref, zi_ref, zg_ref, lb_ref, ng_ref, tri_ref, o_ref,
                  st_ref, q_s, k_s, v_s, cum_s, o_s, dec_s, qd_s, kd_s):
    c_len = HGRN_C
    heads, t_len, hd = q_s.shape

    @pl.when(pl.program_id(0) == 0)
    def _():
        st_ref[...] = jnp.zeros_like(st_ref)

    lb = lb_ref[...]
    f = lb + (1.0 - lb) * _sigmoid(zf_ref[...])
    lf = jnp.log(f)
    lf_hi = lf.astype(BF16)
    lf_lo = (lf - lf_hi.astype(F32)).astype(BF16)
    tri = tri_ref[...]
    cum = (jnp.dot(tri, lf_hi, preferred_element_type=F32) + jnp.dot(tri, lf_lo, preferred_element_type=F32))
    cum3 = cum.reshape(t_len // c_len, c_len, heads * hd)
    last3 = cum3[:, c_len - 1:c_len, :]
    last = jnp.broadcast_to(last3, cum3.shape).reshape(t_len, heads * hd)
    dec = jnp.exp(last3.reshape(t_len // c_len, heads * hd))
    q = _silu(zq_ref[...])
    k = 1.0 - f
    qd = (q * jnp.exp(cum)).astype(BF16)
    kd = (k * jnp.exp(last - cum)).astype(BF16)
    v = zi_ref[...]
    cum2 = cum * math.log2(math.e)
    for h in range(heads):
        sl = slice(h * hd, (h + 1) * hd)
        q_s[h] = q[:, sl]
        k_s[h] = k[:, sl]
        v_s[h] = v[:, sl]
        cum_s[h] = cum2[:, sl]
        dec_s[h] = dec[:, sl]
        qd_s[h] = qd[:, sl]
        kd_s[h] = kd[:, sl]

    half = c_len // 2
    row = lax.broadcasted_iota(jnp.int32, (half, hd), 0)
    lane = lax.broadcasted_iota(jnp.int32, (half, hd), 1)

    def body(c, carry):
        r0 = pl.multiple_of(c * c_len, c_len)
        rows = pl.ds(r0, c_len)
        for h in range(heads):
            q_c = q_s[h, rows, :]
            k_c = k_s[h, rows, :]
            cm = cum_s[h, rows, :]
            m_lo = jnp.zeros((half, hd), F32)
            m_hi = jnp.zeros((half, hd), F32)
            for s in range(c_len):
                ks = k_c[s:s + 1, :]
                cs = cm[s:s + 1, :]
                if s < half:
                    w = q_c[:half] * (ks * jnp.exp2(jnp.minimum(cm[:half] - cs, 0.0)))
                    m_lo = jnp.where(lane == s, jnp.sum(w, axis=1, keepdims=True), m_lo)
                w = q_c[half:] * (ks * jnp.exp2(jnp.minimum(cm[half:] - cs, 0.0)))
                m_hi = jnp.where(lane == s, jnp.sum(w, axis=1, keepdims=True), m_hi)
            m_lo = jnp.where(row >= lane, m_lo, 0.0)
            m_hi = jnp.where(row + half >= lane, m_hi, 0.0)
            scores = jnp.concatenate([m_lo, m_hi], axis=0)[:, :c_len].astype(BF16)
            v_c = v_s[h, rows, :].astype(BF16)
            st = st_ref[h]
            o = (jnp.dot(scores, v_c, preferred_element_type=F32)
                 + lax.dot_general(qd_s[h, rows, :], st.astype(BF16), (((1,), (1,)), ((), ())),
                                   preferred_element_type=F32))
            o_s[h, rows, :] = o
            upd = lax.dot_general(v_c, kd_s[h, rows, :], (((0,), (0,)), ((), ())),
                                  preferred_element_type=F32)
            st_ref[h] = st * dec_s[h, pl.ds(c, 1), :] + upd
        return carry

    lax.fori_loop(0, t_len // c_len, body, 0, unroll=8)
    ng = ng_ref[...]
    sg = _silu(zg_ref[...])
    for h in range(heads):
        sl = slice(h * hd, (h + 1) * hd)
        o = o_s[h]
        o = o * lax.rsqrt(jnp.mean(o * o, axis=-1, keepdims=True) + EPS)
        o_ref[:, sl] = (o * ng[:, sl] * sg[:, sl]).astype(BF16)


def hgrn2(z, col0, lb, norm_g, casts=()):
    s = z.shape[0]
    t_len = min(HGRN_T, s)
    heads = W_MIX // A_HEAD_DIM
    cb = col0 // W_MIX
    r = jnp.arange(t_len)
    same = r[:, None] // HGRN_C == r[None, :] // HGRN_C
    tri = (same & (r[None, :] <= r[:, None])).astype(BF16)

    def zspec(k):
        return pl.BlockSpec((t_len, W_MIX), lambda i: (i, cb + k))

    vec = pl.BlockSpec((1, W_MIX), lambda i: (0, 0))
    sq = pl.BlockSpec((t_len, t_len), lambda i: (0, 0))
    per_head = lambda dt: pltpu.VMEM((heads, t_len, A_HEAD_DIM), dt)
    return _mixer_call(
        _hgrn2_kernel, s // t_len, lambda i: i,
        grid=(s // t_len,),
        in_specs=[zspec(0), zspec(1), zspec(2), zspec(3), vec, vec, sq],
        out_spec=pl.BlockSpec((t_len, W_MIX), lambda i: (i, 0)),
        out_shape=jax.ShapeDtypeStruct((s, W_MIX), BF16),
        scratch_shapes=[pltpu.VMEM((heads, A_HEAD_DIM, A_HEAD_DIM), F32)]
        + [per_head(F32)] * 5 + [pltpu.VMEM((heads, t_len // HGRN_C, A_HEAD_DIM), F32)] + [per_head(BF16)] * 2,
        sem=("arbitrary",), name="hgrn2",
        args=(z, z, z, z, lb.reshape(1, W_MIX), norm_g.reshape(1, W_MIX), tri), casts=casts)


def _s5_kernel(u_ref, wb_ref, pw_ref, wc_ref, d_ref, gw_ref, gb_ref, o_ref,
               carry_ref, bu_s, y_s):
    i = pl.program_id(0)
    j = pl.program_id(1)
    n_slab = pl.num_programs(1)
    t_len = bu_s.shape[0]
    half = bu_s.shape[1] // 2

    @pl.when(i == 0)
    def _():
        carry_ref[j] = jnp.zeros((SUBLANES, 2 * half), F32)

    u = u_ref[...]
    bu_s[...] = jnp.dot(u.astype(BF16), wb_ref[0], preferred_element_type=F32)
    p8_re, p8_im = pw_ref[0, 0:8, :half], pw_ref[0, 0:8, half:]

    def body(b, carry):
        c_re, c_im = carry
        r0 = pl.multiple_of(b * SUBLANES, SUBLANES)
        blk = bu_s[pl.ds(r0, SUBLANES), :]
        x_re, x_im = blk[:, :half], blk[:, half:]
        for step, k in enumerate((1, 2, 4)):
            a_re = pw_ref[0, 8 * (step + 1):8 * (step + 2), :half]
            a_im = pw_ref[0, 8 * (step + 1):8 * (step + 2), half:]
            s_re = pltpu.roll(x_re, k, 0)
            s_im = pltpu.roll(x_im, k, 0)
            x_re, x_im = (x_re + a_re * s_re - a_im * s_im,
                          x_im + a_re * s_im + a_im * s_re)
        x_re, x_im = (x_re + p8_re * c_re - p8_im * c_im,
                      x_im + p8_re * c_im + p8_im * c_re)
        bu_s[pl.ds(r0, SUBLANES), :] = jnp.concatenate([x_re, x_im], axis=1)
        n_re = jnp.broadcast_to(x_re[SUBLANES - 1:SUBLANES, :], x_re.shape)
        n_im = jnp.broadcast_to(x_im[SUBLANES - 1:SUBLANES, :], x_im.shape)
        return n_re, n_im

    c0 = carry_ref[j]
    c_re, c_im = lax.fori_loop(0, t_len // SUBLANES, body, (c0[:, :half], c0[:, half:]), unroll=4)
    carry_ref[j] = jnp.concatenate([c_re, c_im], axis=1)

    y = jnp.dot(bu_s[...].astype(BF16), wc_ref[0], preferred_element_type=F32)
    y = _gelu_tanh(y + d_ref[0] * u)
    y_s[j] = y

    @pl.when(j == n_slab - 1)
    def _():
        yf = jnp.concatenate([y_s[s] for s in range(y_s.shape[0])], axis=1)
        gate = jnp.dot(yf.astype(BF16), gw_ref[...], preferred_element_type=F32) + gb_ref[...]
        o_ref[...] = (yf * _sigmoid(gate)).astype(BF16)


def s5(z, col0, lam_re, lam_im, log_dt, b_re, b_im, c_re, c_im, d_skip, glu_w, glu_b, casts=()):
    s = z.shape[0]
    t_len = min(S5_T, s)
    groups, n_state = lam_re.shape
    gps = S5_SLAB // S5_GROUP
    n_slab = groups // gps
    half = gps * n_state
    cb = col0 // S5_SLAB
    lam_re = lam_re.astype(F32)
    lam_im = lam_im.astype(F32)
    dt = jnp.exp(log_dt.astype(F32))[:, None]
    mag = jnp.exp(lam_re * dt)
    ang = lam_im * dt
    ab_re = mag * jnp.cos(ang)
    ab_im = mag * jnp.sin(ang)
    den = lam_re * lam_re + lam_im * lam_im
    num_re = ab_re - 1.0
    coef_re = (num_re * lam_re + ab_im * lam_im) / den
    coef_im = (ab_im * lam_re - num_re * lam_im) / den
    br = b_re.astype(F32)
    bi = b_im.astype(F32)
    bb_re = coef_re[..., None] * br - coef_im[..., None] * bi
    bb_im = coef_re[..., None] * bi + coef_im[..., None] * br
    eye = jnp.eye(gps, dtype=F32)

    def blockdiag_in(bb):
        t = bb.reshape(n_slab, gps, n_state, S5_GROUP)
        return jnp.einsum('sgnp,gh->sgphn', t, eye).reshape(n_slab, gps * S5_GROUP, gps * n_state)

    wb = jnp.concatenate([blockdiag_in(bb_re), blockdiag_in(bb_im)], axis=-1).astype(BF16)

    def blockdiag_out(cc):
        t = cc.reshape(n_slab, gps, S5_GROUP, n_state)
        return jnp.einsum('sgpn,gh->sgnhp', t, eye).reshape(n_slab, gps * n_state, gps * S5_GROUP)

    wc = jnp.concatenate([blockdiag_out(c_re.astype(F32)), -blockdiag_out(c_im.astype(F32))],
                         axis=1).astype(BF16)
    r8 = jnp.arange(SUBLANES)
    expo = jnp.concatenate([r8 + 1.0] + [jnp.full((SUBLANES,), float(k)) for k in (1, 2, 4)]).astype(F32)
    keep = jnp.concatenate([jnp.ones((SUBLANES,), F32)] + [(r8 >= k).astype(F32) for k in (1, 2, 4)])
    expo = expo[:, None, None]
    p_mag = jnp.exp(expo * (lam_re * dt)[None]) * keep[:, None, None]
    p_re = (p_mag * jnp.cos(expo * ang[None])).reshape(S5_PW_ROWS, n_slab, half)
    p_im = (p_mag * jnp.sin(expo * ang[None])).reshape(S5_PW_ROWS, n_slab, half)
    pw = jnp.moveaxis(jnp.concatenate([p_re, p_im], axis=-1), 1, 0)

    return _mixer_call(
        _s5_kernel, (s // t_len) * n_slab, lambda i, j: i * n_slab + j,
        grid=(s // t_len, n_slab),
        in_specs=[
            pl.BlockSpec((t_len, S5_SLAB), lambda i, j: (i, cb + j)),
            pl.BlockSpec((1, S5_SLAB, 2 * half), lambda i, j: (j, 0, 0)),
            pl.BlockSpec((1, S5_PW_ROWS, 2 * half), lambda i, j: (j, 0, 0)),
            pl.BlockSpec((1, 2 * half, S5_SLAB), lambda i, j: (j, 0, 0)),
            pl.BlockSpec((1, 1, S5_SLAB), lambda i, j: (j, 0, 0)),
            pl.BlockSpec((W_MIX, W_MIX), lambda i, j: (0, 0)),
            pl.BlockSpec((1, W_MIX), lambda i, j: (0, 0)),
        ],
        out_spec=pl.BlockSpec((t_len, W_MIX), lambda i, j: (i, 0)),
        out_shape=jax.ShapeDtypeStruct((s, W_MIX), BF16),
        scratch_shapes=[pltpu.VMEM((n_slab, SUBLANES, 2 * half), F32),
                        pltpu.VMEM((t_len, 2 * half), F32),
                        pltpu.VMEM((n_slab, t_len, S5_SLAB), F32)],
        sem=("arbitrary", "arbitrary"), name="s5",
        args=(z, wb, pw, wc, d_skip.astype(F32).reshape(n_slab, 1, S5_SLAB), glu_w.astype(BF16),
              glu_b.astype(F32).reshape(1, W_MIX)), casts=casts)


def _retention_kernel(zq_ref, zk_ref, zv_ref, zg_ref, cos_ref, sin_ref, dmat_ref, qdec_ref, kdec_ref,
                      cdec_ref, ng_ref, o_ref, st_ref):
    @pl.when(pl.program_id(0) == 0)
    def _():
        st_ref[...] = jnp.zeros_like(st_ref)

    heads, hd, _ = st_ref.shape
    half = hd // 2
    cos = cos_ref[...]
    sin = sin_ref[...]

    def rope(t):
        t1, t2 = t[:, :half], t[:, half:]
        return jnp.concatenate([t1 * cos - t2 * sin, t1 * sin + t2 * cos], axis=1)

    for h in range(heads):
        sl = slice(h * hd, (h + 1) * hd)
        q = rope(zq_ref[:, sl])
        k = rope(zk_ref[:, sl]) * hd ** -0.5
        v = zv_ref[:, sl].astype(BF16)
        qb = q.astype(BF16)
        scores = lax.dot_general(qb, k.astype(BF16), (((1,), (1,)), ((), ())),
                                 preferred_element_type=F32) * dmat_ref[h]
        st = st_ref[h]
        o = (jnp.dot(scores.astype(BF16), v, preferred_element_type=F32)
             + jnp.dot(qb, st.astype(BF16), preferred_element_type=F32) * qdec_ref[h])
        kd = (k * kdec_ref[h]).astype(BF16)
        st_ref[h] = cdec_ref[h] * st + lax.dot_general(kd, v, (((0,), (0,)), ((), ())),
                                                       preferred_element_type=F32)
        mu = jnp.mean(o, axis=-1, keepdims=True)
        oc = o - mu
        var = jnp.mean(oc * oc, axis=-1, keepdims=True)
        o = oc * lax.rsqrt(var + EPS)
        o_ref[:, sl] = (o * ng_ref[:, sl] * _silu(zg_ref[:, sl])).astype(BF16)


def retention(z, col0, norm_g, casts=()):
    s = z.shape[0]
    t_len = min(RET_T, s)
    hd = W_MIX // C_HEADS
    pos = jnp.arange(s, dtype=F32)
    inv_freq = ROPE_THETA ** (-jnp.arange(0, hd, 2, dtype=F32) / hd)
    ang = pos[:, None] * inv_freq[None, :]
    cos = jnp.cos(ang)
    sin = jnp.sin(ang)
    log_gamma = jnp.log(1.0 - 2.0 ** (-5.0 - jnp.arange(C_HEADS, dtype=F32)))
    idx = jnp.arange(t_len, dtype=F32)
    rel = idx[:, None] - idx[None, :]
    dmat = jnp.where(rel[None] >= 0, jnp.exp(jnp.maximum(rel, 0.0)[None] * log_gamma[:, None, None]), 0.0)
    qdec = jnp.exp((idx + 1.0)[None, :] * log_gamma[:, None])[..., None]
    kdec = jnp.exp((t_len - 1.0 - idx)[None, :] * log_gamma[:, None])[..., None]
    cdec = jnp.broadcast_to(jnp.exp(t_len * log_gamma)[:, None, None], (C_HEADS, 1, hd))

    cb = col0 // W_MIX

    def zspec(k):
        return pl.BlockSpec((t_len, W_MIX), lambda i: (i, cb + k))

    tab = pl.BlockSpec((t_len, hd // 2), lambda i: (i, 0))
    full = lambda shape: pl.BlockSpec(shape, lambda i: (0,) * len(shape))
    return _mixer_call(
        _retention_kernel, s // t_len, lambda i: i,
        grid=(s // t_len,),
        in_specs=[zspec(0), zspec(1), zspec(2), zspec(3), tab, tab,
                  full((C_HEADS, t_len, t_len)), full((C_HEADS, t_len, 1)), full((C_HEADS, t_len, 1)),
                  full((C_HEADS, 1, hd)), full((1, W_MIX))],
        out_spec=pl.BlockSpec((t_len, W_MIX), lambda i: (i, 0)),
        out_shape=jax.ShapeDtypeStruct((s, W_MIX), BF16),
        scratch_shapes=[pltpu.VMEM((C_HEADS, hd, hd), F32)],
        sem=("arbitrary",), name="retention",
        args=(z, z, z, z, cos, sin, dmat, qdec, kdec, cdec, norm_g.reshape(1, W_MIX)), casts=casts)


def _rglru_kernel(zg_ref, zx_ref, cw_ref, cb_ref, wa_ref, ba_ref, wx_ref, bx_ref, sp_ref, o_ref,
                  xbuf, h_ref, a_s, u_s):
    t_len = zx_ref.shape[0]

    @pl.when(pl.program_id(0) == 0)
    def _():
        xbuf[0:SUBLANES, :] = jnp.zeros((SUBLANES, W_MIX), F32)
        h_ref[...] = jnp.zeros_like(h_ref)

    xbuf[SUBLANES:, :] = zx_ref[...]
    xc = cb_ref[...]
    for tap in range(CONV_WIDTH):
        off = SUBLANES - (CONV_WIDTH - 1) + tap
        xc = xc + xbuf[off:off + t_len, :] * cw_ref[tap:tap + 1, :]
    xbuf[0:SUBLANES, :] = xbuf[t_len:t_len + SUBLANES, :]
    xcb = xc.astype(BF16)
    n_blk = W_MIX // D_BLOCK
    pre_r = jnp.concatenate(
        [jnp.dot(xcb[:, b * D_BLOCK:(b + 1) * D_BLOCK], wa_ref[b], preferred_element_type=F32)
         for b in range(n_blk)], axis=1)
    pre_i = jnp.concatenate(
        [jnp.dot(xcb[:, b * D_BLOCK:(b + 1) * D_BLOCK], wx_ref[b], preferred_element_type=F32)
         for b in range(n_blk)], axis=1)
    r = _sigmoid(pre_r + ba_ref[...])
    gi = _sigmoid(pre_i + bx_ref[...])
    log_a = -RG_C * r * sp_ref[...]
    a = jnp.exp(log_a)
    a_s[...] = a
    u_s[...] = jnp.sqrt(1.0 - a * a) * (gi * xc)
    row = lax.broadcasted_iota(jnp.int32, (SUBLANES, W_MIX), 0)

    def body(b, h):
        r0 = pl.multiple_of(b * SUBLANES, SUBLANES)
        aa = a_s[pl.ds(r0, SUBLANES), :]
        uu = u_s[pl.ds(r0, SUBLANES), :]
        for k in (1, 2, 4):
            us = jnp.where(row >= k, pltpu.roll(uu, k, 0), 0.0)
            as_ = jnp.where(row >= k, pltpu.roll(aa, k, 0), 1.0)
            uu = uu + aa * us
            aa = aa * as_
        hh = uu + aa * h
        u_s[pl.ds(r0, SUBLANES), :] = hh
        return jnp.broadcast_to(hh[SUBLANES - 1:SUBLANES, :], hh.shape)

    h_ref[...] = lax.fori_loop(0, t_len // SUBLANES, body, h_ref[...], unroll=4)
    o_ref[...] = (_gelu_tanh(zg_ref[...]) * u_s[...]).astype(BF16)


def rglru(z, col0, conv_w, conv_b, w_a, b_a, w_x, b_x, lam, casts=()):
    s = z.shape[0]
    t_len = min(RG_T, s)
    cb = col0 // W_MIX
    sp = jax.nn.softplus(-lam.astype(F32)).reshape(1, W_MIX)
    row = lambda a: a.astype(F32).reshape(1, W_MIX)
    full = lambda shape: pl.BlockSpec(shape, lambda i: (0,) * len(shape))
    n_blk = W_MIX // D_BLOCK
    return _mixer_call(
        _rglru_kernel, s // t_len, lambda i: i,
        grid=(s // t_len,),
        in_specs=[pl.BlockSpec((t_len, W_MIX), lambda i: (i, cb)),
                  pl.BlockSpec((t_len, W_MIX), lambda i: (i, cb + 1)),
                  full((CONV_WIDTH, W_MIX)), full((1, W_MIX)),
                  full((n_blk, D_BLOCK, D_BLOCK)), full((1, W_MIX)),
                  full((n_blk, D_BLOCK, D_BLOCK)), full((1, W_MIX)), full((1, W_MIX))],
        out_spec=pl.BlockSpec((t_len, W_MIX), lambda i: (i, 0)),
        out_shape=jax.ShapeDtypeStruct((s, W_MIX), BF16),
        scratch_shapes=[pltpu.VMEM((t_len + SUBLANES, W_MIX), F32),
                        pltpu.VMEM((SUBLANES, W_MIX), F32),
                        pltpu.VMEM((t_len, W_MIX), F32),
                        pltpu.VMEM((t_len, W_MIX), F32)],
        sem=("arbitrary",), name="rglru",
        args=(z, z, conv_w.astype(F32), row(conv_b), w_a.astype(BF16), row(b_a), w_x.astype(BF16), row(b_x), sp),
        casts=casts)


def kernel(x, norm_mix_g, norm_ffn_g, final_norm_g, w_in, w_out, hgrn_lb_logits, hgrn_norm_g, s5_lambda_re, s5_lambda_im, s5_log_dt, s5_b_re, s5_b_im, s5_c_re, s5_c_im, s5_d, s5_glu_w, s5_glu_b, ret_norm_g, rg_conv_w, rg_conv_b, rg_w_a, rg_b_a, rg_w_x, rg_b_x, rg_lambda, ffn_w1, ffn_w3, ffn_w2, router_w, moe_w1, moe_w3, moe_w2):
    b_, s_, d_ = x.shape
    depth = w_in.shape[0]
    xs = x.reshape(b_ * s_, d_).astype(F32)
    lb_p = jax.nn.softmax(hgrn_lb_logits.astype(F32), axis=0)
    lb_all = jnp.cumsum(lb_p, axis=0) - lb_p[0]
    col_a, col_b, col_c, col_d = 0, 4 * W_MIX, 5 * W_MIX, 9 * W_MIX
    w_in_b = w_in[0].astype(BF16)
    for layer in range(depth):
        m = layer // 2
        dense = layer % 2 == 0
        ch = (ffn_w1, ffn_w3, ffn_w2) if dense else (moe_w1, moe_w3, moe_w2)
        nxt = ((w_in, layer + 1),) if layer + 1 < depth else ()
        z, _ = norm_matmul(xs, norm_mix_g[layer], w_in_b)
        o_a, (w1_b,) = hgrn2(z, col_a, lb_all[layer], hgrn_norm_g[layer], casts=((ch[0], m),))
        o_b, (w3_b, w2_b) = s5(z, col_b, s5_lambda_re[layer], s5_lambda_im[layer], s5_log_dt[layer],
                               s5_b_re[layer], s5_b_im[layer], s5_c_re[layer], s5_c_im[layer],
                               s5_d[layer], s5_glu_w[layer], s5_glu_b[layer], casts=((ch[1], m), (ch[2], m)))
        o_c, _ = retention(z, col_c, ret_norm_g[layer])
        o_d, rest = rglru(z, col_d, rg_conv_w[layer], rg_conv_b[layer], rg_w_a[layer], rg_b_a[layer],
                          rg_w_x[layer], rg_b_x[layer], rg_lambda[layer], casts=((w_out, layer),) + nxt)
        xs, _ = out_proj((o_a, o_b, o_c, o_d), rest[0], xs)
        if nxt:
            w_in_b = rest[1]
        if dense:
            xs = ffn(xs, norm_ffn_g[layer], w1_b, w3_b, w2_b)
        else:
            last = layer == depth - 1
            xs = moe(xs, norm_ffn_g[layer], router_w[m], w1_b, w3_b, w2_b, final_norm_g if last else None)
            if last:
                return xs.reshape(b_, s_, d_)
    return final_norm(xs, final_norm_g).reshape(b_, s_, d_)
```

```python
import functools
import math

import jax
import jax.numpy as jnp
from jax import lax
from jax.experimental import pallas as pl
from jax.experimental.pallas import tpu as pltpu

F32 = jnp.float32
BF16 = jnp.bfloat16
EPS = 1e-6

V7X_VMEM_BYTES = 64 * 1024 * 1024
VMEM_LIMIT_BYTES = V7X_VMEM_BYTES - 4 * 1024 * 1024
SUBLANES = 8
LANES = 128

A_HEAD_DIM = 128
S5_GROUP = 16
S5_STATE = 64
C_HEADS = 4
ROPE_THETA = 10000.0
D_BLOCK = 128
CONV_WIDTH = 4
RG_C = 8.0
N_EXPERTS = 8
TOP_K = 2
W_MIX = 1024

ROW_TILE = 512
IN_PROJ_TN = 1024
OUT_PROJ_TM = 1024
OUT_PROJ_TN = 1024
FFN_TF = 512
MOE_TM = 512
COMBINE_ROWS = 256
HGRN_T = 256
HGRN_C = 16
RET_T = 256
S5_T = 1024
S5_SLAB = 128
S5_PW_ROWS = 4 * SUBLANES
RG_T = 512
CAST_ROW_ALIGN = 16
CAST_INLINE_ELEMS = 256 * 1024


def _cparams(*sem):
    return pltpu.CompilerParams(dimension_semantics=sem, vmem_limit_bytes=VMEM_LIMIT_BYTES)


def _rms(xf, g):
    return xf * lax.rsqrt(jnp.mean(xf * xf, axis=-1, keepdims=True) + EPS) * g


def _sigmoid(x):
    return 0.5 + 0.5 * jnp.tanh(0.5 * x)


def _silu(x):
    h = 0.5 * x
    return h + h * jnp.tanh(h)


def _gelu_tanh(x):
    c = math.sqrt(2.0 / math.pi)
    return 0.5 * x * (1.0 + jnp.tanh(c * (x + 0.044715 * (x * x * x))))


def _cast_specs(casts, n_steps, step_index):
    in_specs, out_specs, shapes = [], [], []
    for w, lead in casts:
        *mid, rows, cols = w.shape[1:]
        per = n_steps // math.prod(mid)
        assert per * math.prod(mid) == n_steps, (w.shape, n_steps)
        nr = next(n for n in range(per, 0, -1)
                  if per % n == 0 and rows % (n * CAST_ROW_ALIGN) == 0 and cols % (per // n * LANES) == 0)
        nc = per // nr
        blk = (1,) * len(mid) + (rows // nr, cols // nc)

        def pos(*g, mid=tuple(mid), nr=nr, nc=nc):
            t = step_index(*g)
            idx = [(t // nc) % nr, t % nc]
            t = t // (nr * nc)
            for size in reversed(mid):
                idx.insert(0, t % size)
                t = t // size
            return tuple(idx)

        in_specs.append(pl.BlockSpec((1,) + blk, lambda *g, pos=pos, lead=lead: (lead,) + pos(*g)))
        out_specs.append(pl.BlockSpec(blk, pos))
        shapes.append(jax.ShapeDtypeStruct(w.shape[1:], BF16))
    return in_specs, out_specs, shapes


def _cast_blocks(src_refs, dst_refs):
    for src, dst in zip(src_refs, dst_refs):
        rows, cols = dst.shape[-2:]
        lead = (0,) * (len(dst.shape) - 2)
        if rows * cols <= CAST_INLINE_ELEMS:
            dst[...] = src[0].astype(BF16)
        else:
            def body(c, carry, src=src, dst=dst, lead=lead):
                sl = pl.ds(pl.multiple_of(c * CAST_ROW_ALIGN, CAST_ROW_ALIGN), CAST_ROW_ALIGN)
                dst[lead + (sl, slice(None))] = src[(0,) + lead + (sl, slice(None))].astype(BF16)
                return carry

            lax.fori_loop(0, rows // CAST_ROW_ALIGN, body, 0)


def _with_casts(body, n_in, n_cast, cast_when):
    def kern(*refs):
        ins = refs[:n_in]
        cast_in = refs[n_in:n_in + n_cast]
        out = refs[n_in + n_cast]
        cast_out = refs[n_in + n_cast + 1:n_in + 2 * n_cast + 1]
        scratch = refs[n_in + 2 * n_cast + 1:]
        body(*ins, out, *scratch)
        if cast_when is None or not n_cast:
            _cast_blocks(cast_in, cast_out)
        else:
            pl.when(cast_when())(lambda: _cast_blocks(cast_in, cast_out))

    return kern


def _mixer_call(body, n_steps, step_index, grid, in_specs, out_spec, out_shape, scratch_shapes, sem, name,
                args, casts, cast_when=None):
    cin, cout, cshapes = _cast_specs(casts, n_steps, step_index)
    outs = pl.pallas_call(
        _with_casts(body, len(in_specs), len(casts), cast_when),
        out_shape=[out_shape] + cshapes,
        grid=grid,
        in_specs=list(in_specs) + cin,
        out_specs=[out_spec] + cout,
        scratch_shapes=scratch_shapes,
        compiler_params=_cparams(*sem),
        name=name,
    )(*args, *[w for w, _ in casts])
    return outs[0], tuple(outs[1:])


def _norm_matmul_kernel(x_ref, g_ref, w_ref, o_ref, h_ref):
    @pl.when(pl.program_id(1) == 0)
    def _():
        h_ref[...] = _rms(x_ref[...], g_ref[...]).astype(BF16)

    o_ref[...] = jnp.dot(h_ref[...], w_ref[...], preferred_element_type=F32)


def norm_matmul(x, g, w, casts=()):
    s, d = x.shape
    n = w.shape[1]
    tm, tn = min(ROW_TILE, s), IN_PROJ_TN
    n_col = n // tn
    cast_cols = 1 << (n_col.bit_length() - 1)
    return _mixer_call(
        _norm_matmul_kernel, (s // tm) * cast_cols, lambda i, j: i * cast_cols + jnp.minimum(j, cast_cols - 1),
        grid=(s // tm, n_col),
        in_specs=[
            pl.BlockSpec((tm, d), lambda i, j: (i, 0)),
            pl.BlockSpec((1, d), lambda i, j: (0, 0)),
            pl.BlockSpec((d, tn), lambda i, j: (0, j)),
        ],
        out_spec=pl.BlockSpec((tm, tn), lambda i, j: (i, j)),
        out_shape=jax.ShapeDtypeStruct((s, n), F32),
        scratch_shapes=[pltpu.VMEM((tm, d), BF16)],
        sem=("arbitrary", "arbitrary"), name="norm_in_proj",
        args=(x, g.reshape(1, d), w), casts=casts, cast_when=lambda: pl.program_id(1) < cast_cols)


def _out_proj_kernel(oa_ref, ob_ref, oc_ref, od_ref, w_ref, x_ref, o_ref):
    acc = x_ref[...]
    for idx, r in enumerate((oa_ref, ob_ref, oc_ref, od_ref)):
        acc = acc + jnp.dot(r[...], w_ref[idx * W_MIX:(idx + 1) * W_MIX, :], preferred_element_type=F32)
    o_ref[...] = acc


def out_proj(parts, w, x, casts=()):
    s, d = x.shape
    tm, tn = min(OUT_PROJ_TM, s), OUT_PROJ_TN
    n_col = d // tn
    part_spec = pl.BlockSpec((tm, W_MIX), lambda i, j: (i, 0))
    return _mixer_call(
        _out_proj_kernel, (s // tm) * n_col, lambda i, j: i * n_col + j,
        grid=(s // tm, n_col),
        in_specs=[part_spec, part_spec, part_spec, part_spec,
                  pl.BlockSpec((4 * W_MIX, tn), lambda i, j: (0, j)),
                  pl.BlockSpec((tm, tn), lambda i, j: (i, j))],
        out_spec=pl.BlockSpec((tm, tn), lambda i, j: (i, j)),
        out_shape=jax.ShapeDtypeStruct((s, d), F32),
        scratch_shapes=[], sem=("arbitrary", "arbitrary"), name="out_proj",
        args=(*parts, w, x), casts=casts)


def _swiglu_step(h, w1, w3, w2):
    a = jnp.dot(h, w1, preferred_element_type=F32)
    b = jnp.dot(h, w3, preferred_element_type=F32)
    act = (_silu(a) * b).astype(BF16)
    return jnp.dot(act, w2, preferred_element_type=F32)


def _ffn_kernel(x_ref, g_ref, w1_ref, w3_ref, w2_ref, o_ref, h_ref):
    @pl.when(pl.program_id(1) == 0)
    def _():
        xf = x_ref[...]
        h_ref[...] = _rms(xf, g_ref[...]).astype(BF16)
        o_ref[...] = xf

    o_ref[...] += _swiglu_step(h_ref[...], w1_ref[...], w3_ref[...], w2_ref[...])


def ffn(x, g, w1, w3, w2):
    s, d = x.shape
    dff = w1.shape[1]
    tm, tf = min(ROW_TILE, s), FFN_TF
    return pl.pallas_call(
        _ffn_kernel,
        out_shape=jax.ShapeDtypeStruct((s, d), F32),
        grid=(s // tm, dff // tf),
        in_specs=[
            pl.BlockSpec((tm, d), lambda i, f: (i, 0), pipeline_mode=pl.Buffered(1)),
            pl.BlockSpec((1, d), lambda i, f: (0, 0)),
            pl.BlockSpec((d, tf), lambda i, f: (0, f)),
            pl.BlockSpec((d, tf), lambda i, f: (0, f)),
            pl.BlockSpec((tf, d), lambda i, f: (f, 0)),
        ],
        out_specs=pl.BlockSpec((tm, d), lambda i, f: (i, 0), pipeline_mode=pl.Buffered(1)),
        scratch_shapes=[pltpu.VMEM((tm, d), BF16)],
        compiler_params=_cparams("parallel", "arbitrary"),
        name="ffn_swiglu",
    )(x, g.reshape(1, d), w1, w3, w2)


def _router_kernel(x_ref, g_ref, whi_ref, wlo_ref, comb_ref):
    h = _rms(x_ref[...], g_ref[...])
    h_hi = h.astype(BF16)
    h_lo = (h - h_hi.astype(F32)).astype(BF16)
    logits = (jnp.dot(h_hi, whi_ref[...], preferred_element_type=F32)
              + (jnp.dot(h_lo, whi_ref[...], preferred_element_type=F32)
                 + jnp.dot(h_hi, wlo_ref[...], preferred_element_type=F32)))
    lane = lax.broadcasted_iota(jnp.int32, logits.shape, 1)
    neg = jnp.float32(-jnp.inf)
    logits = jnp.where(lane < N_EXPERTS, logits, neg)
    v1 = jnp.max(logits, axis=-1, keepdims=True)
    i1 = jnp.min(jnp.where(logits == v1, lane, LANES), axis=-1, keepdims=True)
    rest = jnp.where(lane == i1, neg, logits)
    v2 = jnp.max(rest, axis=-1, keepdims=True)
    i2 = jnp.min(jnp.where(rest == v2, lane, LANES), axis=-1, keepdims=True)
    e2 = jnp.exp(v2 - v1)
    g1 = 1.0 / (1.0 + e2)
    g2 = e2 / (1.0 + e2)
    comb_ref[...] = (jnp.where(lane == 0, i1.astype(F32), 0.0) + jnp.where(lane == 1, i2.astype(F32), 0.0)
                     + jnp.where(lane == 2, g1, 0.0) + jnp.where(lane == 3, g2, 0.0))


def router(x, g, router_w):
    s, d = x.shape
    tm = min(ROW_TILE, s)
    wr = jnp.zeros((d, LANES), F32).at[:, :N_EXPERTS].set(router_w.astype(F32))
    w_hi = wr.astype(BF16)
    w_lo = (wr - w_hi.astype(F32)).astype(BF16)
    return pl.pallas_call(
        _router_kernel,
        out_shape=jax.ShapeDtypeStruct((s, LANES), F32),
        grid=(s // tm,),
        in_specs=[pl.BlockSpec((tm, d), lambda i: (i, 0)),
                  pl.BlockSpec((1, d), lambda i: (0, 0)),
                  pl.BlockSpec((d, LANES), lambda i: (0, 0)),
                  pl.BlockSpec((d, LANES), lambda i: (0, 0))],
        out_specs=pl.BlockSpec((tm, LANES), lambda i: (i, 0)),
        compiler_params=_cparams("parallel"),
        name="moe_router",
    )(x, g.reshape(1, d), w_hi, w_lo)


def _row_copy(src_hbm, src_row, dst_ref, dst_row, sem):
    return pltpu.make_async_copy(src_hbm.at[pl.ds(src_row, 1)], dst_ref.at[pl.ds(dst_row, 1)], sem)


def _moe_grouped_kernel(te_ref, nu_ref, src_ref, x_hbm, g_ref, w1_ref, w3_ref, w2_ref, o_ref,
                        xbuf, h_ref, sem):
    i = pl.program_id(0)
    f = pl.program_id(1)
    n_used = nu_ref[0]
    used = i < n_used
    tm = xbuf.shape[0]

    def start_gather(tile):
        def issue(r, c):
            _row_copy(x_hbm, src_ref[tile * tm + r], xbuf, r, sem).start()
            return c

        lax.fori_loop(0, tm, issue, 0, unroll=8)

    @pl.when((i == 0) & (f == 0))
    def _():
        start_gather(0)

    @pl.when(jnp.logical_not(used) & (f == 0))
    def _():
        o_ref[...] = jnp.zeros_like(o_ref)

    @pl.when(used & (f == 0))
    def _():
        def wait(r, c):
            _row_copy(x_hbm, 0, xbuf, r, sem).wait()
            return c

        lax.fori_loop(0, tm, wait, 0, unroll=8)
        h_ref[...] = _rms(xbuf[...], g_ref[...]).astype(BF16)
        o_ref[...] = _swiglu_step(h_ref[...], w1_ref[0], w3_ref[0], w2_ref[0])

    @pl.when((f == 1) & (i + 1 < n_used))
    def _():
        start_gather(i + 1)

    @pl.when(used & (f > 0))
    def _():
        o_ref[...] += _swiglu_step(h_ref[...], w1_ref[0], w3_ref[0], w2_ref[0])


def moe_grouped(x, src, g, tile_expert, n_used, w1, w3, w2):
    d = x.shape[1]
    p = src.shape[0]
    dff = w1.shape[2]
    tm, tf = MOE_TM, min(FFN_TF, dff // 2)
    nf = dff // tf

    def f_eff(i, f, nu):
        return jnp.where(i < nu[0], f, nf - 1)

    grid_spec = pltpu.PrefetchScalarGridSpec(
        num_scalar_prefetch=3,
        grid=(p // tm, nf),
        in_specs=[
            pl.BlockSpec(memory_space=pl.ANY),
            pl.BlockSpec((1, d), lambda i, f, te, nu, sr: (0, 0)),
            pl.BlockSpec((1, d, tf), lambda i, f, te, nu, sr: (te[i], 0, f_eff(i, f, nu))),
            pl.BlockSpec((1, d, tf), lambda i, f, te, nu, sr: (te[i], 0, f_eff(i, f, nu))),
            pl.BlockSpec((1, tf, d), lambda i, f, te, nu, sr: (te[i], f_eff(i, f, nu), 0)),
        ],
        out_specs=pl.BlockSpec((tm, d), lambda i, f, te, nu, sr: (i, 0), pipeline_mode=pl.Buffered(1)),
        scratch_shapes=[pltpu.VMEM((tm, d), F32), pltpu.VMEM((tm, d), BF16), pltpu.SemaphoreType.DMA(())],
    )
    return pl.pallas_call(
        _moe_grouped_kernel,
        out_shape=jax.ShapeDtypeStruct((p, d), F32),
        grid_spec=grid_spec,
        compiler_params=_cparams("arbitrary", "arbitrary"),
        name="moe_grouped",
    )(tile_expert, n_used, src, x, g.reshape(1, d), w1, w3, w2)


def _combine_kernel(dest_ref, x_ref, info_ref, y_hbm, *rest, final_norm):
    if final_norm:
        fg_ref, o_ref, buf, sem = rest
    else:
        o_ref, buf, sem = rest
    i = pl.program_id(0)
    n_rows = o_ref.shape[0]
    slot = i % 2

    def start_gather(step, sl):
        def issue(r, c):
            for k in range(TOP_K):
                _row_copy(y_hbm, dest_ref[TOP_K * (step * n_rows + r) + k], buf.at[sl, k], r, sem.at[sl]).start()
            return c

        lax.fori_loop(0, n_rows, issue, 0, unroll=4)

    @pl.when(i == 0)
    def _():
        start_gather(0, 0)

    @pl.when(i + 1 < pl.num_programs(0))
    def _():
        start_gather(i + 1, 1 - slot)

    def wait(r, c):
        for k in range(TOP_K):
            _row_copy(y_hbm, 0, buf.at[slot, k], r, sem.at[slot]).wait()
        return c

    lax.fori_loop(0, n_rows, wait, 0, unroll=4)
    acc = x_ref[...]
    info = info_ref[...]
    for k in range(TOP_K):
        acc = acc + info[:, TOP_K + k:TOP_K + k + 1] * buf[slot, k]
    o_ref[...] = _rms(acc, fg_ref[...]) if final_norm else acc


def combine(x, y, dest, info, final_g=None):
    s, d = x.shape
    rt = COMBINE_ROWS
    final_norm = final_g is not None
    in_specs = [pl.BlockSpec((rt, d), lambda i, dr: (i, 0)), pl.BlockSpec((rt, LANES), lambda i, dr: (i, 0)),
                pl.BlockSpec(memory_space=pl.ANY)]
    args = [x, info, y]
    if final_norm:
        in_specs.append(pl.BlockSpec((1, d), lambda i, dr: (0, 0)))
        args.append(final_g.astype(F32).reshape(1, d))
    grid_spec = pltpu.PrefetchScalarGridSpec(
        num_scalar_prefetch=1,
        grid=(s // rt,),
        in_specs=in_specs,
        out_specs=pl.BlockSpec((rt, d), lambda i, dr: (i, 0)),
        scratch_shapes=[pltpu.VMEM((2, TOP_K, rt, d), F32), pltpu.SemaphoreType.DMA((2,))],
    )
    return pl.pallas_call(
        functools.partial(_combine_kernel, final_norm=final_norm),
        out_shape=jax.ShapeDtypeStruct((s, d), F32),
        grid_spec=grid_spec,
        compiler_params=_cparams("arbitrary"),
        name="moe_combine",
    )(dest, *args)


def _routing_tables(experts, tm, n_tiles):
    n_assign = experts.size
    e_flat = experts.reshape(n_assign)
    onehot = (e_flat[:, None] == jnp.arange(N_EXPERTS, dtype=jnp.int32)[None, :]).astype(jnp.int32)
    csum = jnp.cumsum(onehot, axis=0)
    rank = jnp.sum(csum * onehot, axis=1) - 1
    counts = csum[-1]
    padded = ((counts + tm - 1) // tm) * tm
    seg_end = jnp.cumsum(padded)
    seg_start = seg_end - padded
    dest = (seg_start[e_flat] + rank).astype(jnp.int32)
    p = n_tiles * tm
    src = jnp.zeros((p,), jnp.int32).at[dest].set(jnp.arange(n_assign, dtype=jnp.int32) // TOP_K)
    n_used = (seg_end[-1] // tm).astype(jnp.int32)
    tile_start = jnp.arange(n_tiles, dtype=jnp.int32) * tm
    tile_e = jnp.sum((tile_start[:, None] >= seg_end[None, :]).astype(jnp.int32), axis=1)
    tile_e = jnp.minimum(tile_e, N_EXPERTS - 1)
    last_e = tile_e[n_used - 1]
    tile_e = jnp.where(jnp.arange(n_tiles) < n_used, tile_e, last_e).astype(jnp.int32)
    return src, dest, tile_e, n_used.reshape(1)


def moe(x, g, router_w, w1, w3, w2, final_g=None):
    s, _ = x.shape
    info = router(x, g, router_w)
    experts = info[:, 0:TOP_K].astype(jnp.int32)
    tm = MOE_TM
    n_tiles = (TOP_K * s) // tm + N_EXPERTS
    src, dest, tile_e, n_used = _routing_tables(experts, tm, n_tiles)
    y = moe_grouped(x, src, g, tile_e, n_used, w1.astype(BF16), w3.astype(BF16), w2.astype(BF16))
    return combine(x, y, dest, info, final_g)


def _final_norm_kernel(x_ref, g_ref, o_ref):
    o_ref[...] = _rms(x_ref[...], g_ref[...])


def final_norm(x, g):
    s, d = x.shape
    tm = min(ROW_TILE, s)
    return pl.pallas_call(
        _final_norm_kernel,
        out_shape=jax.ShapeDtypeStruct((s, d), F32),
        grid=(s // tm,),
        in_specs=[pl.BlockSpec((tm, d), lambda i: (i, 0)), pl.BlockSpec((1, d), lambda i: (0, 0))],
        out_specs=pl.BlockSpec((tm, d), lambda i: (i, 0)),
        compiler_params=_cparams("parallel"),
        name="final_norm",
    )(x, g.reshape(1, d))


def _hgrn2_kernel(zq_ref, zf_ref, zi_ref, zg_ref, lb_ref, ng_ref, tri_ref, o_ref,
                  st_ref, q_s, k_s, v_s, cum_s, o_s, dec_s, qd_s, kd_s):
    c_len = HGRN_C
    heads, t_len, hd = q_s.shape

    @pl.when(pl.program_id(0) == 0)
    def _():
        st_ref[...] = jnp.zeros_like(st_ref)

    lb = lb_ref[...]
    f = lb + (1.0 - lb) * _sigmoid(zf_ref[...])
    lf = jnp.log(f)
    lf_hi = lf.astype(BF16)
    lf_lo = (lf - lf_hi.astype(F32)).astype(BF16)
    tri = tri_ref[...]
    cum = (jnp.dot(tri, lf_hi, preferred_element_type=F32) + jnp.dot(tri, lf_lo, preferred_element_type=F32))
    cum3 = cum.reshape(t_len // c_len, c_len, heads * hd)
    last3 = cum3[:, c_len - 1:c_len, :]
    last = jnp.broadcast_to(last3, cum3.shape).reshape(t_len, heads * hd)
    dec = jnp.exp(last3.reshape(t_len // c_len, heads * hd))
    q = _silu(zq_ref[...])
    k = 1.0 - f
    qd = (q * jnp.exp(cum)).astype(BF16)
    kd = (k * jnp.exp(last - cum)).astype(BF16)
    v = zi_ref[...]
    cum2 = cum * math.log2(math.e)
    for h in range(heads):
        sl = slice(h * hd, (h + 1) * hd)
        q_s[h] = q[:, sl]
        k_s[h] = k[:, sl]
        v_s[h] = v[:, sl]
        cum_s[h] = cum2[:, sl]
        dec_s[h] = dec[:, sl]
        qd_s[h] = qd[:, sl]
        kd_s[h] = kd[:, sl]

    half = c_len // 2
    row = lax.broadcasted_iota(jnp.int32, (half, hd), 0)
    lane = lax.broadcasted_iota(jnp.int32, (half, hd), 1)

    def body(c, carry):
        r0 = pl.multiple_of(c * c_len, c_len)
        rows = pl.ds(r0, c_len)
        for h in range(heads):
            q_c = q_s[h, rows, :]
            k_c = k_s[h, rows, :]
            cm = cum_s[h, rows, :]
            m_lo = jnp.zeros((half, hd), F32)
            m_hi = jnp.zeros((half, hd), F32)
            for s in range(c_len):
                ks = k_c[s:s + 1, :]
                cs = cm[s:s + 1, :]
                if s < half:
                    w = q_c[:half] * (ks * jnp.exp2(jnp.minimum(cm[:half] - cs, 0.0)))
                    m_lo = jnp.where(lane == s, jnp.sum(w, axis=1, keepdims=True), m_lo)
                w = q_c[half:] * (ks * jnp.exp2(jnp.minimum(cm[half:] - cs, 0.0)))
                m_hi = jnp.where(lane == s, jnp.sum(w, axis=1, keepdims=True), m_hi)
            m_lo = jnp.where(row >= lane, m_lo, 0.0)
            m_hi = jnp.where(row + half >= lane, m_hi, 0.0)
            scores = jnp.concatenate([m_lo, m_hi], axis=0)[:, :c_len].astype(BF16)
            v_c = v_s[h, rows, :].astype(BF16)
            st = st_ref[h]
            o = (jnp.dot(scores, v_c, preferred_element_type=F32)
                 + lax.dot_general(qd_s[h, rows, :], st.astype(BF16), (((1,), (1,)), ((), ())),
                                   preferred_element_type=F32))
            o_s[h, rows, :] = o
            upd = lax.dot_general(v_c, kd_s[h, rows, :], (((0,), (0,)), ((), ())),
                                  preferred_element_type=F32)
            st_ref[h] = st * dec_s[h, pl.ds(c, 1), :] + upd
        return carry

    lax.fori_loop(0, t_len // c_len, body, 0, unroll=8)
    ng = ng_ref[...]
    sg = _silu(zg_ref[...])
    for h in range(heads):
        sl = slice(h * hd, (h + 1) * hd)
        o = o_s[h]
        o = o * lax.rsqrt(jnp.mean(o * o, axis=-1, keepdims=True) + EPS)
        o_ref[:, sl] = (o * ng[:, sl] * sg[:, sl]).astype(BF16)


def hgrn2(z, col0, lb, norm_g, casts=()):
    s = z.shape[0]
    t_len = min(HGRN_T, s)
    heads = W_MIX // A_HEAD_DIM
    cb = col0 // W_MIX
    r = jnp.arange(t_len)
    same = r[:, None] // HGRN_C == r[None, :] // HGRN_C
    tri = (same & (r[None, :] <= r[:, None])).astype(BF16)

    def zspec(k):
        return pl.BlockSpec((t_len, W_MIX), lambda i: (i, cb + k))

    vec = pl.BlockSpec((1, W_MIX), lambda i: (0, 0))
    sq = pl.BlockSpec((t_len, t_len), lambda i: (0, 0))
    per_head = lambda dt: pltpu.VMEM((heads, t_len, A_HEAD_DIM), dt)
    return _mixer_call(
        _hgrn2_kernel, s // t_len, lambda i: i,
        grid=(s // t_len,),
        in_specs=[zspec(0), zspec(1), zspec(2), zspec(3), vec, vec, sq],
        out_spec=pl.BlockSpec((t_len, W_MIX), lambda i: (i, 0)),
        out_shape=jax.ShapeDtypeStruct((s, W_MIX), BF16),
        scratch_shapes=[pltpu.VMEM((heads, A_HEAD_DIM, A_HEAD_DIM), F32)]
        + [per_head(F32)] * 5 + [pltpu.VMEM((heads, t_len // HGRN_C, A_HEAD_DIM), F32)] + [per_head(BF16)] * 2,
        sem=("arbitrary",), name="hgrn2",
        args=(z, z, z, z, lb.reshape(1, W_MIX), norm_g.reshape(1, W_MIX), tri), casts=casts)


def _s5_kernel(u_ref, wb_ref, pw_ref, wc_ref, d_ref, gw_ref, gb_ref, o_ref,
               carry_ref, bu_s, y_s):
    i = pl.program_id(0)
    j = pl.program_id(1)
    n_slab = pl.num_programs(1)
    t_len = bu_s.shape[0]
    half = bu_s.shape[1] // 2

    @pl.when(i == 0)
    def _():
        carry_ref[j] = jnp.zeros((SUBLANES, 2 * half), F32)

    u = u_ref[...]
    bu_s[...] = jnp.dot(u.astype(BF16), wb_ref[0], preferred_element_type=F32)
    p8_re, p8_im = pw_ref[0, 0:8, :half], pw_ref[0, 0:8, half:]

    def body(b, carry):
        c_re, c_im = carry
        r0 = pl.multiple_of(b * SUBLANES, SUBLANES)
        blk = bu_s[pl.ds(r0, SUBLANES), :]
        x_re, x_im = blk[:, :half], blk[:, half:]
        for step, k in enumerate((1, 2, 4)):
            a_re = pw_ref[0, 8 * (step + 1):8 * (step + 2), :half]
            a_im = pw_ref[0, 8 * (step + 1):8 * (step + 2), half:]
            s_re = pltpu.roll(x_re, k, 0)
            s_im = pltpu.roll(x_im, k, 0)
            x_re, x_im = (x_re + a_re * s_re - a_im * s_im,
                          x_im + a_re * s_im + a_im * s_re)
        x_re, x_im = (x_re + p8_re * c_re - p8_im * c_im,
                      x_im + p8_re * c_im + p8_im * c_re)
        bu_s[pl.ds(r0, SUBLANES), :] = jnp.concatenate([x_re, x_im], axis=1)
        n_re = jnp.broadcast_to(x_re[SUBLANES - 1:SUBLANES, :], x_re.shape)
        n_im = jnp.broadcast_to(x_im[SUBLANES - 1:SUBLANES, :], x_im.shape)
        return n_re, n_im

    c0 = carry_ref[j]
    c_re, c_im = lax.fori_loop(0, t_len // SUBLANES, body, (c0[:, :half], c0[:, half:]), unroll=4)
    carry_ref[j] = jnp.concatenate([c_re, c_im], axis=1)

    y = jnp.dot(bu_s[...].astype(BF16), wc_ref[0], preferred_element_type=F32)
    y = _gelu_tanh(y + d_ref[0] * u)
    y_s[j] = y

    @pl.when(j == n_slab - 1)
    def _():
        yf = jnp.concatenate([y_s[s] for s in range(y_s.shape[0])], axis=1)
        gate = jnp.dot(yf.astype(BF16), gw_ref[...], preferred_element_type=F32) + gb_ref[...]
        o_ref[...] = (yf * _sigmoid(gate)).astype(BF16)


def s5(z, col0, lam_re, lam_im, log_dt, b_re, b_im, c_re, c_im, d_skip, glu_w, glu_b, casts=()):
    s = z.shape[0]
    t_len = min(S5_T, s)
    groups, n_state = lam_re.shape
    gps = S5_SLAB // S5_GROUP
    n_slab = groups // gps
    half = gps * n_state
    cb = col0 // S5_SLAB
    lam_re = lam_re.astype(F32)
    lam_im = lam_im.astype(F32)
    dt = jnp.exp(log_dt.astype(F32))[:, None]
    mag = jnp.exp(lam_re * dt)
    ang = lam_im * dt
    ab_re = mag * jnp.cos(ang)
    ab_im = mag * jnp.sin(ang)
    den = lam_re * lam_re + lam_im * lam_im
    num_re = ab_re - 1.0
    coef_re = (num_re * lam_re + ab_im * lam_im) / den
    coef_im = (ab_im * lam_re - num_re * lam_im) / den
    br = b_re.astype(F32)
    bi = b_im.astype(F32)
    bb_re = coef_re[..., None] * br - coef_im[..., None] * bi
    bb_im = coef_re[..., None] * bi + coef_im[..., None] * br
    eye = jnp.eye(gps, dtype=F32)

    def blockdiag_in(bb):
        t = bb.reshape(n_slab, gps, n_state, S5_GROUP)
        return jnp.einsum('sgnp,gh->sgphn', t, eye).reshape(n_slab, gps * S5_GROUP, gps * n_state)

    wb = jnp.concatenate([blockdiag_in(bb_re), blockdiag_in(bb_im)], axis=-1).astype(BF16)

    def blockdiag_out(cc):
        t = cc.reshape(n_slab, gps, S5_GROUP, n_state)
        return jnp.einsum('sgpn,gh->sgnhp', t, eye).reshape(n_slab, gps * n_state, gps * S5_GROUP)

    wc = jnp.concatenate([blockdiag_out(c_re.astype(F32)), -blockdiag_out(c_im.astype(F32))],
                         axis=1).astype(BF16)
    r8 = jnp.arange(SUBLANES)
    expo = jnp.concatenate([r8 + 1.0] + [jnp.full((SUBLANES,), float(k)) for k in (1, 2, 4)]).astype(F32)
    keep = jnp.concatenate([jnp.ones((SUBLANES,), F32)] + [(r8 >= k).astype(F32) for k in (1, 2, 4)])
    expo = expo[:, None, None]
    p_mag = jnp.exp(expo * (lam_re * dt)[None]) * keep[:, None, None]
    p_re = (p_mag * jnp.cos(expo * ang[None])).reshape(S5_PW_ROWS, n_slab, half)
    p_im = (p_mag * jnp.sin(expo * ang[None])).reshape(S5_PW_ROWS, n_slab, half)
    pw = jnp.moveaxis(jnp.concatenate([p_re, p_im], axis=-1), 1, 0)

    return _mixer_call(
        _s5_kernel, (s // t_len) * n_slab, lambda i, j: i * n_slab + j,
        grid=(s // t_len, n_slab),
        in_specs=[
            pl.BlockSpec((t_len, S5_SLAB), lambda i, j: (i, cb + j)),
            pl.BlockSpec((1, S5_SLAB, 2 * half), lambda i, j: (j, 0, 0)),
            pl.BlockSpec((1, S5_PW_ROWS, 2 * half), lambda i, j: (j, 0, 0)),
            pl.BlockSpec((1, 2 * half, S5_SLAB), lambda i, j: (j, 0, 0)),
            pl.BlockSpec((1, 1, S5_SLAB), lambda i, j: (j, 0, 0)),
            pl.BlockSpec((W_MIX, W_MIX), lambda i, j: (0, 0)),
            pl.BlockSpec((1, W_MIX), lambda i, j: (0, 0)),
        ],
        out_spec=pl.BlockSpec((t_len, W_MIX), lambda i, j: (i, 0)),
        out_shape=jax.ShapeDtypeStruct((s, W_MIX), BF16),
        scratch_shapes=[pltpu.VMEM((n_slab, SUBLANES, 2 * half), F32),
                        pltpu.VMEM((t_len, 2 * half), F32),
                        pltpu.VMEM((n_slab, t_len, S5_SLAB), F32)],
        sem=("arbitrary", "arbitrary"), name="s5",
        args=(z, wb, pw, wc, d_skip.astype(F32).reshape(n_slab, 1, S5_SLAB), glu_w.astype(BF16),
              glu_b.astype(F32).reshape(1, W_MIX)), casts=casts)


def _retention_kernel(zq_ref, zk_ref, zv_ref, zg_ref, cos_ref, sin_ref, dmat_ref, qdec_ref, kdec_ref,
                      cdec_ref, ng_ref, o_ref, st_ref):
    @pl.when(pl.program_id(0) == 0)
    def _():
        st_ref[...] = jnp.zeros_like(st_ref)

    heads, hd, _ = st_ref.shape
    half = hd // 2
    cos = cos_ref[...]
    sin = sin_ref[...]

    def rope(t):
        t1, t2 = t[:, :half], t[:, half:]
        return jnp.concatenate([t1 * cos - t2 * sin, t1 * sin + t2 * cos], axis=1)

    for h in range(heads):
        sl = slice(h * hd, (h + 1) * hd)
        q = rope(zq_ref[:, sl])
        k = rope(zk_ref[:, sl]) * hd ** -0.5
        v = zv_ref[:, sl].astype(BF16)
        qb = q.astype(BF16)
        scores = lax.dot_general(qb, k.astype(BF16), (((1,), (1,)), ((), ())),
                                 preferred_element_type=F32) * dmat_ref[h]
        st = st_ref[h]
        o = (jnp.dot(scores.astype(BF16), v, preferred_element_type=F32)
             + jnp.dot(qb, st.astype(BF16), preferred_element_type=F32) * qdec_ref[h])
        kd = (k * kdec_ref[h]).astype(BF16)
        st_ref[h] = cdec_ref[h] * st + lax.dot_general(kd, v, (((0,), (0,)), ((), ())),
                                                       preferred_element_type=F32)
        mu = jnp.mean(o, axis=-1, keepdims=True)
        oc = o - mu
        var = jnp.mean(oc * oc, axis=-1, keepdims=True)
        o = oc * lax.rsqrt(var + EPS)
        o_ref[:, sl] = (o * ng_ref[:, sl] * _silu(zg_ref[:, sl])).astype(BF16)


def retention(z, col0, norm_g, casts=()):
    s = z.shape[0]
    t_len = min(RET_T, s)
    hd = W_MIX // C_HEADS
    pos = jnp.arange(s, dtype=F32)
    inv_freq = ROPE_THETA ** (-jnp.arange(0, hd, 2, dtype=F32) / hd)
    ang = pos[:, None] * inv_freq[None, :]
    cos = jnp.cos(ang)
    sin = jnp.sin(ang)
    log_gamma = jnp.log(1.0 - 2.0 ** (-5.0 - jnp.arange(C_HEADS, dtype=F32)))
    idx = jnp.arange(t_len, dtype=F32)
    rel = idx[:, None] - idx[None, :]
    dmat = jnp.where(rel[None] >= 0, jnp.exp(jnp.maximum(rel, 0.0)[None] * log_gamma[:, None, None]), 0.0)
    qdec = jnp.exp((idx + 1.0)[None, :] * log_gamma[:, None])[..., None]
    kdec = jnp.exp((t_len - 1.0 - idx)[None, :] * log_gamma[:, None])[..., None]
    cdec = jnp.broadcast_to(jnp.exp(t_len * log_gamma)[:, None, None], (C_HEADS, 1, hd))

    cb = col0 // W_MIX

    def zspec(k):
        return pl.BlockSpec((t_len, W_MIX), lambda i: (i, cb + k))

    tab = pl.BlockSpec((t_len, hd // 2), lambda i: (i, 0))
    full = lambda shape: pl.BlockSpec(shape, lambda i: (0,) * len(shape))
    return _mixer_call(
        _retention_kernel, s // t_len, lambda i: i,
        grid=(s // t_len,),
        in_specs=[zspec(0), zspec(1), zspec(2), zspec(3), tab, tab,
                  full((C_HEADS, t_len, t_len)), full((C_HEADS, t_len, 1)), full((C_HEADS, t_len, 1)),
                  full((C_HEADS, 1, hd)), full((1, W_MIX))],
        out_spec=pl.BlockSpec((t_len, W_MIX), lambda i: (i, 0)),
        out_shape=jax.ShapeDtypeStruct((s, W_MIX), BF16),
        scratch_shapes=[pltpu.VMEM((C_HEADS, hd, hd), F32)],
        sem=("arbitrary",), name="retention",
        args=(z, z, z, z, cos, sin, dmat, qdec, kdec, cdec, norm_g.reshape(1, W_MIX)), casts=casts)


def _rglru_kernel(zg_ref, zx_ref, cw_ref, cb_ref, wa_ref, ba_ref, wx_ref, bx_ref, sp_ref, o_ref,
                  xbuf, h_ref, a_s, u_s):
    t_len = zx_ref.shape[0]

    @pl.when(pl.program_id(0) == 0)
    def _():
        xbuf[0:SUBLANES, :] = jnp.zeros((SUBLANES, W_MIX), F32)
        h_ref[...] = jnp.zeros_like(h_ref)

    xbuf[SUBLANES:, :] = zx_ref[...]
    xc = cb_ref[...]
    for tap in range(CONV_WIDTH):
        off = SUBLANES - (CONV_WIDTH - 1) + tap
        xc = xc + xbuf[off:off + t_len, :] * cw_ref[tap:tap + 1, :]
    xbuf[0:SUBLANES, :] = xbuf[t_len:t_len + SUBLANES, :]
    xcb = xc.astype(BF16)
    n_blk = W_MIX // D_BLOCK
    pre_r = jnp.concatenate(
        [jnp.dot(xcb[:, b * D_BLOCK:(b + 1) * D_BLOCK], wa_ref[b], preferred_element_type=F32)
         for b in range(n_blk)], axis=1)
    pre_i = jnp.concatenate(
        [jnp.dot(xcb[:, b * D_BLOCK:(b + 1) * D_BLOCK], wx_ref[b], preferred_element_type=F32)
         for b in range(n_blk)], axis=1)
    r = _sigmoid(pre_r + ba_ref[...])
    gi = _sigmoid(pre_i + bx_ref[...])
    log_a = -RG_C * r * sp_ref[...]
    a = jnp.exp(log_a)
    a_s[...] = a
    u_s[...] = jnp.sqrt(1.0 - a * a) * (gi * xc)
    row = lax.broadcasted_iota(jnp.int32, (SUBLANES, W_MIX), 0)

    def body(b, h):
        r0 = pl.multiple_of(b * SUBLANES, SUBLANES)
        aa = a_s[pl.ds(r0, SUBLANES), :]
        uu = u_s[pl.ds(r0, SUBLANES), :]
        for k in (1, 2, 4):
            us = jnp.where(row >= k, pltpu.roll(uu, k, 0), 0.0)
            as_ = jnp.where(row >= k, pltpu.roll(aa, k, 0), 1.0)
            uu = uu + aa * us
            aa = aa * as_
        hh = uu + aa * h
        u_s[pl.ds(r0, SUBLANES), :] = hh
        return jnp.broadcast_to(hh[SUBLANES - 1:SUBLANES, :], hh.shape)

    h_ref[...] = lax.fori_loop(0, t_len // SUBLANES, body, h_ref[...], unroll=4)
    o_ref[...] = (_gelu_tanh(zg_ref[...]) * u_s[...]).astype(BF16)


def rglru(z, col0, conv_w, conv_b, w_a, b_a, w_x, b_x, lam, casts=()):
    s = z.shape[0]
    t_len = min(RG_T, s)
    cb = col0 // W_MIX
    sp = jax.nn.softplus(-lam.astype(F32)).reshape(1, W_MIX)
    row = lambda a: a.astype(F32).reshape(1, W_MIX)
    full = lambda shape: pl.BlockSpec(shape, lambda i: (0,) * len(shape))
    n_blk = W_MIX // D_BLOCK
    return _mixer_call(
        _rglru_kernel, s // t_len, lambda i: i,
        grid=(s // t_len,),
        in_specs=[pl.BlockSpec((t_len, W_MIX), lambda i: (i, cb)),
                  pl.BlockSpec((t_len, W_MIX), lambda i: (i, cb + 1)),
                  full((CONV_WIDTH, W_MIX)), full((1, W_MIX)),
                  full((n_blk, D_BLOCK, D_BLOCK)), full((1, W_MIX)),
                  full((n_blk, D_BLOCK, D_BLOCK)), full((1, W_MIX)), full((1, W_MIX))],
        out_spec=pl.BlockSpec((t_len, W_MIX), lambda i: (i, 0)),
        out_shape=jax.ShapeDtypeStruct((s, W_MIX), BF16),
        scratch_shapes=[pltpu.VMEM((t_len + SUBLANES, W_MIX), F32),
                        pltpu.VMEM((SUBLANES, W_MIX), F32),
                        pltpu.VMEM((t_len, W_MIX), F32),
                        pltpu.VMEM((t_len, W_MIX), F32)],
        sem=("arbitrary",), name="rglru",
        args=(z, z, conv_w.astype(F32), row(conv_b), w_a.astype(BF16), row(b_a), w_x.astype(BF16), row(b_x), sp),
        casts=casts)


def kernel(x, norm_mix_g, norm_ffn_g, final_norm_g, w_in, w_out, hgrn_lb_logits, hgrn_norm_g, s5_lambda_re, s5_lambda_im, s5_log_dt, s5_b_re, s5_b_im, s5_c_re, s5_c_im, s5_d, s5_glu_w, s5_glu_b, ret_norm_g, rg_conv_w, rg_conv_b, rg_w_a, rg_b_a, rg_w_x, rg_b_x, rg_lambda, ffn_w1, ffn_w3, ffn_w2, router_w, moe_w1, moe_w3, moe_w2):
    b_, s_, d_ = x.shape
    depth = w_in.shape[0]
    xs = x.reshape(b_ * s_, d_).astype(F32)
    lb_p = jax.nn.softmax(hgrn_lb_logits.astype(F32), axis=0)
    lb_all = jnp.cumsum(lb_p, axis=0) - lb_p[0]
    col_a, col_b, col_c, col_d = 0, 4 * W_MIX, 5 * W_MIX, 9 * W_MIX
    w_in_b = w_in[0].astype(BF16)
    for layer in range(depth):
        m = layer // 2
        dense = layer % 2 == 0
        ch = (ffn_w1, ffn_w3, ffn_w2) if dense else (moe_w1, moe_w3, moe_w2)
        nxt = ((w_in, layer + 1),) if layer + 1 < depth else ()
        z, _ = norm_matmul(xs, norm_mix_g[layer], w_in_b)
        o_a, (w1_b,) = hgrn2(z, col_a, lb_all[layer], hgrn_norm_g[layer], casts=((ch[0], m),))
        o_b, (w3_b, w2_b) = s5(z, col_b, s5_lambda_re[layer], s5_lambda_im[layer], s5_log_dt[layer],
                               s5_b_re[layer], s5_b_im[layer], s5_c_re[layer], s5_c_im[layer],
                               s5_d[layer], s5_glu_w[layer], s5_glu_b[layer], casts=((ch[1], m), (ch[2], m)))
        o_c, _ = retention(z, col_c, ret_norm_g[layer])
        o_d, rest = rglru(z, col_d, rg_conv_w[layer], rg_conv_b[layer], rg_w_a[layer], rg_b_a[layer],
                          rg_w_x[layer], rg_b_x[layer], rg_lambda[layer], casts=((w_out, layer),) + nxt)
        xs, _ = out_proj((o_a, o_b, o_c, o_d), rest[0], xs)
        if nxt:
            w_in_b = rest[1]
        if dense:
            xs = ffn(xs, norm_ffn_g[layer], w1_b, w3_b, w2_b)
        else:
            last = layer == depth - 1
            xs = moe(xs, norm_ffn_g[layer], router_w[m], w1_b, w3_b, w2_b, final_norm_g if last else None)
            if last:
                return xs.reshape(b_, s_, d_)
    return final_norm(xs, final_norm_g).reshape(b_, s_, d_)
```
